```python
import jax, jax.numpy as jnp
from jax import lax
import numpy as np

D_MODEL = 2048
BATCH = 4
SEQ = 2048
DEPTH = 1

CHUNK = 64
N_META = 16
D_MIX = D_MODEL
D_POOL = D_MIX // 2
POOL_WINDOWS = (2, 4, 8, 16)
N_POOL_GROUPS = len(POOL_WINDOWS)
POOL_GROUP = D_POOL // N_POOL_GROUPS
D_ATT = D_MIX - D_POOL
HEAD_DIM = 128
N_HEADS = D_ATT // HEAD_DIM
D_IN = D_POOL + 3 * D_ATT + N_HEADS
D_FF = ((8 * D_MODEL // 3 + 255) // 256) * 256
Q_BLOCK = 128
EPS = 1e-6

kernel_name = "hymba_pool_fox_macaron_block"


def rmsnorm(x, g):
    xf = x.astype(jnp.float32)
    y = xf * lax.rsqrt(jnp.mean(xf * xf, axis=-1, keepdims=True) + EPS)
    return (y * g.astype(jnp.float32)).astype(x.dtype)


def swiglu(x, w_gate, w_up, w_down):
    return (jax.nn.silu(x @ w_gate) * (x @ w_up)) @ w_down


def pool_mixer(p, pool_w, pool_scale):
    B, L, _ = p.shape
    pg = p.reshape(B, L, N_POOL_GROUPS, POOL_GROUP)
    c = jnp.cumsum(pg.astype(jnp.float32), axis=1)
    c = jnp.pad(c, ((0, 0), (1, 0), (0, 0), (0, 0)))
    win = jnp.array(POOL_WINDOWS, dtype=jnp.int32)
    end = jnp.arange(1, L + 1, dtype=jnp.int32)[:, None]
    start = jnp.maximum(end - win[None, :], 0)
    gidx = jnp.arange(N_POOL_GROUPS, dtype=jnp.int32)[None, :]
    window_sum = c[:, end, gidx] - c[:, start, gidx]
    count = (end - start).astype(jnp.float32)[None, :, :, None]
    pooled = (window_sum / count - pg.astype(jnp.float32)).astype(p.dtype)
    mixed = jnp.einsum('blgc,gcd->blgd', pooled, pool_w)
    return mixed.reshape(B, L, D_POOL) * pool_scale


def fox_attention(q, k, v, log_f):
    B, L, H, Dh = q.shape
    scale = 1.0 / np.sqrt(Dh).astype(np.float32)
    cum = jnp.cumsum(log_f, axis=-1)
    n_blocks = -(-L // Q_BLOCK)
    Lp = n_blocks * Q_BLOCK
    qp = jnp.pad(q, ((0, 0), (0, Lp - L), (0, 0), (0, 0)))
    cqp = jnp.pad(cum, ((0, 0), (0, 0), (0, Lp - L)))
    qb = qp.reshape(B, n_blocks, Q_BLOCK, H, Dh).transpose(1, 0, 2, 3, 4)
    cqb = cqp.reshape(B, H, n_blocks, Q_BLOCK).transpose(2, 0, 1, 3)
    kpos = jnp.arange(L, dtype=jnp.int32)

    def one_block(args):
        qi, cqi, bi = args
        qpos = bi * Q_BLOCK + jnp.arange(Q_BLOCK, dtype=jnp.int32)
        s = jnp.einsum('bqhd,bkhd->bhqk', qi, k).astype(jnp.float32) * scale
        s = s + (cqi[:, :, :, None] - cum[:, :, None, :])
        s = jnp.where(qpos[:, None] >= kpos[None, :], s, -jnp.inf)
        pr = jax.nn.softmax(s, axis=-1)
        return jnp.einsum('bhqk,bkhd->bqhd', pr.astype(v.dtype), v)

    out = lax.map(one_block, (qb, cqb, jnp.arange(n_blocks, dtype=jnp.int32)))
    out = out.transpose(1, 0, 2, 3, 4).reshape(B, Lp, H * Dh)
    return out[:, :L]


def setup_inputs(seed: int = 0) -> dict:
    key = jax.random.key(seed)
    ks = jax.random.split(key, 20)
    f32 = jnp.float32

    def nrm(k, shape, s):
        return jax.random.normal(k, shape, f32) * s

    def gain(k, shape):
        return 1.0 + 0.05 * jax.random.normal(k, shape, f32)

    return {
        "x": jax.random.normal(ks[0], (BATCH, SEQ, D_MODEL), f32),
        "meta_tokens": nrm(ks[1], (N_META, D_MODEL), 1.0),
        "ffn1_norm": gain(ks[2], (DEPTH, D_MODEL)),
        "ffn1_w_gate": nrm(ks[3], (DEPTH, D_MODEL, D_FF), D_MODEL ** -0.5),
        "ffn1_w_up": nrm(ks[4], (DEPTH, D_MODEL, D_FF), D_MODEL ** -0.5),
        "ffn1_w_down": nrm(ks[5], (DEPTH, D_FF, D_MODEL), D_FF ** -0.5),
        "mix_norm": gain(ks[6], (DEPTH, D_MODEL)),
        "w_in": nrm(ks[7], (DEPTH, D_MODEL, D_IN), D_MODEL ** -0.5),
        "b_forget": jax.random.uniform(ks[8], (DEPTH, N_HEADS), f32, minval=1.0, maxval=5.0),
        "q_norm": gain(ks[9], (DEPTH, HEAD_DIM)),
        "k_norm": gain(ks[10], (DEPTH, HEAD_DIM)),
        "pool_w": nrm(ks[11], (DEPTH, N_POOL_GROUPS, POOL_GROUP, POOL_GROUP), POOL_GROUP ** -0.5),
        "pool_scale": 1.0 + 0.1 * jax.random.normal(ks[12], (DEPTH, D_POOL), f32),
        "w_out": nrm(ks[13], (DEPTH, D_MIX, D_MODEL), D_MIX ** -0.5),
        "ffn2_norm": gain(ks[14], (DEPTH, D_MODEL)),
        "ffn2_w_gate": nrm(ks[15], (DEPTH, D_MODEL, D_FF), D_MODEL ** -0.5),
        "ffn2_w_up": nrm(ks[16], (DEPTH, D_MODEL, D_FF), D_MODEL ** -0.5),
        "ffn2_w_down": nrm(ks[17], (DEPTH, D_FF, D_MODEL), D_FF ** -0.5),
    }


def reference(x, meta_tokens, ffn1_norm, ffn1_w_gate, ffn1_w_up, ffn1_w_down, mix_norm, w_in,
              b_forget, q_norm, k_norm, pool_w, pool_scale, w_out, ffn2_norm, ffn2_w_gate,
              ffn2_w_up, ffn2_w_down):
    B = x.shape[0]
    meta = jnp.broadcast_to(meta_tokens[None].astype(x.dtype), (B, N_META, D_MODEL))
    h = jnp.concatenate([meta, x], axis=1)
    L = h.shape[1]
    for i in range(DEPTH):
        h = h + 0.5 * swiglu(rmsnorm(h, ffn1_norm[i]), ffn1_w_gate[i], ffn1_w_up[i], ffn1_w_down[i])

        u = rmsnorm(h, mix_norm[i])
        z = u @ w_in[i]
        o = D_POOL
        p = z[..., :o]
        q = z[..., o:o + D_ATT].reshape(B, L, N_HEADS, HEAD_DIM)
        k = z[..., o + D_ATT:o + 2 * D_ATT].reshape(B, L, N_HEADS, HEAD_DIM)
        v = z[..., o + 2 * D_ATT:o + 3 * D_ATT].reshape(B, L, N_HEADS, HEAD_DIM)
        f_logit = z[..., o + 3 * D_ATT:]

        pool_out = pool_mixer(p, pool_w[i], pool_scale[i])

        q = rmsnorm(q, q_norm[i])
        k = rmsnorm(k, k_norm[i])
        log_f = jax.nn.log_sigmoid(f_logit.astype(jnp.float32) + b_forget[i].astype(jnp.float32))
        att_out = fox_attention(q, k, v, log_f.transpose(0, 2, 1))

        mix = jnp.concatenate([pool_out, att_out.astype(pool_out.dtype)], axis=-1)
        h = h + mix @ w_out[i]

        h = h + 0.5 * swiglu(rmsnorm(h, ffn2_norm[i]), ffn2_w_gate[i], ffn2_w_up[i], ffn2_w_down[i])
    return h[:, N_META:]
```

```python
import functools

import jax
import jax.numpy as jnp
from jax import lax
from jax.experimental import pallas as pl
from jax.experimental.pallas import tpu as pltpu

F32 = jnp.float32
BF16 = jnp.bfloat16

EPS = 1e-6
N_META = 16
POOL_WINDOWS = (2, 4, 8, 16)
HEAD_DIM = 128
LANES = 128
META_PAD = 128
NEG_INF = float("-inf")

VMEM_LIMIT = 60 * 1024 * 1024


def _cparams(n_axes):
    return pltpu.CompilerParams(
        dimension_semantics=("arbitrary",) * n_axes,
        vmem_limit_bytes=VMEM_LIMIT,
    )


def _rmsnorm_rows(x, gain):
    ms = jnp.mean(x * x, axis=-1, keepdims=True)
    return x * lax.rsqrt(ms + EPS) * gain


def _dot(a, b):
    return jnp.dot(a, b, preferred_element_type=F32)


def _dot_nt(a, b):
    return lax.dot_general(a, b, (((1,), (1,)), ((), ())), preferred_element_type=F32)


def _ffn_kernel(*refs, tm, n_extra, row_chunk, col_chunk):
    if n_extra:
        x_ref, e_ref, g_ref, wg_ref, wu_ref, wd_ref, o_ref, oe_ref, xn_ref = refs
    else:
        x_ref, g_ref, wg_ref, wu_ref, wd_ref, o_ref, xn_ref = refs
    f = pl.program_id(1)
    d = x_ref.shape[1]

    @pl.when(f == 0)
    def _():
        gain = g_ref[...]
        for r in range(0, tm, row_chunk):
            xn_ref[r:r + row_chunk, :] = _rmsnorm_rows(x_ref[r:r + row_chunk, :], gain).astype(BF16)
        if n_extra:
            xn_ref[tm:tm + n_extra, :] = _rmsnorm_rows(e_ref[...], gain).astype(BF16)
            oe_ref[...] = jnp.zeros_like(oe_ref)
        o_ref[...] = jnp.zeros_like(o_ref)

    xn = xn_ref[...]
    gate = _dot(xn, wg_ref[...])
    up = _dot(xn, wu_ref[...])
    hmid = (gate / (1.0 + jnp.exp(-gate)) * up).astype(BF16)
    for c in range(0, d, col_chunk):
        part = _dot(hmid, wd_ref[:, c:c + col_chunk])
        o_ref[:, c:c + col_chunk] += part[:tm]
        if n_extra:
            oe_ref[:, c:c + col_chunk] += part[tm:]

    @pl.when(f == pl.num_programs(1) - 1)
    def _():
        for r in range(0, tm, row_chunk):
            o_ref[r:r + row_chunk, :] = x_ref[r:r + row_chunk, :] + 0.5 * o_ref[r:r + row_chunk, :]
        if n_extra:
            oe_ref[...] = e_ref[...] + 0.5 * oe_ref[...]


def _ffn(x, extra, gain, wg, wu, wd, *, tm=1024, tf=512):
    n, d = x.shape
    dff = wg.shape[1]
    n_extra = 0 if extra is None else extra.shape[0]
    assert n % tm == 0 and dff % tf == 0
    grid = (n // tm, dff // tf)
    row_spec = pl.BlockSpec((tm, d), lambda i, f: (i, 0))
    in_specs = [row_spec]
    args = [x]
    if n_extra:
        in_specs.append(pl.BlockSpec((n_extra, d), lambda i, f: (0, 0)))
        args.append(extra)
    in_specs += [
        pl.BlockSpec((1, d), lambda i, f: (0, 0)),
        pl.BlockSpec((d, tf), lambda i, f: (0, f)),
        pl.BlockSpec((d, tf), lambda i, f: (0, f)),
        pl.BlockSpec((tf, d), lambda i, f: (f, 0)),
    ]
    args += [gain, wg, wu, wd]
    out_shape = [jax.ShapeDtypeStruct((n, d), F32)]
    out_specs = [row_spec]
    if n_extra:
        out_shape.append(jax.ShapeDtypeStruct((n_extra, d), F32))
        out_specs.append(pl.BlockSpec((n_extra, d), lambda i, f: (0, 0)))
    kern = functools.partial(_ffn_kernel, tm=tm, n_extra=n_extra, row_chunk=256, col_chunk=512)
    outs = pl.pallas_call(
        kern,
        grid=grid,
        in_specs=in_specs,
        out_specs=out_specs,
        out_shape=out_shape,
        scratch_shapes=[pltpu.VMEM((tm + n_extra, d), BF16)],
        compiler_params=_cparams(2),
        name="ffn_extra" if n_extra else "ffn",
    )(*args)
    return (outs[0], outs[1]) if n_extra else (outs[0], None)


def _log_sigmoid(x):
    return jnp.minimum(x, 0.0) - jnp.log1p(jnp.exp(-jnp.abs(x)))


def _split3(x):
    hi = x.astype(BF16)
    r1 = x - hi.astype(F32)
    mid = r1.astype(BF16)
    lo = (r1 - mid.astype(F32)).astype(BF16)
    return hi, mid, lo


def _cumsum_rows(x, tri):
    hi, mid, lo = _split3(x)
    return _dot(tri, hi) + _dot(tri, mid) + _dot(tri, lo)


def _cumsum_lanes(x, triu):
    hi, mid, lo = _split3(x)
    return _dot(hi, triu) + _dot(mid, triu) + _dot(lo, triu)


def _head_rmsnorm(z, gain, n_heads):
    outs = []
    for h in range(n_heads):
        outs.append(_rmsnorm_rows(z[:, h * HEAD_DIM:(h + 1) * HEAD_DIM], gain))
    return jnp.concatenate(outs, axis=-1)


def _inproj_kernel(h_ref, hm_ref, gn_ref, wp_ref, wq_ref, wk_ref, wv_ref, wf_ref, wft_ref,
                   bcol_ref, brow_ref, qn_ref, kn_ref, pw_ref, ps_ref,
                   pool_ref, q_ref, k_ref, v_ref, cq_ref, ck_ref, km_ref, vm_ref, ckm_ref,
                   u_ref, pe_ref, pmeta_ref, ccol_ref, crow_ref, mcol_ref, mrow_ref,
                   *, tm, tiles_per_batch, n_heads, row_chunk):
    i = pl.program_id(0)
    d_pool = wp_ref.shape[1]
    gain = gn_ref[...]

    @pl.when(i == 0)
    def _():
        u_ref[0:META_PAD, :] = jnp.zeros((META_PAD, u_ref.shape[1]), BF16)
        u_ref[0:N_META, :] = _rmsnorm_rows(hm_ref[...], gain).astype(BF16)
        um = u_ref[0:META_PAD, :]
        row_valid = lax.broadcasted_iota(jnp.int32, (META_PAD, 1), 0) < N_META
        pmeta_ref[...] = _dot(um, wp_ref[...])[0:N_META, :]
        km = _head_rmsnorm(_dot(um, wk_ref[...]), kn_ref[...], n_heads)
        km_ref[...] = jnp.where(row_valid, km, 0.0).astype(BF16)
        vm_ref[...] = _dot(um, wv_ref[...]).astype(BF16)
        lf_col = _log_sigmoid(_dot(um, wf_ref[...]) + bcol_ref[...])
        lf_col = jnp.where(row_valid, lf_col, 0.0)
        mcol_ref[...] = jnp.sum(lf_col, axis=0, keepdims=True)
        lf_row = _log_sigmoid(_dot_nt(wft_ref[...], um) + brow_ref[...])
        lane_valid = lax.broadcasted_iota(jnp.int32, (1, META_PAD), 1) < N_META
        lf_row = jnp.where(lane_valid, lf_row, 0.0)
        r_i = lax.broadcasted_iota(jnp.int32, (META_PAD, META_PAD), 0)
        c_i = lax.broadcasted_iota(jnp.int32, (META_PAD, META_PAD), 1)
        triu = (r_i <= c_i).astype(BF16)
        ckm_ref[...] = _cumsum_lanes(lf_row, triu)
        mrow_ref[...] = jnp.broadcast_to(jnp.sum(lf_row, axis=1, keepdims=True), mrow_ref.shape)

    @pl.when(i % tiles_per_batch == 0)
    def _():
        ccol_ref[...] = mcol_ref[...]
        crow_ref[...] = mrow_ref[...]
        pe_ref[0:N_META, :] = pmeta_ref[...]

    for r in range(0, tm, row_chunk):
        u_ref[r:r + row_chunk, :] = _rmsnorm_rows(h_ref[r:r + row_chunk, :], gain).astype(BF16)
    u = u_ref[0:tm, :]

    pe_ref[N_META:N_META + tm, :] = _dot(u, wp_ref[...])
    gw = d_pool // len(POOL_WINDOWS)
    for g, w in enumerate(POOL_WINDOWS):
        lanes = slice(g * gw, (g + 1) * gw)
        cur = pe_ref[N_META:N_META + tm, lanes]
        wsum = cur
        for j in range(1, w):
            wsum = wsum + pe_ref[N_META - j:N_META - j + tm, lanes]
        pooled = (wsum / float(w) - cur).astype(BF16)
        mixed = _dot(pooled, pw_ref[g]) * ps_ref[:, lanes]
        pool_ref[:, lanes] = mixed.astype(BF16)
    pe_ref[0:N_META, :] = pe_ref[tm:tm + N_META, :]

    q_ref[...] = _head_rmsnorm(_dot(u, wq_ref[...]), qn_ref[...], n_heads).astype(BF16)
    k_ref[...] = _head_rmsnorm(_dot(u, wk_ref[...]), kn_ref[...], n_heads).astype(BF16)
    v_ref[...] = _dot(u, wv_ref[...]).astype(BF16)

    r_i = lax.broadcasted_iota(jnp.int32, (tm, tm), 0)
    c_i = lax.broadcasted_iota(jnp.int32, (tm, tm), 1)
    lf_col = _log_sigmoid(_dot(u, wf_ref[...]) + bcol_ref[...])
    cum_col = _cumsum_rows(lf_col, (c_i <= r_i).astype(BF16)) + ccol_ref[...]
    cq_ref[...] = cum_col
    ccol_ref[...] = cum_col[tm - 1:tm, :]
    lf_row = _log_sigmoid(_dot_nt(wft_ref[...], u) + brow_ref[...])
    cum_row = _cumsum_lanes(lf_row, (r_i <= c_i).astype(BF16)) + crow_ref[:, 0:1]
    ck_ref[0] = cum_row
    crow_ref[...] = jnp.broadcast_to(cum_row[:, tm - 1:tm], crow_ref.shape)


def _const_spec(shape):
    nd = len(shape)
    return pl.BlockSpec(shape, lambda i: (0,) * nd, pipeline_mode=pl.Buffered(1))


def _in_proj(h1, h1m, gn, wp, wq, wk, wv, wf, wft, bcol, brow, qn, kn, pw, ps, *, batch, tm=512):
    n, d = h1.shape
    seq = n // batch
    d_pool = wp.shape[1]
    d_att = wq.shape[1]
    n_heads = d_att // HEAD_DIM
    assert seq % tm == 0 and tm >= META_PAD
    tiles_per_batch = seq // tm
    rows = lambda w: pl.BlockSpec((tm, w), lambda i: (i, 0))
    in_specs = [
        rows(d),
        _const_spec((N_META, d)),
        _const_spec((1, d)),
        _const_spec((d, d_pool)),
        _const_spec((d, d_att)),
        _const_spec((d, d_att)),
        _const_spec((d, d_att)),
        _const_spec((d, LANES)),
        _const_spec((8, d)),
        _const_spec((1, LANES)),
        _const_spec((8, 1)),
        _const_spec((1, HEAD_DIM)),
        _const_spec((1, HEAD_DIM)),
        _const_spec(pw.shape),
        _const_spec((1, d_pool)),
    ]
    out_shape = [
        jax.ShapeDtypeStruct((n, d_pool), BF16),
        jax.ShapeDtypeStruct((n, d_att), BF16),
        jax.ShapeDtypeStruct((n, d_att), BF16),
        jax.ShapeDtypeStruct((n, d_att), BF16),
        jax.ShapeDtypeStruct((n, LANES), F32),
        jax.ShapeDtypeStruct((batch, 8, seq), F32),
        jax.ShapeDtypeStruct((META_PAD, d_att), BF16),
        jax.ShapeDtypeStruct((META_PAD, d_att), BF16),
        jax.ShapeDtypeStruct((8, META_PAD), F32),
    ]
    out_specs = [
        rows(d_pool), rows(d_att), rows(d_att), rows(d_att), rows(LANES),
        pl.BlockSpec((1, 8, tm), lambda i: (i // tiles_per_batch, 0, i % tiles_per_batch)),
        pl.BlockSpec((META_PAD, d_att), lambda i: (0, 0)),
        pl.BlockSpec((META_PAD, d_att), lambda i: (0, 0)),
        pl.BlockSpec((8, META_PAD), lambda i: (0, 0)),
    ]
    scratch = [
        pltpu.VMEM((tm, d), BF16),
        pltpu.VMEM((tm + N_META, d_pool), F32),
        pltpu.VMEM((N_META, d_pool), F32),
        pltpu.VMEM((1, LANES), F32),
        pltpu.VMEM((8, LANES), F32),
        pltpu.VMEM((1, LANES), F32),
        pltpu.VMEM((8, LANES), F32),
    ]
    kern = functools.partial(_inproj_kernel, tm=tm, tiles_per_batch=tiles_per_batch,
                             n_heads=n_heads, row_chunk=256)
    return pl.pallas_call(
        kern,
        grid=(n // tm,),
        in_specs=in_specs,
        out_specs=out_specs,
        out_shape=out_shape,
        scratch_shapes=scratch,
        compiler_params=_cparams(1),
        name="in_proj",
    )(h1, h1m, gn, wp, wq, wk, wv, wf, wft, bcol, brow, qn, kn, pw, ps)


def _attn_kernel(q_ref, k_ref, v_ref, km_ref, vm_ref, cq_ref, ck_ref, ckm_ref, o_ref, *, tq, scale):
    h = pl.program_id(1)
    seq = q_ref.shape[0]
    lane = lax.broadcasted_iota(jnp.int32, (1, LANES), 1)
    head_lane = lane == h
    ckm = ckm_ref[pl.ds(h, 1), :]
    row_i = lax.broadcasted_iota(jnp.int32, (tq, tq), 0)
    col_i = lax.broadcasted_iota(jnp.int32, (tq, tq), 1)
    causal = col_i <= row_i
    for qi in range(seq // tq):
        rows = slice(qi * tq, (qi + 1) * tq)
        q = q_ref[rows, :]
        cq = jnp.sum(jnp.where(head_lane, cq_ref[rows, :], 0.0), axis=-1, keepdims=True)
        s = _dot_nt(q, km_ref[...]) * scale + (cq - ckm)
        s = jnp.where(lane < N_META, s, NEG_INF)
        m = jnp.max(s, axis=-1, keepdims=True)
        p = jnp.exp(s - m)
        l = jnp.sum(p, axis=-1, keepdims=True)
        acc = _dot(p.astype(BF16), vm_ref[...])
        for j in range(qi + 1):
            cols = slice(j * tq, (j + 1) * tq)
            ck = ck_ref[0, pl.ds(h, 1), cols]
            s = _dot_nt(q, k_ref[cols, :]) * scale + (cq - ck)
            if j == qi:
                s = jnp.where(causal, s, NEG_INF)
            m_new = jnp.maximum(m, jnp.max(s, axis=-1, keepdims=True))
            alpha = jnp.exp(m - m_new)
            p = jnp.exp(s - m_new)
            l = alpha * l + jnp.sum(p, axis=-1, keepdims=True)
            acc = alpha * acc + _dot(p.astype(BF16), v_ref[cols, :])
            m = m_new
        o_ref[rows, :] = (acc / l).astype(BF16)


def _attention(q, k, v, km, vm, cq, ck, ckm, *, batch, tq=512):
    n, d_att = q.shape
    seq = n // batch
    n_heads = d_att // HEAD_DIM
    scale = 1.0 / (HEAD_DIM ** 0.5)
    tok = pl.BlockSpec((seq, HEAD_DIM), lambda b, h: (b, h))
    meta = pl.BlockSpec((META_PAD, HEAD_DIM), lambda b, h: (0, h))
    kern = functools.partial(_attn_kernel, tq=tq, scale=scale)
    return pl.pallas_call(
        kern,
        grid=(batch, n_heads),
        in_specs=[
            tok, tok, tok, meta, meta,
            pl.BlockSpec((seq, LANES), lambda b, h: (b, 0)),
            pl.BlockSpec((1, 8, seq), lambda b, h: (b, 0, 0)),
            pl.BlockSpec((8, META_PAD), lambda b, h: (0, 0)),
        ],
        out_specs=tok,
        out_shape=jax.ShapeDtypeStruct((n, d_att), BF16),
        compiler_params=_cparams(2),
        name="fox_attn",
    )(q, k, v, km, vm, cq, ck, ckm)


def _outproj_kernel(h_ref, pool_ref, att_ref, wop_ref, woa_ref, o_ref, *, col_chunk):
    d = h_ref.shape[1]
    for c in range(0, d, col_chunk):
        cols = slice(c, c + col_chunk)
        o_ref[:, cols] = (h_ref[:, cols] + _dot(pool_ref[...], wop_ref[:, cols])
                          + _dot(att_ref[...], woa_ref[:, cols]))


def _out_proj(h1, pool, att, wop, woa, *, tm=1024):
    n, d = h1.shape
    rows = lambda w: pl.BlockSpec((tm, w), lambda i: (i, 0))
    return pl.pallas_call(
        functools.partial(_outproj_kernel, col_chunk=512),
        grid=(n // tm,),
        in_specs=[rows(d), rows(pool.shape[1]), rows(att.shape[1]),
                  _const_spec(wop.shape), _const_spec(woa.shape)],
        out_specs=rows(d),
        out_shape=jax.ShapeDtypeStruct((n, d), F32),
        compiler_params=_cparams(1),
        name="out_proj",
    )(h1, pool, att, wop, woa)


def kernel(x, meta_tokens, ffn1_norm, ffn1_w_gate, ffn1_w_up, ffn1_w_down, mix_norm, w_in,
           b_forget, q_norm, k_norm, pool_w, pool_scale, w_out, ffn2_norm, ffn2_w_gate,
           ffn2_w_up, ffn2_w_down):
    batch, seq, d = x.shape
    depth = ffn1_norm.shape[0]
    n_heads = b_forget.shape[1]
    d_att = n_heads * HEAD_DIM
    d_pool = pool_scale.shape[1]
    assert meta_tokens.shape[0] == N_META and n_heads <= 8

    hx = x.reshape(batch * seq, d)
    hm = meta_tokens.astype(x.dtype)
    for i in range(depth):
        bf = lambda a: a.astype(BF16)
        hx, hm = _ffn(hx, hm, ffn1_norm[i][None], bf(ffn1_w_gate[i]), bf(ffn1_w_up[i]),
                      bf(ffn1_w_down[i]))

        w = w_in[i]
        o = d_pool
        wf = w[:, o + 3 * d_att:]
        wf_cols = jnp.pad(wf, ((0, 0), (0, LANES - n_heads)))
        wf_rows = jnp.pad(wf.T, ((0, 8 - n_heads), (0, 0)))
        b_cols = jnp.pad(b_forget[i][None], ((0, 0), (0, LANES - n_heads)))
        b_rows = jnp.pad(b_forget[i][:, None], ((0, 8 - n_heads), (0, 0)))
        pool, q, k, v, cq, ck, km, vm, ckm = _in_proj(
            hx, hm, mix_norm[i][None], bf(w[:, :o]), bf(w[:, o:o + d_att]),
            bf(w[:, o + d_att:o + 2 * d_att]), bf(w[:, o + 2 * d_att:o + 3 * d_att]),
            bf(wf_cols), bf(wf_rows), b_cols, b_rows, q_norm[i][None], k_norm[i][None],
            bf(pool_w[i]), pool_scale[i][None], batch=batch)

        att = _attention(q, k, v, km, vm, cq, ck, ckm, batch=batch)
        hx = _out_proj(hx, pool, att, bf(w_out[i][:d_pool]), bf(w_out[i][d_pool:]))
        assert i == depth - 1, "multi-layer meta-row update not implemented"

        hx, _ = _ffn(hx, None, ffn2_norm[i][None], bf(ffn2_w_gate[i]), bf(ffn2_w_up[i]),
                     bf(ffn2_w_down[i]))
    return hx.reshape(batch, seq, d)
```

```python
import functools

import jax
import jax.numpy as jnp
from jax import lax
from jax.experimental import pallas as pl
from jax.experimental.pallas import tpu as pltpu

F32 = jnp.float32
BF16 = jnp.bfloat16

EPS = 1e-6
N_META = 16
POOL_WINDOWS = (2, 4, 8, 16)
HEAD_DIM = 128
LANES = 128
SUBLANES = 8
META_PAD = 128
NEG_INF = float("-inf")

VMEM_LIMIT = 60 * 1024 * 1024


def _cparams(n_axes):
    return pltpu.CompilerParams(
        dimension_semantics=("arbitrary",) * n_axes,
        vmem_limit_bytes=VMEM_LIMIT,
    )


def _rmsnorm_rows(x, gain):
    ms = jnp.mean(x * x, axis=-1, keepdims=True)
    return x * lax.rsqrt(ms + EPS) * gain


def _dot(a, b):
    return jnp.dot(a, b, preferred_element_type=F32)


def _dot_nt(a, b):
    return lax.dot_general(a, b, (((1,), (1,)), ((), ())), preferred_element_type=F32)


def _swiglu_accumulate(xn_ref, wg, wu, wd_cols, o_ref, oe_ref, *, tm, col_chunk):
    d = o_ref.shape[1]
    xn = xn_ref[...]
    gate = _dot(xn, wg)
    up = _dot(xn, wu)
    hmid = (gate / (1.0 + jnp.exp(-gate)) * up).astype(BF16)
    for c in range(0, d, col_chunk):
        part = _dot(hmid, wd_cols(c, c + col_chunk))
        o_ref[:, c:c + col_chunk] += part[:tm]
        if oe_ref is not None:
            oe_ref[:, c:c + col_chunk] += part[tm:]


def _ffn_head_kernel(*refs, tm, n_extra, row_chunk, col_chunk):
    if n_extra:
        (x_ref, e_ref, g_ref, wg_ref, wu_ref, wd_ref,
         o_ref, oe_ref, wgb_ref, wub_ref, wdb_ref, xn_ref) = refs
    else:
        x_ref, g_ref, wg_ref, wu_ref, wd_ref, o_ref, wgb_ref, wub_ref, wdb_ref, xn_ref = refs
        e_ref = oe_ref = None
    f = pl.program_id(0)

    @pl.when(f == 0)
    def _():
        gain = g_ref[...]
        for r in range(0, tm, row_chunk):
            xn_ref[r:r + row_chunk, :] = _rmsnorm_rows(x_ref[r:r + row_chunk, :], gain).astype(BF16)
            o_ref[r:r + row_chunk, :] = x_ref[r:r + row_chunk, :]
        if n_extra:
            xn_ref[tm:tm + n_extra, :] = _rmsnorm_rows(e_ref[...], gain).astype(BF16)
            oe_ref[...] = e_ref[...]

    wg = wg_ref[...].astype(BF16)
    wu = wu_ref[...].astype(BF16)
    wd = (0.5 * wd_ref[...]).astype(BF16)
    wgb_ref[...] = wg
    wub_ref[...] = wu
    wdb_ref[...] = wd
    _swiglu_accumulate(xn_ref, wg, wu, lambda c0, c1: wd[:, c0:c1], o_ref, oe_ref,
                       tm=tm, col_chunk=col_chunk)


def _ffn_tail_kernel(*refs, tm, n_side, row_chunk, col_chunk):
    x_ref, g_ref, wg_ref, wu_ref, wd_ref = refs[:5]
    side_in = refs[5:5 + n_side]
    o_ref = refs[6 + n_side]
    side_out = refs[7 + n_side:7 + 2 * n_side]
    xn_ref = refs[-1]
    f = pl.program_id(1)

    @pl.when(f == 0)
    def _():
        gain = g_ref[...]
        for r in range(0, tm, row_chunk):
            xn_ref[r:r + row_chunk, :] = _rmsnorm_rows(x_ref[r:r + row_chunk, :], gain).astype(BF16)
            o_ref[r:r + row_chunk, :] = x_ref[r:r + row_chunk, :]

    for src, dst in zip(side_in, side_out):
        dst[...] = src[...].astype(BF16)
    _swiglu_accumulate(xn_ref, wg_ref[...], wu_ref[...], lambda c0, c1: wd_ref[:, c0:c1],
                       o_ref, None, tm=tm, col_chunk=col_chunk)


def _ffn(x, extra, gain, wg, wu, wd, side=(), *, tm=1024, tf_head=256, tf=512, side_chunk=128):
    n, d = x.shape
    dff = wg.shape[1]
    n_extra = 0 if extra is None else extra.shape[0]
    n_tiles = n // tm
    assert n % tm == 0 and dff % tf == 0 and dff % tf_head == 0 and n_tiles >= 2

    in_specs = [pl.BlockSpec((tm, d), lambda f: (0, 0), pipeline_mode=pl.Buffered(1))]
    args = [x]
    if n_extra:
        in_specs.append(pl.BlockSpec((n_extra, d), lambda f: (0, 0)))
        args.append(extra)
    in_specs += [
        pl.BlockSpec((1, d), lambda f: (0, 0)),
        pl.BlockSpec((d, tf_head), lambda f: (0, f)),
        pl.BlockSpec((d, tf_head), lambda f: (0, f)),
        pl.BlockSpec((tf_head, d), lambda f: (f, 0)),
    ]
    args += [gain, wg, wu, wd]
    out_shape = [jax.ShapeDtypeStruct((n, d), F32)]
    out_specs = [pl.BlockSpec((tm, d), lambda f: (0, 0))]
    if n_extra:
        out_shape.append(jax.ShapeDtypeStruct((n_extra, d), F32))
        out_specs.append(pl.BlockSpec((n_extra, d), lambda f: (0, 0)))
    out_shape += [jax.ShapeDtypeStruct(wg.shape, BF16), jax.ShapeDtypeStruct(wu.shape, BF16),
                  jax.ShapeDtypeStruct(wd.shape, BF16)]
    out_specs += [pl.BlockSpec((d, tf_head), lambda f: (0, f)),
                  pl.BlockSpec((d, tf_head), lambda f: (0, f)),
                  pl.BlockSpec((tf_head, d), lambda f: (f, 0))]
    head = pl.pallas_call(
        functools.partial(_ffn_head_kernel, tm=tm, n_extra=n_extra, row_chunk=256, col_chunk=512),
        grid=(dff // tf_head,),
        in_specs=in_specs,
        out_specs=out_specs,
        out_shape=out_shape,
        scratch_shapes=[pltpu.VMEM((tm + n_extra, d), BF16)],
        compiler_params=_cparams(1),
        name="ffn_head",
    )(*args)
    out0 = head[0]
    out_extra = head[1] if n_extra else None
    wgb, wub, wdb = head[-3:]

    nf = dff // tf
    steps = (n_tiles - 1) * nf
    in_specs = [
        pl.BlockSpec((tm, d), lambda i, f: (i + 1, 0)),
        pl.BlockSpec((1, d), lambda i, f: (0, 0)),
        pl.BlockSpec((d, tf), lambda i, f: (0, f)),
        pl.BlockSpec((d, tf), lambda i, f: (0, f)),
        pl.BlockSpec((tf, d), lambda i, f: (f, 0)),
    ]
    args = [x, gain, wgb, wub, wdb]
    out_shape = [jax.ShapeDtypeStruct((n, d), F32)]
    out_specs = [pl.BlockSpec((tm, d), lambda i, f: (i + 1, 0))]
    first = 0
    for w in side:
        rows, cols = w.shape
        by_cols = cols > rows
        n_chunks = pl.cdiv(cols if by_cols else rows, side_chunk)
        assert first + n_chunks <= steps

        def chunk_of(i, f, first=first, n_chunks=n_chunks):
            return jnp.clip(i * nf + f - first, 0, n_chunks - 1)

        if by_cols:
            spec = pl.BlockSpec((rows, side_chunk), lambda i, f, c=chunk_of: (0, c(i, f)))
        else:
            spec = pl.BlockSpec((side_chunk, cols), lambda i, f, c=chunk_of: (c(i, f), 0))
        in_specs.append(spec)
        args.append(w)
        out_shape.append(jax.ShapeDtypeStruct(w.shape, BF16))
        out_specs.append(spec)
        first += n_chunks
    in_specs.append(pl.BlockSpec(memory_space=pl.ANY))
    args.append(out0)
    tail = pl.pallas_call(
        functools.partial(_ffn_tail_kernel, tm=tm, n_side=len(side), row_chunk=256, col_chunk=512),
        grid=(n_tiles - 1, nf),
        in_specs=in_specs,
        out_specs=out_specs,
        out_shape=out_shape,
        scratch_shapes=[pltpu.VMEM((tm, d), BF16)],
        input_output_aliases={len(args) - 1: 0},
        compiler_params=_cparams(2),
        name="ffn_tail",
    )(*args)
    return tail[0], out_extra, tuple(tail[1:])


def _log_sigmoid(x):
    return jnp.minimum(x, 0.0) - jnp.log1p(jnp.exp(-jnp.abs(x)))


def _split3(x):
    hi = x.astype(BF16)
    r1 = x - hi.astype(F32)
    mid = r1.astype(BF16)
    lo = (r1 - mid.astype(F32)).astype(BF16)
    return hi, mid, lo


def _cumsum_rows(x, tri):
    hi, mid, lo = _split3(x)
    return _dot(tri, hi) + _dot(tri, mid) + _dot(tri, lo)


def _cumsum_lanes(x, triu):
    hi, mid, lo = _split3(x)
    return _dot(hi, triu) + _dot(mid, triu) + _dot(lo, triu)


def _head_rmsnorm(z, gain, n_heads):
    outs = []
    for h in range(n_heads):
        outs.append(_rmsnorm_rows(z[:, h * HEAD_DIM:(h + 1) * HEAD_DIM], gain))
    return jnp.concatenate(outs, axis=-1)


def _inproj_kernel(h_ref, hm_ref, gn_ref, wp_ref, wq_ref, wk_ref, wv_ref, wfraw_ref,
                   b_ref, qn_ref, kn_ref, pw_ref, ps_ref, side_ref,
                   pool_ref, q_ref, k_ref, v_ref, cq_ref, ck_ref, km_ref, vm_ref, ckm_ref, sideb_ref,
                   u_ref, pe_ref, pmeta_ref, wf_ref, ccol_ref, crow_ref, mcol_ref, mrow_ref,
                   *, tm, tiles_per_batch, n_heads, row_chunk):
    i = pl.program_id(0)
    d_pool = wp_ref.shape[1]
    gain = gn_ref[...]
    sideb_ref[...] = side_ref[...].astype(BF16)

    @pl.when(i == 0)
    def _():
        gate_lane = lax.broadcasted_iota(jnp.int32, (1, LANES), 1) < n_heads
        wf_ref[...] = jnp.where(gate_lane, wfraw_ref[...], jnp.zeros_like(wfraw_ref))
        u_ref[0:META_PAD, :] = jnp.zeros((META_PAD, u_ref.shape[1]), BF16)
        u_ref[0:N_META, :] = _rmsnorm_rows(hm_ref[...], gain).astype(BF16)
        um = u_ref[0:META_PAD, :]
        row_valid = lax.broadcasted_iota(jnp.int32, (META_PAD, 1), 0) < N_META
        pmeta_ref[...] = _dot(um, wp_ref[...])[0:N_META, :]
        km = _head_rmsnorm(_dot(um, wk_ref[...]), kn_ref[...], n_heads)
        km_ref[...] = jnp.where(row_valid, km, 0.0).astype(BF16)
        vm_ref[...] = _dot(um, wv_ref[...]).astype(BF16)
        lf_col = _log_sigmoid(_dot(um, wf_ref[...]) + b_ref[...])
        lf_col = jnp.where(row_valid, lf_col, 0.0)
        mcol_ref[...] = jnp.sum(lf_col, axis=0, keepdims=True)
        lf_row = lf_col.T[0:SUBLANES, :]
        r_i = lax.broadcasted_iota(jnp.int32, (META_PAD, META_PAD), 0)
        c_i = lax.broadcasted_iota(jnp.int32, (META_PAD, META_PAD), 1)
        triu = (r_i <= c_i).astype(BF16)
        ckm_ref[...] = _cumsum_lanes(lf_row, triu)
        mrow_ref[...] = jnp.broadcast_to(jnp.sum(lf_row, axis=1, keepdims=True), mrow_ref.shape)

    @pl.when(i % tiles_per_batch == 0)
    def _():
        ccol_ref[...] = mcol_ref[...]
        crow_ref[...] = mrow_ref[...]
        pe_ref[0:N_META, :] = pmeta_ref[...]

    for r in range(0, tm, row_chunk):
        u_ref[r:r + row_chunk, :] = _rmsnorm_rows(h_ref[r:r + row_chunk, :], gain).astype(BF16)
    u = u_ref[0:tm, :]

    pe_ref[N_META:N_META + tm, :] = _dot(u, wp_ref[...])
    gw = d_pool // len(POOL_WINDOWS)
    for g, w in enumerate(POOL_WINDOWS):
        lanes = slice(g * gw, (g + 1) * gw)
        cur = pe_ref[N_META:N_META + tm, lanes]
        wsum = cur
        for j in range(1, w):
            wsum = wsum + pe_ref[N_META - j:N_META - j + tm, lanes]
        pooled = (wsum / float(w) - cur).astype(BF16)
        mixed = _dot(pooled, pw_ref[g]) * ps_ref[:, lanes]
        pool_ref[:, lanes] = mixed.astype(BF16)
    pe_ref[0:N_META, :] = pe_ref[tm:tm + N_META, :]

    q_ref[...] = _head_rmsnorm(_dot(u, wq_ref[...]), qn_ref[...], n_heads).astype(BF16)
    k_ref[...] = _head_rmsnorm(_dot(u, wk_ref[...]), kn_ref[...], n_heads).astype(BF16)
    v_ref[...] = _dot(u, wv_ref[...]).astype(BF16)

    r_i = lax.broadcasted_iota(jnp.int32, (tm, tm), 0)
    c_i = lax.broadcasted_iota(jnp.int32, (tm, tm), 1)
    lf_col = _log_sigmoid(_dot(u, wf_ref[...]) + b_ref[...])
    cum_col = _cumsum_rows(lf_col, (c_i <= r_i).astype(BF16)) + ccol_ref[...]
    cq_ref[...] = cum_col
    ccol_ref[...] = cum_col[tm - 1:tm, :]
    lf_row = lf_col.T[0:SUBLANES, :]
    cum_row = _cumsum_lanes(lf_row, (r_i <= c_i).astype(BF16)) + crow_ref[:, 0:1]
    ck_ref[0] = cum_row
    crow_ref[...] = jnp.broadcast_to(cum_row[:, tm - 1:tm], crow_ref.shape)


def _const_spec(shape, index=None):
    nd = len(shape)
    index = (0,) * nd if index is None else index
    return pl.BlockSpec(shape, lambda i: index, pipeline_mode=pl.Buffered(1))


def _in_proj(h1, h1m, gn, w_in, b_gate, qn, kn, pw, ps, side, *, batch, d_pool, d_att, tm=512):
    n, d = h1.shape
    seq = n // batch
    n_heads = d_att // HEAD_DIM
    assert seq % tm == 0 and tm >= META_PAD and d_pool == d_att and n_heads <= SUBLANES
    gate_block = (d_pool + 3 * d_att) // LANES
    tiles_per_batch = seq // tm
    side_rows = side.shape[0] // (n // tm)
    assert side_rows * (n // tm) == side.shape[0] and side_rows % SUBLANES == 0
    side_spec = pl.BlockSpec((side_rows, side.shape[1]), lambda i: (i, 0))
    rows = lambda w: pl.BlockSpec((tm, w), lambda i: (i, 0))
    in_specs = [
        rows(d),
        _const_spec((N_META, d)),
        _const_spec((1, d)),
        _const_spec((d, d_pool), (0, 0)),
        _const_spec((d, d_att), (0, 1)),
        _const_spec((d, d_att), (0, 2)),
        _const_spec((d, d_att), (0, 3)),
        _const_spec((d, LANES), (0, gate_block)),
        _const_spec((1, LANES)),
        _const_spec((1, HEAD_DIM)),
        _const_spec((1, HEAD_DIM)),
        _const_spec(pw.shape),
        _const_spec((1, d_pool)),
        side_spec,
    ]
    out_shape = [
        jax.ShapeDtypeStruct((n, d_pool), BF16),
        jax.ShapeDtypeStruct((n, d_att), BF16),
        jax.ShapeDtypeStruct((n, d_att), BF16),
        jax.ShapeDtypeStruct((n, d_att), BF16),
        jax.ShapeDtypeStruct((n, LANES), F32),
        jax.ShapeDtypeStruct((batch, SUBLANES, seq), F32),
        jax.ShapeDtypeStruct((META_PAD, d_att), BF16),
        jax.ShapeDtypeStruct((META_PAD, d_att), BF16),
        jax.ShapeDtypeStruct((SUBLANES, META_PAD), F32),
        jax.ShapeDtypeStruct(side.shape, BF16),
    ]
    out_specs = [
        rows(d_pool), rows(d_att), rows(d_att), rows(d_att), rows(LANES),
        pl.BlockSpec((1, SUBLANES, tm), lambda i: (i // tiles_per_batch, 0, i % tiles_per_batch)),
        pl.BlockSpec((META_PAD, d_att), lambda i: (0, 0)),
        pl.BlockSpec((META_PAD, d_att), lambda i: (0, 0)),
        pl.BlockSpec((SUBLANES, META_PAD), lambda i: (0, 0)),
        side_spec,
    ]
    scratch = [
        pltpu.VMEM((tm, d), BF16),
        pltpu.VMEM((tm + N_META, d_pool), F32),
        pltpu.VMEM((N_META, d_pool), F32),
        pltpu.VMEM((d, LANES), BF16),
        pltpu.VMEM((1, LANES), F32),
        pltpu.VMEM((SUBLANES, LANES), F32),
        pltpu.VMEM((1, LANES), F32),
        pltpu.VMEM((SUBLANES, LANES), F32),
    ]
    kern = functools.partial(_inproj_kernel, tm=tm, tiles_per_batch=tiles_per_batch,
                             n_heads=n_heads, row_chunk=256)
    return pl.pallas_call(
        kern,
        grid=(n // tm,),
        in_specs=in_specs,
        out_specs=out_specs,
        out_shape=out_shape,
        scratch_shapes=scratch,
        compiler_params=_cparams(1),
        name="in_proj",
    )(h1, h1m, gn, w_in, w_in, w_in, w_in, w_in, b_gate, qn, kn, pw, ps, side)


def _attn_kernel(q_ref, k_ref, v_ref, km_ref, vm_ref, cq_ref, ck_ref, ckm_ref, o_ref, *, tq, scale):
    h = pl.program_id(1)
    seq = q_ref.shape[0]
    lane = lax.broadcasted_iota(jnp.int32, (1, LANES), 1)
    head_lane = lane == h
    ckm = ckm_ref[pl.ds(h, 1), :]
    row_i = lax.broadcasted_iota(jnp.int32, (tq, tq), 0)
    col_i = lax.broadcasted_iota(jnp.int32, (tq, tq), 1)
    causal = col_i <= row_i
    for qi in range(seq // tq):
        rows = slice(qi * tq, (qi + 1) * tq)
        q = q_ref[rows, :]
        cq = jnp.sum(jnp.where(head_lane, cq_ref[rows, :], 0.0), axis=-1, keepdims=True)
        s = _dot_nt(q, km_ref[...]) * scale + (cq - ckm)
        s = jnp.where(lane < N_META, s, NEG_INF)
        m = jnp.max(s, axis=-1, keepdims=True)
        p = jnp.exp(s - m)
        l = jnp.sum(p, axis=-1, keepdims=True)
        acc = _dot(p.astype(BF16), vm_ref[...])
        for j in range(qi + 1):
            cols = slice(j * tq, (j + 1) * tq)
            ck = ck_ref[0, pl.ds(h, 1), cols]
            s = _dot_nt(q, k_ref[cols, :]) * scale + (cq - ck)
            if j == qi:
                s = jnp.where(causal, s, NEG_INF)
            m_new = jnp.maximum(m, jnp.max(s, axis=-1, keepdims=True))
            alpha = jnp.exp(m - m_new)
            p = jnp.exp(s - m_new)
            l = alpha * l + jnp.sum(p, axis=-1, keepdims=True)
            acc = alpha * acc + _dot(p.astype(BF16), v_ref[cols, :])
            m = m_new
        o_ref[rows, :] = (acc / l).astype(BF16)


def _attention(q, k, v, km, vm, cq, ck, ckm, *, batch, tq=512):
    n, d_att = q.shape
    seq = n // batch
    n_heads = d_att // HEAD_DIM
    scale = 1.0 / (HEAD_DIM ** 0.5)
    tok = pl.BlockSpec((seq, HEAD_DIM), lambda b, h: (b, h))
    meta = pl.BlockSpec((META_PAD, HEAD_DIM), lambda b, h: (0, h))
    kern = functools.partial(_attn_kernel, tq=tq, scale=scale)
    return pl.pallas_call(
        kern,
        grid=(batch, n_heads),
        in_specs=[
            tok, tok, tok, meta, meta,
            pl.BlockSpec((seq, LANES), lambda b, h: (b, 0)),
            pl.BlockSpec((1, SUBLANES, seq), lambda b, h: (b, 0, 0)),
            pl.BlockSpec((SUBLANES, META_PAD), lambda b, h: (0, 0)),
        ],
        out_specs=tok,
        out_shape=jax.ShapeDtypeStruct((n, d_att), BF16),
        compiler_params=_cparams(2),
        name="fox_attn",
    )(q, k, v, km, vm, cq, ck, ckm)


def _outproj_kernel(h_ref, pool_ref, att_ref, wop_ref, woa_ref, o_ref, *, col_chunk):
    d = h_ref.shape[1]
    for c in range(0, d, col_chunk):
        cols = slice(c, c + col_chunk)
        o_ref[:, cols] = (h_ref[:, cols] + _dot(pool_ref[...], wop_ref[:, cols])
                          + _dot(att_ref[...], woa_ref[:, cols]))


def _out_proj(h1, pool, att, w_out, *, tm=1024):
    n, d = h1.shape
    d_pool, d_att = pool.shape[1], att.shape[1]
    assert d_pool == d_att
    rows = lambda w: pl.BlockSpec((tm, w), lambda i: (i, 0))
    return pl.pallas_call(
        functools.partial(_outproj_kernel, col_chunk=512),
        grid=(n // tm,),
        in_specs=[rows(d), rows(d_pool), rows(d_att),
                  _const_spec((d_pool, d), (0, 0)), _const_spec((d_att, d), (1, 0))],
        out_specs=rows(d),
        out_shape=jax.ShapeDtypeStruct((n, d), F32),
        compiler_params=_cparams(1),
        name="out_proj",
    )(h1, pool, att, w_out, w_out)


def kernel(x, meta_tokens, ffn1_norm, ffn1_w_gate, ffn1_w_up, ffn1_w_down, mix_norm, w_in,
           b_forget, q_norm, k_norm, pool_w, pool_scale, w_out, ffn2_norm, ffn2_w_gate,
           ffn2_w_up, ffn2_w_down):
    batch, seq, d = x.shape
    n_heads = b_forget.shape[1]
    d_att = n_heads * HEAD_DIM
    d_pool = pool_scale.shape[1]
    assert meta_tokens.shape[0] == N_META
    assert ffn1_norm.shape[0] == 1, "one layer: the meta rows past the mixers never reach the output"

    hx = x.reshape(batch * seq, d)
    hm = meta_tokens.astype(x.dtype)
    hx, hm, (w_in_b,) = _ffn(hx, hm, ffn1_norm[0][None], ffn1_w_gate[0], ffn1_w_up[0],
                             ffn1_w_down[0], side=(w_in[0],))

    b_gate = jnp.pad(b_forget[0][None], ((0, 0), (0, LANES - n_heads)))
    pool, q, k, v, cq, ck, km, vm, ckm, w_out_b = _in_proj(
        hx, hm, mix_norm[0][None], w_in_b, b_gate, q_norm[0][None], k_norm[0][None],
        pool_w[0].astype(BF16), pool_scale[0][None], w_out[0],
        batch=batch, d_pool=d_pool, d_att=d_att)

    att = _attention(q, k, v, km, vm, cq, ck, ckm, batch=batch)
    hx = _out_proj(hx, pool, att, w_out_b)

    hx, _, _ = _ffn(hx, None, ffn2_norm[0][None], ffn2_w_gate[0], ffn2_w_up[0], ffn2_w_down[0])
    return hx.reshape(batch, seq, d)
```

```python
import functools

import jax
import jax.numpy as jnp
from jax import lax
from jax.experimental import pallas as pl
from jax.experimental.pallas import tpu as pltpu

F32 = jnp.float32
BF16 = jnp.bfloat16

EPS = 1e-6
N_META = 16
POOL_WINDOWS = (2, 4, 8, 16)
HEAD_DIM = 128
LANES = 128
SUBLANES = 8
META_PAD = 128
NEG_INF = float("-inf")
BIAS_SCALE = HEAD_DIM ** 0.5
EXP2_SCALE = 1.4426950408889634 / BIAS_SCALE

VMEM_LIMIT = 60 * 1024 * 1024


def _cparams(n_axes):
    return pltpu.CompilerParams(
        dimension_semantics=("arbitrary",) * n_axes,
        vmem_limit_bytes=VMEM_LIMIT,
    )


def _rmsnorm_rows(x, gain):
    ms = jnp.mean(x * x, axis=-1, keepdims=True)
    return x * lax.rsqrt(ms + EPS) * gain


def _dot(a, b):
    return jnp.dot(a, b, preferred_element_type=F32)


def _dot_nt(a, b):
    return lax.dot_general(a, b, (((1,), (1,)), ((), ())), preferred_element_type=F32)


def _swiglu_accumulate(xn_ref, wg, wu, wd_cols, o_ref, oe_ref, *, tm, col_chunk):
    d = o_ref.shape[1]
    xn = xn_ref[...]
    gate = _dot(xn, wg)
    up = _dot(xn, wu)
    hmid = (gate / (1.0 + jnp.exp(-gate)) * up).astype(BF16)
    for c in range(0, d, col_chunk):
        part = _dot(hmid, wd_cols(c, c + col_chunk))
        o_ref[:, c:c + col_chunk] += part[:tm]
        if oe_ref is not None:
            oe_ref[:, c:c + col_chunk] += part[tm:]


def _ffn_head_kernel(*refs, tm, n_extra, row_chunk, col_chunk):
    if n_extra:
        (x_ref, e_ref, g_ref, wg_ref, wu_ref, wd_ref,
         o_ref, oe_ref, wgb_ref, wub_ref, wdb_ref, xn_ref) = refs
    else:
        x_ref, g_ref, wg_ref, wu_ref, wd_ref, o_ref, wgb_ref, wub_ref, wdb_ref, xn_ref = refs
        e_ref = oe_ref = None
    f = pl.program_id(0)

    @pl.when(f == 0)
    def _():
        gain = g_ref[...]
        for r in range(0, tm, row_chunk):
            xn_ref[r:r + row_chunk, :] = _rmsnorm_rows(x_ref[r:r + row_chunk, :], gain).astype(BF16)
            o_ref[r:r + row_chunk, :] = x_ref[r:r + row_chunk, :]
        if n_extra:
            xn_ref[tm:tm + n_extra, :] = _rmsnorm_rows(e_ref[...], gain).astype(BF16)
            oe_ref[...] = e_ref[...]

    wg = wg_ref[...].astype(BF16)
    wu = wu_ref[...].astype(BF16)
    wd = (0.5 * wd_ref[...]).astype(BF16)
    wgb_ref[...] = wg
    wub_ref[...] = wu
    wdb_ref[...] = wd
    _swiglu_accumulate(xn_ref, wg, wu, lambda c0, c1: wd[:, c0:c1], o_ref, oe_ref,
                       tm=tm, col_chunk=col_chunk)


def _ffn_tail_kernel(*refs, tm, has_side, row_chunk, col_chunk):
    x_ref, g_ref, wg_ref, wu_ref, wd_ref = refs[:5]
    xn_ref = refs[-1]
    if has_side:
        side_ref, o_ref, sideb_ref = refs[5], refs[6], refs[7]
        sideb_ref[...] = side_ref[...].T.astype(BF16)
    else:
        o_ref = refs[5]
    f = pl.program_id(1)

    @pl.when(f == 0)
    def _():
        gain = g_ref[...]
        for r in range(0, tm, row_chunk):
            xn_ref[r:r + row_chunk, :] = _rmsnorm_rows(x_ref[r:r + row_chunk, :], gain).astype(BF16)
            o_ref[r:r + row_chunk, :] = x_ref[r:r + row_chunk, :]

    _swiglu_accumulate(xn_ref, wg_ref[...], wu_ref[...], lambda c0, c1: wd_ref[:, c0:c1],
                       o_ref, None, tm=tm, col_chunk=col_chunk)


def _ffn(x, extra, gain, wg, wu, wd, side_t=None, *, in_place, tm=1024, tf_head=256, tf=512,
         side_chunk=LANES):
    n, d = x.shape
    dff = wg.shape[1]
    n_extra = 0 if extra is None else extra.shape[0]
    n_tiles = n // tm
    assert n % tm == 0 and dff % tf == 0 and dff % tf_head == 0 and n_tiles >= 2
    assert not (in_place and side_t is not None)

    in_specs = [pl.BlockSpec((tm, d), lambda f: (0, 0), pipeline_mode=pl.Buffered(1))]
    args = [x]
    if n_extra:
        in_specs.append(pl.BlockSpec((n_extra, d), lambda f: (0, 0)))
        args.append(extra)
    in_specs += [
        pl.BlockSpec((1, d), lambda f: (0, 0)),
        pl.BlockSpec((d, tf_head), lambda f: (0, f)),
        pl.BlockSpec((d, tf_head), lambda f: (0, f)),
        pl.BlockSpec((tf_head, d), lambda f: (f, 0)),
    ]
    args += [gain, wg, wu, wd]
    out_shape = [jax.ShapeDtypeStruct((n if in_place else tm, d), F32)]
    out_specs = [pl.BlockSpec((tm, d), lambda f: (0, 0))]
    if n_extra:
        out_shape.append(jax.ShapeDtypeStruct((n_extra, d), F32))
        out_specs.append(pl.BlockSpec((n_extra, d), lambda f: (0, 0)))
    out_shape += [jax.ShapeDtypeStruct(wg.shape, BF16), jax.ShapeDtypeStruct(wu.shape, BF16),
                  jax.ShapeDtypeStruct(wd.shape, BF16)]
    out_specs += [pl.BlockSpec((d, tf_head), lambda f: (0, f)),
                  pl.BlockSpec((d, tf_head), lambda f: (0, f)),
                  pl.BlockSpec((tf_head, d), lambda f: (f, 0))]
    head = pl.pallas_call(
        functools.partial(_ffn_head_kernel, tm=tm, n_extra=n_extra, row_chunk=256, col_chunk=512),
        grid=(dff // tf_head,),
        in_specs=in_specs,
        out_specs=out_specs,
        out_shape=out_shape,
        scratch_shapes=[pltpu.VMEM((tm + n_extra, d), BF16)],
        input_output_aliases={0: 0} if in_place else {},
        compiler_params=_cparams(1),
        name="ffn_head",
    )(*args)
    out_head = head[0]
    out_extra = head[1] if n_extra else None
    wgb, wub, wdb = head[-3:]

    nf = dff // tf
    x_tail = out_head if in_place else x
    out_block = (lambda i, f: (i + 1, 0)) if in_place else (lambda i, f: (i, 0))
    in_specs = [
        pl.BlockSpec((tm, d), lambda i, f: (i + 1, 0)),
        pl.BlockSpec((1, d), lambda i, f: (0, 0)),
        pl.BlockSpec((d, tf), lambda i, f: (0, f)),
        pl.BlockSpec((d, tf), lambda i, f: (0, f)),
        pl.BlockSpec((tf, d), lambda i, f: (f, 0)),
    ]
    args = [x_tail, gain, wgb, wub, wdb]
    out_shape = [jax.ShapeDtypeStruct((n if in_place else n - tm, d), F32)]
    out_specs = [pl.BlockSpec((tm, d), out_block)]
    if side_t is not None:
        cols, rows = side_t.shape
        n_chunks = pl.cdiv(cols, side_chunk)
        assert n_chunks <= (n_tiles - 1) * nf
        chunk_of = lambda i, f: jnp.minimum(i * nf + f, n_chunks - 1)
        in_specs.append(pl.BlockSpec((side_chunk, rows), lambda i, f: (chunk_of(i, f), 0)))
        args.append(side_t)
        out_shape.append(jax.ShapeDtypeStruct((rows, cols), BF16))
        out_specs.append(pl.BlockSpec((rows, side_chunk), lambda i, f: (0, chunk_of(i, f))))
    tail = pl.pallas_call(
        functools.partial(_ffn_tail_kernel, tm=tm, has_side=side_t is not None, row_chunk=256,
                          col_chunk=512),
        grid=(n_tiles - 1, nf),
        in_specs=in_specs,
        out_specs=out_specs,
        out_shape=out_shape,
        scratch_shapes=[pltpu.VMEM((tm, d), BF16)],
        input_output_aliases={0: 0} if in_place else {},
        compiler_params=_cparams(2),
        name="ffn_tail",
    )(*args)
    if in_place:
        return tail[0], out_extra
    return (out_head, tail[0]), out_extra, tail[1]


def _log_sigmoid(x):
    return jnp.minimum(x, 0.0) - jnp.log1p(jnp.exp(-jnp.abs(x)))


def _split3(x):
    hi = x.astype(BF16)
    r1 = x - hi.astype(F32)
    mid = r1.astype(BF16)
    lo = (r1 - mid.astype(F32)).astype(BF16)
    return hi, mid, lo


def _cumsum_rows(x, tri):
    hi, mid, lo = _split3(x)
    return _dot(tri, hi) + _dot(tri, mid) + _dot(tri, lo)


def _bias_columns(cum):
    hi, mid, lo = (p.astype(F32) for p in _split3(cum * BIAS_SCALE))
    roll = lambda x, g: pltpu.roll(x, g * SUBLANES, axis=1)
    cols = hi + roll(mid, 1) + roll(lo, 2) - roll(hi, 3) - roll(mid, 4) - roll(lo, 5)
    return cols.astype(BF16)


def _head_rmsnorm(z, gain, n_heads):
    outs = []
    for h in range(n_heads):
        outs.append(_rmsnorm_rows(z[:, h * HEAD_DIM:(h + 1) * HEAD_DIM], gain))
    return jnp.concatenate(outs, axis=-1)


def _inproj_kernel(ha_ref, hb_ref, hm_ref, gn_ref, wp_ref, wq_ref, wk_ref, wv_ref, wfraw_ref,
                   b_ref, qn_ref, kn_ref, pw_ref, ps_ref, side_ref,
                   pool_ref, q_ref, k_ref, v_ref, aug_ref, km_ref, vm_ref, augm_ref, sideb_ref,
                   u_ref, pe_ref, pmeta_ref, wf_ref, ccol_ref, mcol_ref,
                   *, tm, tiles_a, tiles_per_batch, n_heads, row_chunk):
    i = pl.program_id(0)
    d_pool = wp_ref.shape[1]
    gain = gn_ref[...]
    sideb_ref[...] = side_ref[...].astype(BF16)
    head_lane = lax.broadcasted_iota(jnp.int32, (1, LANES), 1) < n_heads

    @pl.when(i == 0)
    def _():
        wf_ref[...] = jnp.where(head_lane, wfraw_ref[...], jnp.zeros_like(wfraw_ref))
        u_ref[0:META_PAD, :] = jnp.zeros((META_PAD, u_ref.shape[1]), BF16)
        u_ref[0:N_META, :] = _rmsnorm_rows(hm_ref[...], gain).astype(BF16)
        um = u_ref[0:META_PAD, :]
        row_valid = lax.broadcasted_iota(jnp.int32, (META_PAD, 1), 0) < N_META
        pmeta_ref[...] = _dot(um, wp_ref[...])[0:N_META, :]
        km = _head_rmsnorm(_dot(um, wk_ref[...]), kn_ref[...], n_heads)
        km_ref[...] = jnp.where(row_valid, km, 0.0).astype(BF16)
        vm_ref[...] = _dot(um, wv_ref[...]).astype(BF16)
        lf_col = _log_sigmoid(_dot(um, wf_ref[...]) + b_ref[...])
        lf_col = jnp.where(row_valid & head_lane, lf_col, 0.0)
        r_i = lax.broadcasted_iota(jnp.int32, (META_PAD, META_PAD), 0)
        c_i = lax.broadcasted_iota(jnp.int32, (META_PAD, META_PAD), 1)
        cum_m = _cumsum_rows(lf_col, (c_i <= r_i).astype(BF16))
        augm_ref[...] = _bias_columns(cum_m)
        mcol_ref[...] = cum_m[META_PAD - 1:META_PAD, :]

    @pl.when(i % tiles_per_batch == 0)
    def _():
        ccol_ref[...] = mcol_ref[...]
        pe_ref[0:N_META, :] = pmeta_ref[...]

    for r in range(0, tm, row_chunk):
        h = jnp.where(i < tiles_a, ha_ref[r:r + row_chunk, :], hb_ref[r:r + row_chunk, :])
        u_ref[r:r + row_chunk, :] = _rmsnorm_rows(h, gain).astype(BF16)
    u = u_ref[0:tm, :]

    pe_ref[N_META:N_META + tm, :] = _dot(u, wp_ref[...])
    gw = d_pool // len(POOL_WINDOWS)
    for g, w in enumerate(POOL_WINDOWS):
        lanes = slice(g * gw, (g + 1) * gw)
        cur = pe_ref[N_META:N_META + tm, lanes]
        wsum = cur
        for j in range(1, w):
            wsum = wsum + pe_ref[N_META - j:N_META - j + tm, lanes]
        pooled = (wsum / float(w) - cur).astype(BF16)
        mixed = _dot(pooled, pw_ref[g]) * ps_ref[:, lanes]
        pool_ref[:, lanes] = mixed.astype(BF16)
    pe_ref[0:N_META, :] = pe_ref[tm:tm + N_META, :]

    q_ref[...] = _head_rmsnorm(_dot(u, wq_ref[...]), qn_ref[...], n_heads).astype(BF16)
    k_ref[...] = _head_rmsnorm(_dot(u, wk_ref[...]), kn_ref[...], n_heads).astype(BF16)
    v_ref[...] = _dot(u, wv_ref[...]).astype(BF16)

    r_i = lax.broadcasted_iota(jnp.int32, (tm, tm), 0)
    c_i = lax.broadcasted_iota(jnp.int32, (tm, tm), 1)
    lf_col = _log_sigmoid(_dot(u, wf_ref[...]) + b_ref[...])
    lf_col = jnp.where(head_lane, lf_col, 0.0)
    cum_col = _cumsum_rows(lf_col, (c_i <= r_i).astype(BF16)) + ccol_ref[...]
    aug_ref[...] = _bias_columns(cum_col)
    ccol_ref[...] = cum_col[tm - 1:tm, :]


def _const_spec(shape, index=None):
    nd = len(shape)
    index = (0,) * nd if index is None else index
    return pl.BlockSpec(shape, lambda i: index, pipeline_mode=pl.Buffered(1))


def _in_proj(h1, h1m, gn, w_in, b_gate, qn, kn, pw, ps, side, *, batch, d_pool, d_att, tm=512):
    h1a, h1b = h1
    d = h1a.shape[1]
    n = h1a.shape[0] + h1b.shape[0]
    assert h1a.shape[0] % tm == 0 and h1b.shape[0] % tm == 0
    tiles_a = h1a.shape[0] // tm
    seq = n // batch
    n_heads = d_att // HEAD_DIM
    assert seq % tm == 0 and tm >= META_PAD and d_pool == d_att and n_heads <= SUBLANES
    gate_block = (d_pool + 3 * d_att) // LANES
    tiles_per_batch = seq // tm
    side_rows = side.shape[0] // (n // tm)
    assert side_rows * (n // tm) == side.shape[0] and side_rows % SUBLANES == 0
    side_spec = pl.BlockSpec((side_rows, side.shape[1]), lambda i: (i, 0))
    rows = lambda w: pl.BlockSpec((tm, w), lambda i: (i, 0))
    in_specs = [
        pl.BlockSpec((tm, d), lambda i: (jnp.minimum(i, tiles_a - 1), 0)),
        pl.BlockSpec((tm, d), lambda i: (jnp.maximum(i - tiles_a, 0), 0)),
        _const_spec((N_META, d)),
        _const_spec((1, d)),
        _const_spec((d, d_pool), (0, 0)),
        _const_spec((d, d_att), (0, 1)),
        _const_spec((d, d_att), (0, 2)),
        _const_spec((d, d_att), (0, 3)),
        _const_spec((d, LANES), (0, gate_block)),
        _const_spec((1, LANES)),
        _const_spec((1, HEAD_DIM)),
        _const_spec((1, HEAD_DIM)),
        _const_spec(pw.shape),
        _const_spec((1, d_pool)),
        side_spec,
    ]
    out_shape = [
        jax.ShapeDtypeStruct((n, d_pool), BF16),
        jax.ShapeDtypeStruct((n, d_att), BF16),
        jax.ShapeDtypeStruct((n, d_att), BF16),
        jax.ShapeDtypeStruct((n, d_att), BF16),
        jax.ShapeDtypeStruct((n, LANES), BF16),
        jax.ShapeDtypeStruct((META_PAD, d_att), BF16),
        jax.ShapeDtypeStruct((META_PAD, d_att), BF16),
        jax.ShapeDtypeStruct((META_PAD, LANES), BF16),
        jax.ShapeDtypeStruct(side.shape, BF16),
    ]
    out_specs = [
        rows(d_pool), rows(d_att), rows(d_att), rows(d_att), rows(LANES),
        pl.BlockSpec((META_PAD, d_att), lambda i: (0, 0)),
        pl.BlockSpec((META_PAD, d_att), lambda i: (0, 0)),
        pl.BlockSpec((META_PAD, LANES), lambda i: (0, 0)),
        side_spec,
    ]
    scratch = [
        pltpu.VMEM((tm, d), BF16),
        pltpu.VMEM((tm + N_META, d_pool), F32),
        pltpu.VMEM((N_META, d_pool), F32),
        pltpu.VMEM((d, LANES), BF16),
        pltpu.VMEM((1, LANES), F32),
        pltpu.VMEM((1, LANES), F32),
    ]
    kern = functools.partial(_inproj_kernel, tm=tm, tiles_a=tiles_a, tiles_per_batch=tiles_per_batch,
                             n_heads=n_heads, row_chunk=256)
    return pl.pallas_call(
        kern,
        grid=(n // tm,),
        in_specs=in_specs,
        out_specs=out_specs,
        out_shape=out_shape,
        scratch_shapes=scratch,
        compiler_params=_cparams(1),
        name="in_proj",
    )(h1a, h1b, h1m, gn, w_in, w_in, w_in, w_in, w_in, b_gate, qn, kn, pw, ps, side)


def _attn_kernel(q_ref, k_ref, v_ref, km_ref, vm_ref, aug_ref, augm_ref, o_ref, qa_ref, ka_ref,
                 *, tq):
    h = pl.program_id(1)
    seq = q_ref.shape[0]
    lane = lax.broadcasted_iota(jnp.int32, (1, LANES), 1)
    group = lane >> 3
    mine = (lane & (SUBLANES - 1)) == h
    one_hot = lambda lo: jnp.where(mine & (group >= lo) & (group < lo + 3), 1.0, 0.0).astype(BF16)
    keep = lambda lo: jnp.where((group >= lo) & (group < lo + 3), 0.0, 1.0).astype(BF16)
    qa_ref[:, 0:HEAD_DIM] = q_ref[...]
    qa_ref[:, HEAD_DIM:] = aug_ref[...] * keep(3) + one_hot(3)
    ka_ref[:, 0:HEAD_DIM] = k_ref[...]
    ka_ref[:, HEAD_DIM:] = aug_ref[...] * keep(0) + one_hot(0)
    kma = jnp.concatenate([km_ref[...], augm_ref[...] * keep(0) + one_hot(0)], axis=1)

    row_i = lax.broadcasted_iota(jnp.int32, (tq, tq), 0)
    col_i = lax.broadcasted_iota(jnp.int32, (tq, tq), 1)
    causal = col_i <= row_i
    for qi in range(seq // tq):
        rows = slice(qi * tq, (qi + 1) * tq)
        q = qa_ref[rows, :]
        blocks = [slice(j * tq, (j + 1) * tq) for j in range(qi + 1)]
        raw_m = jnp.where(lane < N_META, _dot_nt(q, kma), NEG_INF)
        raws = [_dot_nt(q, ka_ref[cols, :]) for cols in blocks]
        raws[-1] = jnp.where(causal, raws[-1], NEG_INF)
        m = jnp.maximum(jnp.max(functools.reduce(jnp.maximum, raws), axis=-1, keepdims=True),
                        jnp.max(raw_m, axis=-1, keepdims=True))
        p_m = jnp.exp2(EXP2_SCALE * (raw_m - m))
        ps = [jnp.exp2(EXP2_SCALE * (r - m)) for r in raws]
        l = (jnp.sum(functools.reduce(jnp.add, ps), axis=-1, keepdims=True)
             + jnp.sum(p_m, axis=-1, keepdims=True))
        acc = _dot(p_m.astype(BF16), vm_ref[...])
        for p, cols in zip(ps, blocks):
            acc = acc + _dot(p.astype(BF16), v_ref[cols, :])
        o_ref[rows, :] = (acc / l).astype(BF16)


def _attention(q, k, v, km, vm, aug, augm, *, batch, tq=512):
    n, d_att = q.shape
    seq = n // batch
    n_heads = d_att // HEAD_DIM
    tok = pl.BlockSpec((seq, HEAD_DIM), lambda b, h: (b, h))
    meta = pl.BlockSpec((META_PAD, HEAD_DIM), lambda b, h: (0, h))
    return pl.pallas_call(
        functools.partial(_attn_kernel, tq=tq),
        grid=(batch, n_heads),
        in_specs=[
            tok, tok, tok, meta, meta,
            pl.BlockSpec((seq, LANES), lambda b, h: (b, 0)),
            pl.BlockSpec((META_PAD, LANES), lambda b, h: (0, 0)),
        ],
        out_specs=tok,
        out_shape=jax.ShapeDtypeStruct((n, d_att), BF16),
        scratch_shapes=[pltpu.VMEM((seq, 2 * HEAD_DIM), BF16),
                        pltpu.VMEM((seq, 2 * HEAD_DIM), BF16)],
        compiler_params=_cparams(2),
        name="fox_attn",
    )(q, k, v, km, vm, aug, augm)


def _outproj_kernel(ha_ref, hb_ref, pool_ref, att_ref, wop_ref, woa_ref, o_ref, *, tiles_a,
                    col_chunk):
    i = pl.program_id(0)
    d = o_ref.shape[1]
    for c in range(0, d, col_chunk):
        cols = slice(c, c + col_chunk)
        h = jnp.where(i < tiles_a, ha_ref[:, cols], hb_ref[:, cols])
        o_ref[:, cols] = (h + _dot(pool_ref[...], wop_ref[:, cols])
                          + _dot(att_ref[...], woa_ref[:, cols]))


def _out_proj(h1, pool, att, w_out, *, tm=512):
    h1a, h1b = h1
    d = h1a.shape[1]
    n = h1a.shape[0] + h1b.shape[0]
    assert h1a.shape[0] % tm == 0 and h1b.shape[0] % tm == 0
    tiles_a = h1a.shape[0] // tm
    d_pool, d_att = pool.shape[1], att.shape[1]
    assert d_pool == d_att
    rows = lambda w: pl.BlockSpec((tm, w), lambda i: (i, 0))
    return pl.pallas_call(
        functools.partial(_outproj_kernel, tiles_a=tiles_a, col_chunk=512),
        grid=(n // tm,),
        in_specs=[pl.BlockSpec((tm, d), lambda i: (jnp.minimum(i, tiles_a - 1), 0)),
                  pl.BlockSpec((tm, d), lambda i: (jnp.maximum(i - tiles_a, 0), 0)),
                  rows(d_pool), rows(d_att),
                  _const_spec((d_pool, d), (0, 0)), _const_spec((d_att, d), (1, 0))],
        out_specs=rows(d),
        out_shape=jax.ShapeDtypeStruct((n, d), F32),
        compiler_params=_cparams(1),
        name="out_proj",
    )(h1a, h1b, pool, att, w_out, w_out)


def kernel(x, meta_tokens, ffn1_norm, ffn1_w_gate, ffn1_w_up, ffn1_w_down, mix_norm, w_in,
           b_forget, q_norm, k_norm, pool_w, pool_scale, w_out, ffn2_norm, ffn2_w_gate,
           ffn2_w_up, ffn2_w_down):
    batch, seq, d = x.shape
    n_heads = b_forget.shape[1]
    d_att = n_heads * HEAD_DIM
    d_pool = pool_scale.shape[1]
    assert meta_tokens.shape[0] == N_META
    assert ffn1_norm.shape[0] == 1, "one layer: the meta rows past the mixers never reach the output"

    hx = x.reshape(batch * seq, d)
    hm = meta_tokens.astype(x.dtype)
    h1, hm, w_in_b = _ffn(hx, hm, ffn1_norm[0][None], ffn1_w_gate[0], ffn1_w_up[0],
                          ffn1_w_down[0], side_t=w_in[0].T, in_place=False)

    b_gate = jnp.pad(b_forget[0][None], ((0, 0), (0, LANES - n_heads)))
    pool, q, k, v, aug, km, vm, augm, w_out_b = _in_proj(
        h1, hm, mix_norm[0][None], w_in_b, b_gate, q_norm[0][None], k_norm[0][None],
        pool_w[0].astype(BF16), pool_scale[0][None], w_out[0],
        batch=batch, d_pool=d_pool, d_att=d_att)

    att = _attention(q, k, v, km, vm, aug, augm, batch=batch)
    h2 = _out_proj(h1, pool, att, w_out_b)

    hx, _ = _ffn(h2, None, ffn2_norm[0][None], ffn2_w_gate[0], ffn2_w_up[0], ffn2_w_down[0],
                 in_place=True)
    return hx.reshape(batch, seq, d)
```

```python
import functools

import jax
import jax.numpy as jnp
from jax import lax
from jax.experimental import pallas as pl
from jax.experimental.pallas import tpu as pltpu

F32 = jnp.float32
BF16 = jnp.bfloat16

EPS = 1e-6
N_META = 16
POOL_WINDOWS = (2, 4, 8, 16)
HEAD_DIM = 128
LANES = 128
SUBLANES = 8
META_PAD = 128
NEG_INF = float("-inf")
BIAS_SCALE = HEAD_DIM ** 0.5
EXP2_SCALE = 1.4426950408889634 / BIAS_SCALE

VMEM_LIMIT = 60 * 1024 * 1024


def _cparams(n_axes):
    return pltpu.CompilerParams(
        dimension_semantics=("arbitrary",) * n_axes,
        vmem_limit_bytes=VMEM_LIMIT,
    )


def _rmsnorm_rows(x, gain):
    ms = jnp.mean(x * x, axis=-1, keepdims=True)
    return x * lax.rsqrt(ms + EPS) * gain


def _dot(a, b):
    return jnp.dot(a, b, preferred_element_type=F32)


def _dot_nt(a, b):
    return lax.dot_general(a, b, (((1,), (1,)), ((), ())), preferred_element_type=F32)


def _swiglu_accumulate(xn_ref, wg, wu, wd_cols, o_ref, oe_ref, *, tm, col_chunk):
    d = o_ref.shape[1]
    xn = xn_ref[...]
    gate = _dot(xn, wg)
    up = _dot(xn, wu)
    hmid = (gate / (1.0 + jnp.exp(-gate)) * up).astype(BF16)
    for c in range(0, d, col_chunk):
        part = _dot(hmid, wd_cols(c, c + col_chunk))
        o_ref[:, c:c + col_chunk] += part[:tm]
        if oe_ref is not None:
            oe_ref[:, c:c + col_chunk] += part[tm:]


def _ffn_head_kernel(*refs, tm, n_extra, row_chunk, col_chunk):
    if n_extra:
        (x_ref, e_ref, g_ref, wg_ref, wu_ref, wd_ref,
         o_ref, oe_ref, wgb_ref, wub_ref, wdb_ref, xn_ref) = refs
    else:
        x_ref, g_ref, wg_ref, wu_ref, wd_ref, o_ref, wgb_ref, wub_ref, wdb_ref, xn_ref = refs
        e_ref = oe_ref = None
    f = pl.program_id(0)

    @pl.when(f == 0)
    def _():
        gain = g_ref[...]
        for r in range(0, tm, row_chunk):
            xn_ref[r:r + row_chunk, :] = _rmsnorm_rows(x_ref[r:r + row_chunk, :], gain).astype(BF16)
            o_ref[r:r + row_chunk, :] = x_ref[r:r + row_chunk, :]
        if n_extra:
            xn_ref[tm:tm + n_extra, :] = _rmsnorm_rows(e_ref[...], gain).astype(BF16)
            oe_ref[...] = e_ref[...]

    wg = wg_ref[...].astype(BF16)
    wu = wu_ref[...].astype(BF16)
    wd = (0.5 * wd_ref[...]).astype(BF16)
    wgb_ref[...] = wg
    wub_ref[...] = wu
    wdb_ref[...] = wd
    _swiglu_accumulate(xn_ref, wg, wu, lambda c0, c1: wd[:, c0:c1], o_ref, oe_ref,
                       tm=tm, col_chunk=col_chunk)


def _ffn_tail_kernel(*refs, tm, side_rows, side_chunks, row_chunk, col_chunk):
    x_ref, g_ref, wg_ref, wu_ref, wd_ref = refs[:5]
    xn_ref = refs[-1]
    f = pl.program_id(1)
    if side_rows:
        side_ref, o_ref, sideb_ref = refs[5], refs[6], refs[7]
        chunk = jnp.minimum(pl.program_id(0) * pl.num_programs(1) + f, side_chunks - 1)
        row = chunk * side_ref.shape[0] + lax.broadcasted_iota(jnp.int32, (side_ref.shape[0], 1), 0)
        sideb_ref[...] = jnp.where(row < side_rows, side_ref[...], 0.0).T.astype(BF16)
    else:
        o_ref = refs[5]

    @pl.when(f == 0)
    def _():
        gain = g_ref[...]
        for r in range(0, tm, row_chunk):
            xn_ref[r:r + row_chunk, :] = _rmsnorm_rows(x_ref[r:r + row_chunk, :], gain).astype(BF16)
            o_ref[r:r + row_chunk, :] = x_ref[r:r + row_chunk, :]

    _swiglu_accumulate(xn_ref, wg_ref[...], wu_ref[...], lambda c0, c1: wd_ref[:, c0:c1],
                       o_ref, None, tm=tm, col_chunk=col_chunk)


def _ffn(x, extra, gain, wg, wu, wd, side_t=None, *, in_place, tm=1024, tf_head=256, tf=512,
         side_chunk=LANES):
    n, d = x.shape
    dff = wg.shape[1]
    n_extra = 0 if extra is None else extra.shape[0]
    n_tiles = n // tm
    assert n % tm == 0 and dff % tf == 0 and dff % tf_head == 0 and n_tiles >= 2
    assert not (in_place and side_t is not None)

    in_specs = [pl.BlockSpec((tm, d), lambda f: (0, 0), pipeline_mode=pl.Buffered(1))]
    args = [x]
    if n_extra:
        in_specs.append(pl.BlockSpec((n_extra, d), lambda f: (0, 0)))
        args.append(extra)
    in_specs += [
        pl.BlockSpec((1, d), lambda f: (0, 0)),
        pl.BlockSpec((d, tf_head), lambda f: (0, f)),
        pl.BlockSpec((d, tf_head), lambda f: (0, f)),
        pl.BlockSpec((tf_head, d), lambda f: (f, 0)),
    ]
    args += [gain, wg, wu, wd]
    out_shape = [jax.ShapeDtypeStruct((n if in_place else tm, d), F32)]
    out_specs = [pl.BlockSpec((tm, d), lambda f: (0, 0))]
    if n_extra:
        out_shape.append(jax.ShapeDtypeStruct((n_extra, d), F32))
        out_specs.append(pl.BlockSpec((n_extra, d), lambda f: (0, 0)))
    out_shape += [jax.ShapeDtypeStruct(wg.shape, BF16), jax.ShapeDtypeStruct(wu.shape, BF16),
                  jax.ShapeDtypeStruct(wd.shape, BF16)]
    out_specs += [pl.BlockSpec((d, tf_head), lambda f: (0, f)),
                  pl.BlockSpec((d, tf_head), lambda f: (0, f)),
                  pl.BlockSpec((tf_head, d), lambda f: (f, 0))]
    head = pl.pallas_call(
        functools.partial(_ffn_head_kernel, tm=tm, n_extra=n_extra, row_chunk=256, col_chunk=512),
        grid=(dff // tf_head,),
        in_specs=in_specs,
        out_specs=out_specs,
        out_shape=out_shape,
        scratch_shapes=[pltpu.VMEM((tm + n_extra, d), BF16)],
        input_output_aliases={0: 0} if in_place else {},
        compiler_params=_cparams(1),
        name="ffn_head",
    )(*args)
    out_head = head[0]
    out_extra = head[1] if n_extra else None
    wgb, wub, wdb = head[-3:]

    nf = dff // tf
    x_tail = out_head if in_place else x
    out_block = (lambda i, f: (i + 1, 0)) if in_place else (lambda i, f: (i, 0))
    in_specs = [
        pl.BlockSpec((tm, d), lambda i, f: (i + 1, 0)),
        pl.BlockSpec((1, d), lambda i, f: (0, 0)),
        pl.BlockSpec((d, tf), lambda i, f: (0, f)),
        pl.BlockSpec((d, tf), lambda i, f: (0, f)),
        pl.BlockSpec((tf, d), lambda i, f: (f, 0)),
    ]
    args = [x_tail, gain, wgb, wub, wdb]
    out_shape = [jax.ShapeDtypeStruct((n if in_place else n - tm, d), F32)]
    out_specs = [pl.BlockSpec((tm, d), out_block)]
    n_chunks = 0
    if side_t is not None:
        cols, rows = side_t.shape
        n_chunks = pl.cdiv(cols, side_chunk)
        assert n_chunks <= (n_tiles - 1) * nf
        chunk_of = lambda i, f: jnp.minimum(i * nf + f, n_chunks - 1)
        in_specs.append(pl.BlockSpec((side_chunk, rows), lambda i, f: (chunk_of(i, f), 0)))
        args.append(side_t)
        out_shape.append(jax.ShapeDtypeStruct((rows, n_chunks * side_chunk), BF16))
        out_specs.append(pl.BlockSpec((rows, side_chunk), lambda i, f: (0, chunk_of(i, f))))
    tail = pl.pallas_call(
        functools.partial(_ffn_tail_kernel, tm=tm,
                          side_rows=0 if side_t is None else side_t.shape[0],
                          side_chunks=n_chunks, row_chunk=256, col_chunk=512),
        grid=(n_tiles - 1, nf),
        in_specs=in_specs,
        out_specs=out_specs,
        out_shape=out_shape,
        scratch_shapes=[pltpu.VMEM((tm, d), BF16)],
        input_output_aliases={0: 0} if in_place else {},
        compiler_params=_cparams(2),
        name="ffn_tail",
    )(*args)
    if in_place:
        return tail[0], out_extra
    return (out_head, tail[0]), out_extra, tail[1]


def _log_sigmoid(x):
    return jnp.minimum(x, 0.0) - jnp.log1p(jnp.exp(-jnp.abs(x)))


def _split3(x):
    hi = x.astype(BF16)
    r1 = x - hi.astype(F32)
    mid = r1.astype(BF16)
    lo = (r1 - mid.astype(F32)).astype(BF16)
    return hi, mid, lo


def _cumsum_rows(x, tri, head_lane):
    hi, mid, lo = (p.astype(F32) for p in _split3(x))
    packed = hi + pltpu.roll(mid, SUBLANES, axis=1) + pltpu.roll(lo, 2 * SUBLANES, axis=1)
    c = _dot(tri, packed.astype(BF16))
    c = c + pltpu.roll(c, LANES - SUBLANES, axis=1) + pltpu.roll(c, LANES - 2 * SUBLANES, axis=1)
    return jnp.where(head_lane, c, 0.0)


def _bias_columns(cum):
    hi, mid, lo = (p.astype(F32) for p in _split3(cum * BIAS_SCALE))
    roll = lambda x, g: pltpu.roll(x, g * SUBLANES, axis=1)
    cols = hi + roll(mid, 1) + roll(lo, 2) - roll(hi, 3) - roll(mid, 4) - roll(lo, 5)
    return cols.astype(BF16)


def _head_rmsnorm(z, gain, n_heads):
    outs = []
    for h in range(n_heads):
        outs.append(_rmsnorm_rows(z[:, h * HEAD_DIM:(h + 1) * HEAD_DIM], gain))
    return jnp.concatenate(outs, axis=-1)


def _window_means(e, w):
    assert w & (w - 1) == 0 and w <= N_META
    s, span = e, 1
    while span < w:
        s = s + pltpu.roll(s, span, axis=0)
        span *= 2
    return s[N_META:, :] / float(w)


def _inproj_kernel(ha_ref, hb_ref, hm_ref, gn_ref, w_ref, b_ref, qn_ref, kn_ref, pw_ref, ps_ref,
                   side_ref,
                   pool_ref, q_ref, k_ref, v_ref, aug_ref, km_ref, vm_ref, augm_ref, sideb_ref,
                   u_ref, pe_ref, pmeta_ref, ccol_ref, mcol_ref,
                   *, tm, tiles_a, tiles_per_batch, n_heads, d_pool, d_att, row_chunk):
    i = pl.program_id(0)
    gain = gn_ref[...]
    sideb_ref[...] = side_ref[...].astype(BF16)
    head_lane = lax.broadcasted_iota(jnp.int32, (1, LANES), 1) < n_heads
    w_pool = lambda: w_ref[:, 0:d_pool]
    w_q = lambda: w_ref[:, d_pool:d_pool + d_att]
    w_k = lambda: w_ref[:, d_pool + d_att:d_pool + 2 * d_att]
    w_vf = lambda: w_ref[:, d_pool + 2 * d_att:]

    def log_forget(zvf):
        lf = _log_sigmoid(zvf[:, d_att:] + b_ref[...])
        return jnp.where(head_lane, lf, 0.0)

    def tri(t):
        r_i = lax.broadcasted_iota(jnp.int32, (t, t), 0)
        c_i = lax.broadcasted_iota(jnp.int32, (t, t), 1)
        return (c_i <= r_i).astype(BF16)

    @pl.when(i == 0)
    def _():
        u_ref[0:META_PAD, :] = jnp.zeros((META_PAD, u_ref.shape[1]), BF16)
        u_ref[0:N_META, :] = _rmsnorm_rows(hm_ref[...], gain).astype(BF16)
        um = u_ref[0:META_PAD, :]
        row_valid = lax.broadcasted_iota(jnp.int32, (META_PAD, 1), 0) < N_META
        pmeta_ref[...] = _dot(um, w_pool())[0:N_META, :]
        km = _head_rmsnorm(_dot(um, w_k()), kn_ref[...], n_heads)
        km_ref[...] = jnp.where(row_valid, km, 0.0).astype(BF16)
        zvf = _dot(um, w_vf())
        vm_ref[...] = zvf[:, 0:d_att].astype(BF16)
        cum_m = _cumsum_rows(jnp.where(row_valid, log_forget(zvf), 0.0), tri(META_PAD), head_lane)
        augm_ref[...] = _bias_columns(cum_m)
        mcol_ref[...] = cum_m[META_PAD - 1:META_PAD, :]

    @pl.when(i % tiles_per_batch == 0)
    def _():
        ccol_ref[...] = mcol_ref[...]
        pe_ref[0:N_META, :] = pmeta_ref[...]

    for r in range(0, tm, row_chunk):
        h = jnp.where(i < tiles_a, ha_ref[r:r + row_chunk, :], hb_ref[r:r + row_chunk, :])
        u_ref[r:r + row_chunk, :] = _rmsnorm_rows(h, gain).astype(BF16)
    u = u_ref[0:tm, :]

    pe_ref[N_META:N_META + tm, :] = _dot(u, w_pool())
    q_ref[...] = _head_rmsnorm(_dot(u, w_q()), qn_ref[...], n_heads).astype(BF16)
    k_ref[...] = _head_rmsnorm(_dot(u, w_k()), kn_ref[...], n_heads).astype(BF16)
    zvf = _dot(u, w_vf())
    v_ref[...] = zvf[:, 0:d_att].astype(BF16)

    gw = d_pool // len(POOL_WINDOWS)
    for g, w in enumerate(POOL_WINDOWS):
        lanes = slice(g * gw, (g + 1) * gw)
        e = pe_ref[:, lanes]
        pooled = (_window_means(e, w) - e[N_META:, :]).astype(BF16)
        mixed = _dot(pooled, pw_ref[g]) * ps_ref[:, lanes]
        pool_ref[:, lanes] = mixed.astype(BF16)
    pe_ref[0:N_META, :] = pe_ref[tm:tm + N_META, :]

    cum_col = _cumsum_rows(log_forget(zvf), tri(tm), head_lane) + ccol_ref[...]
    aug_ref[...] = _bias_columns(cum_col)
    ccol_ref[...] = cum_col[tm - 1:tm, :]


def _const_spec(shape, index=None):
    nd = len(shape)
    index = (0,) * nd if index is None else index
    return pl.BlockSpec(shape, lambda i: index, pipeline_mode=pl.Buffered(1))


def _in_proj(h1, h1m, gn, w_in, b_gate, qn, kn, pw, ps, side, *, batch, d_pool, d_att, tm=512):
    h1a, h1b = h1
    d = h1a.shape[1]
    n = h1a.shape[0] + h1b.shape[0]
    assert h1a.shape[0] % tm == 0 and h1b.shape[0] % tm == 0
    tiles_a = h1a.shape[0] // tm
    seq = n // batch
    n_heads = d_att // HEAD_DIM
    assert seq % tm == 0 and tm >= META_PAD and d_pool == d_att and n_heads <= SUBLANES
    assert w_in.shape == (d, d_pool + 3 * d_att + LANES)
    tiles_per_batch = seq // tm
    side_rows = side.shape[0] // (n // tm)
    assert side_rows * (n // tm) == side.shape[0] and side_rows % SUBLANES == 0
    side_spec = pl.BlockSpec((side_rows, side.shape[1]), lambda i: (i, 0))
    rows = lambda w: pl.BlockSpec((tm, w), lambda i: (i, 0))
    in_specs = [
        pl.BlockSpec((tm, d), lambda i: (jnp.minimum(i, tiles_a - 1), 0)),
        pl.BlockSpec((tm, d), lambda i: (jnp.maximum(i - tiles_a, 0), 0)),
        _const_spec((N_META, d)),
        _const_spec((1, d)),
        _const_spec(w_in.shape),
        _const_spec((1, LANES)),
        _const_spec((1, HEAD_DIM)),
        _const_spec((1, HEAD_DIM)),
        _const_spec(pw.shape),
        _const_spec((1, d_pool)),
        side_spec,
    ]
    out_shape = [
        jax.ShapeDtypeStruct((n, d_pool), BF16),
        jax.ShapeDtypeStruct((n, d_att), BF16),
        jax.ShapeDtypeStruct((n, d_att), BF16),
        jax.ShapeDtypeStruct((n, d_att), BF16),
        jax.ShapeDtypeStruct((n, LANES), BF16),
        jax.ShapeDtypeStruct((META_PAD, d_att), BF16),
        jax.ShapeDtypeStruct((META_PAD, d_att), BF16),
        jax.ShapeDtypeStruct((META_PAD, LANES), BF16),
        jax.ShapeDtypeStruct(side.shape, BF16),
    ]
    out_specs = [
        rows(d_pool), rows(d_att), rows(d_att), rows(d_att), rows(LANES),
        pl.BlockSpec((META_PAD, d_att), lambda i: (0, 0)),
        pl.BlockSpec((META_PAD, d_att), lambda i: (0, 0)),
        pl.BlockSpec((META_PAD, LANES), lambda i: (0, 0)),
        side_spec,
    ]
    scratch = [
        pltpu.VMEM((tm, d), BF16),
        pltpu.VMEM((tm + N_META, d_pool), F32),
        pltpu.VMEM((N_META, d_pool), F32),
        pltpu.VMEM((1, LANES), F32),
        pltpu.VMEM((1, LANES), F32),
    ]
    kern = functools.partial(_inproj_kernel, tm=tm, tiles_a=tiles_a, tiles_per_batch=tiles_per_batch,
                             n_heads=n_heads, d_pool=d_pool, d_att=d_att, row_chunk=256)
    return pl.pallas_call(
        kern,
        grid=(n // tm,),
        in_specs=in_specs,
        out_specs=out_specs,
        out_shape=out_shape,
        scratch_shapes=scratch,
        compiler_params=_cparams(1),
        name="in_proj",
    )(h1a, h1b, h1m, gn, w_in, b_gate, qn, kn, pw, ps, side)


def _attn_kernel(q_ref, k_ref, v_ref, km_ref, vm_ref, aug_ref, augm_ref, o_ref, qa_ref, ka_ref,
                 *, tq):
    h = pl.program_id(1)
    seq = q_ref.shape[0]
    lane = lax.broadcasted_iota(jnp.int32, (1, LANES), 1)
    group = lane >> 3
    mine = (lane & (SUBLANES - 1)) == h
    one_hot = lambda lo: jnp.where(mine & (group >= lo) & (group < lo + 3), 1.0, 0.0).astype(BF16)
    keep = lambda lo: jnp.where((group >= lo) & (group < lo + 3), 0.0, 1.0).astype(BF16)
    qa_ref[:, 0:HEAD_DIM] = q_ref[...]
    qa_ref[:, HEAD_DIM:] = aug_ref[...] * keep(3) + one_hot(3)
    ka_ref[:, 0:HEAD_DIM] = k_ref[...]
    ka_ref[:, HEAD_DIM:] = aug_ref[...] * keep(0) + one_hot(0)
    kma = jnp.concatenate([km_ref[...], augm_ref[...] * keep(0) + one_hot(0)], axis=1)

    row_i = lax.broadcasted_iota(jnp.int32, (tq, tq), 0)
    col_i = lax.broadcasted_iota(jnp.int32, (tq, tq), 1)
    causal = col_i <= row_i
    for qi in range(seq // tq):
        rows = slice(qi * tq, (qi + 1) * tq)
        q = qa_ref[rows, :]
        blocks = [slice(j * tq, (j + 1) * tq) for j in range(qi + 1)]
        raw_m = jnp.where(lane < N_META, _dot_nt(q, kma), NEG_INF)
        raws = [_dot_nt(q, ka_ref[cols, :]) for cols in blocks]
        raws[-1] = jnp.where(causal, raws[-1], NEG_INF)
        m = jnp.maximum(jnp.max(functools.reduce(jnp.maximum, raws), axis=-1, keepdims=True),
                        jnp.max(raw_m, axis=-1, keepdims=True))
        p_m = jnp.exp2(EXP2_SCALE * (raw_m - m))
        ps = [jnp.exp2(EXP2_SCALE * (r - m)) for r in raws]
        l = (jnp.sum(functools.reduce(jnp.add, ps), axis=-1, keepdims=True)
             + jnp.sum(p_m, axis=-1, keepdims=True))
        acc = _dot(p_m.astype(BF16), vm_ref[...])
        for p, cols in zip(ps, blocks):
            acc = acc + _dot(p.astype(BF16), v_ref[cols, :])
        o_ref[rows, :] = (acc / l).astype(BF16)


def _attention(q, k, v, km, vm, aug, augm, *, batch, tq=512):
    n, d_att = q.shape
    seq = n // batch
    n_heads = d_att // HEAD_DIM
    tok = pl.BlockSpec((seq, HEAD_DIM), lambda b, h: (b, h))
    meta = pl.BlockSpec((META_PAD, HEAD_DIM), lambda b, h: (0, h))
    return pl.pallas_call(
        functools.partial(_attn_kernel, tq=tq),
        grid=(batch, n_heads),
        in_specs=[
            tok, tok, tok, meta, meta,
            pl.BlockSpec((seq, LANES), lambda b, h: (b, 0)),
            pl.BlockSpec((META_PAD, LANES), lambda b, h: (0, 0)),
        ],
        out_specs=tok,
        out_shape=jax.ShapeDtypeStruct((n, d_att), BF16),
        scratch_shapes=[pltpu.VMEM((seq, 2 * HEAD_DIM), BF16),
                        pltpu.VMEM((seq, 2 * HEAD_DIM), BF16)],
        compiler_params=_cparams(2),
        name="fox_attn",
    )(q, k, v, km, vm, aug, augm)


def _outproj_kernel(ha_ref, hb_ref, pool_ref, att_ref, wop_ref, woa_ref, o_ref, *, tiles_a,
                    col_chunk):
    i = pl.program_id(0)
    d = o_ref.shape[1]
    for c in range(0, d, col_chunk):
        cols = slice(c, c + col_chunk)
        h = jnp.where(i < tiles_a, ha_ref[:, cols], hb_ref[:, cols])
        o_ref[:, cols] = (h + _dot(pool_ref[...], wop_ref[:, cols])
                          + _dot(att_ref[...], woa_ref[:, cols]))


def _out_proj(h1, pool, att, w_out, *, tm=512):
    h1a, h1b = h1
    d = h1a.shape[1]
    n = h1a.shape[0] + h1b.shape[0]
    assert h1a.shape[0] % tm == 0 and h1b.shape[0] % tm == 0
    tiles_a = h1a.shape[0] // tm
    d_pool, d_att = pool.shape[1], att.shape[1]
    assert d_pool == d_att
    rows = lambda w: pl.BlockSpec((tm, w), lambda i: (i, 0))
    return pl.pallas_call(
        functools.partial(_outproj_kernel, tiles_a=tiles_a, col_chunk=512),
        grid=(n // tm,),
        in_specs=[pl.BlockSpec((tm, d), lambda i: (jnp.minimum(i, tiles_a - 1), 0)),
                  pl.BlockSpec((tm, d), lambda i: (jnp.maximum(i - tiles_a, 0), 0)),
                  rows(d_pool), rows(d_att),
                  _const_spec((d_pool, d), (0, 0)), _const_spec((d_att, d), (1, 0))],
        out_specs=rows(d),
        out_shape=jax.ShapeDtypeStruct((n, d), F32),
        compiler_params=_cparams(1),
        name="out_proj",
    )(h1a, h1b, pool, att, w_out, w_out)


def kernel(x, meta_tokens, ffn1_norm, ffn1_w_gate, ffn1_w_up, ffn1_w_down, mix_norm, w_in,
           b_forget, q_norm, k_norm, pool_w, pool_scale, w_out, ffn2_norm, ffn2_w_gate,
           ffn2_w_up, ffn2_w_down):
    batch, seq, d = x.shape
    n_heads = b_forget.shape[1]
    d_att = n_heads * HEAD_DIM
    d_pool = pool_scale.shape[1]
    assert meta_tokens.shape[0] == N_META
    assert ffn1_norm.shape[0] == 1, "one layer: the meta rows past the mixers never reach the output"

    hx = x.reshape(batch * seq, d)
    hm = meta_tokens.astype(x.dtype)
    h1, hm, w_in_b = _ffn(hx, hm, ffn1_norm[0][None], ffn1_w_gate[0], ffn1_w_up[0],
                          ffn1_w_down[0], side_t=w_in[0].T, in_place=False)

    b_gate = jnp.pad(b_forget[0][None], ((0, 0), (0, LANES - n_heads)))
    pool, q, k, v, aug, km, vm, augm, w_out_b = _in_proj(
        h1, hm, mix_norm[0][None], w_in_b, b_gate, q_norm[0][None], k_norm[0][None],
        pool_w[0].astype(BF16), pool_scale[0][None], w_out[0],
        batch=batch, d_pool=d_pool, d_att=d_att)

    att = _attention(q, k, v, km, vm, aug, augm, batch=batch)
    h2 = _out_proj(h1, pool, att, w_out_b)

    hx, _ = _ffn(h2, None, ffn2_norm[0][None], ffn2_w_gate[0], ffn2_w_up[0], ffn2_w_down[0],
                 in_place=True)
    return hx.reshape(batch, seq, d)
```

```python
import functools

import jax
import jax.numpy as jnp
from jax import lax
from jax.experimental import pallas as pl
from jax.experimental.pallas import tpu as pltpu

F32 = jnp.float32
BF16 = jnp.bfloat16

EPS = 1e-6
N_META = 16
POOL_WINDOWS = (2, 4, 8, 16)
HEAD_DIM = 128
LANES = 128
SUBLANES = 8
META_PAD = 128
NEG_INF = float("-inf")
BIAS_SCALE = HEAD_DIM ** 0.5
EXP2_SCALE = 1.4426950408889634 / BIAS_SCALE

VMEM_LIMIT = 60 * 1024 * 1024


def _cparams(n_axes):
    return pltpu.CompilerParams(
        dimension_semantics=("arbitrary",) * n_axes,
        vmem_limit_bytes=VMEM_LIMIT,
    )


def _rmsnorm_rows(x, gain):
    ms = jnp.mean(x * x, axis=-1, keepdims=True)
    return x * lax.rsqrt(ms + EPS) * gain


def _dot(a, b):
    return jnp.dot(a, b, preferred_element_type=F32)


def _dot_nt(a, b):
    return lax.dot_general(a, b, (((1,), (1,)), ((), ())), preferred_element_type=F32)


def _swiglu_accumulate(xn_ref, wg, wu, wd_cols, o_ref, oe_ref, *, tm, col_chunk, init_ref=None):
    d = o_ref.shape[1]
    xn = xn_ref[...]
    gate = _dot(xn, wg)
    up = _dot(xn, wu)
    hmid = (gate / (1.0 + jnp.exp(-gate)) * up).astype(BF16)
    for c in range(0, d, col_chunk):
        cols = slice(c, c + col_chunk)
        part = _dot(hmid, wd_cols(c, c + col_chunk))
        base = o_ref if init_ref is None else init_ref
        o_ref[:, cols] = base[:, cols] + part[:tm]
        if oe_ref is not None:
            oe_ref[:, cols] += part[tm:]


def _ffn_head_kernel(*refs, tm, n_extra, row_chunk, col_chunk):
    if n_extra:
        (x_ref, e_ref, g_ref, wg_ref, wu_ref, wd_ref,
         o_ref, oe_ref, wgb_ref, wub_ref, wdb_ref, xn_ref) = refs
    else:
        x_ref, g_ref, wg_ref, wu_ref, wd_ref, o_ref, wgb_ref, wub_ref, wdb_ref, xn_ref = refs
        e_ref = oe_ref = None
    f = pl.program_id(0)

    @pl.when(f == 0)
    def _():
        gain = g_ref[...]
        for r in range(0, tm, row_chunk):
            xn_ref[r:r + row_chunk, :] = _rmsnorm_rows(x_ref[r:r + row_chunk, :], gain).astype(BF16)
            o_ref[r:r + row_chunk, :] = x_ref[r:r + row_chunk, :]
        if n_extra:
            xn_ref[tm:tm + n_extra, :] = _rmsnorm_rows(e_ref[...], gain).astype(BF16)
            oe_ref[...] = e_ref[...]

    wg = wg_ref[...].astype(BF16)
    wu = wu_ref[...].astype(BF16)
    wd = (0.5 * wd_ref[...]).astype(BF16)
    wgb_ref[...] = wg
    wub_ref[...] = wu
    wdb_ref[...] = wd
    _swiglu_accumulate(xn_ref, wg, wu, lambda c0, c1: wd[:, c0:c1], o_ref, oe_ref,
                       tm=tm, col_chunk=col_chunk)


def _ffn_tail_kernel(*refs, tm, side_rows, side_chunks, row_chunk, col_chunk):
    x_ref, g_ref, wg_ref, wu_ref, wd_ref = refs[:5]
    xn_ref = refs[-1]
    i, f, nf = pl.program_id(0), pl.program_id(1), pl.num_programs(1)
    if side_rows:
        side_ref, o_ref, sideb_ref = refs[5], refs[6], refs[7]
    else:
        o_ref = refs[5]

    @pl.when(f == 0)
    def _():
        gain = g_ref[...]
        for r in range(0, tm, row_chunk):
            xn_ref[r:r + row_chunk, :] = _rmsnorm_rows(x_ref[r:r + row_chunk, :], gain).astype(BF16)

    def step(first):
        if side_rows:
            chunk = jnp.minimum(i * nf + f, side_chunks - 1)
            row = (chunk * side_ref.shape[0]
                   + lax.broadcasted_iota(jnp.int32, (side_ref.shape[0], 1), 0))
            sideb_ref[...] = jnp.where(row < side_rows, side_ref[...], 0.0).T.astype(BF16)
        _swiglu_accumulate(xn_ref, wg_ref[...], wu_ref[...], lambda c0, c1: wd_ref[:, c0:c1],
                           o_ref, None, tm=tm, col_chunk=col_chunk,
                           init_ref=x_ref if first else None)

    pl.when(f == 0)(functools.partial(step, True))
    pl.when(f > 0)(functools.partial(step, False))


def _ffn(x, extra, gain, wg, wu, wd, side_t=None, *, in_place, tm=1024, tf_head=256, tf=512,
         side_chunk=LANES):
    n, d = x.shape
    dff = wg.shape[1]
    n_extra = 0 if extra is None else extra.shape[0]
    n_tiles = n // tm
    assert n % tm == 0 and dff % tf == 0 and dff % tf_head == 0 and n_tiles >= 2
    assert not (in_place and side_t is not None)

    in_specs = [pl.BlockSpec((tm, d), lambda f: (0, 0), pipeline_mode=pl.Buffered(1))]
    args = [x]
    if n_extra:
        in_specs.append(pl.BlockSpec((n_extra, d), lambda f: (0, 0)))
        args.append(extra)
    in_specs += [
        pl.BlockSpec((1, d), lambda f: (0, 0)),
        pl.BlockSpec((d, tf_head), lambda f: (0, f)),
        pl.BlockSpec((d, tf_head), lambda f: (0, f)),
        pl.BlockSpec((tf_head, d), lambda f: (f, 0)),
    ]
    args += [gain, wg, wu, wd]
    out_shape = [jax.ShapeDtypeStruct((n if in_place else tm, d), F32)]
    out_specs = [pl.BlockSpec((tm, d), lambda f: (0, 0))]
    if n_extra:
        out_shape.append(jax.ShapeDtypeStruct((n_extra, d), F32))
        out_specs.append(pl.BlockSpec((n_extra, d), lambda f: (0, 0)))
    out_shape += [jax.ShapeDtypeStruct(wg.shape, BF16), jax.ShapeDtypeStruct(wu.shape, BF16),
                  jax.ShapeDtypeStruct(wd.shape, BF16)]
    out_specs += [pl.BlockSpec((d, tf_head), lambda f: (0, f)),
                  pl.BlockSpec((d, tf_head), lambda f: (0, f)),
                  pl.BlockSpec((tf_head, d), lambda f: (f, 0))]
    head = pl.pallas_call(
        functools.partial(_ffn_head_kernel, tm=tm, n_extra=n_extra, row_chunk=256, col_chunk=512),
        grid=(dff // tf_head,),
        in_specs=in_specs,
        out_specs=out_specs,
        out_shape=out_shape,
        scratch_shapes=[pltpu.VMEM((tm + n_extra, d), BF16)],
        input_output_aliases={0: 0} if in_place else {},
        compiler_params=_cparams(1),
        name="ffn_head",
    )(*args)
    out_head = head[0]
    out_extra = head[1] if n_extra else None
    wgb, wub, wdb = head[-3:]

    nf = dff // tf
    x_tail = out_head if in_place else x
    out_block = (lambda i, f: (i + 1, 0)) if in_place else (lambda i, f: (i, 0))
    in_specs = [
        pl.BlockSpec((tm, d), lambda i, f: (i + 1, 0)),
        pl.BlockSpec((1, d), lambda i, f: (0, 0)),
        pl.BlockSpec((d, tf), lambda i, f: (0, f)),
        pl.BlockSpec((d, tf), lambda i, f: (0, f)),
        pl.BlockSpec((tf, d), lambda i, f: (f, 0)),
    ]
    args = [x_tail, gain, wgb, wub, wdb]
    out_shape = [jax.ShapeDtypeStruct((n if in_place else n - tm, d), F32)]
    out_specs = [pl.BlockSpec((tm, d), out_block)]
    n_chunks = 0
    if side_t is not None:
        cols, rows = side_t.shape
        n_chunks = pl.cdiv(cols, side_chunk)
        assert n_chunks <= (n_tiles - 1) * nf
        chunk_of = lambda i, f: jnp.minimum(i * nf + f, n_chunks - 1)
        in_specs.append(pl.BlockSpec((side_chunk, rows), lambda i, f: (chunk_of(i, f), 0)))
        args.append(side_t)
        out_shape.append(jax.ShapeDtypeStruct((rows, n_chunks * side_chunk), BF16))
        out_specs.append(pl.BlockSpec((rows, side_chunk), lambda i, f: (0, chunk_of(i, f))))
    tail = pl.pallas_call(
        functools.partial(_ffn_tail_kernel, tm=tm,
                          side_rows=0 if side_t is None else side_t.shape[0],
                          side_chunks=n_chunks, row_chunk=256, col_chunk=512),
        grid=(n_tiles - 1, nf),
        in_specs=in_specs,
        out_specs=out_specs,
        out_shape=out_shape,
        scratch_shapes=[pltpu.VMEM((tm, d), BF16)],
        input_output_aliases={0: 0} if in_place else {},
        compiler_params=_cparams(2),
        name="ffn_tail",
    )(*args)
    if in_place:
        return tail[0], out_extra
    return (out_head, tail[0]), out_extra, tail[1]


def _log_sigmoid(x):
    return jnp.minimum(x, 0.0) - jnp.log1p(jnp.exp(-jnp.abs(x)))


def _split3(x):
    hi = x.astype(BF16)
    r1 = x - hi.astype(F32)
    mid = r1.astype(BF16)
    lo = (r1 - mid.astype(F32)).astype(BF16)
    return hi, mid, lo


def _cumsum_rows(x, tri, head_lane):
    hi, mid, lo = (p.astype(F32) for p in _split3(x))
    packed = hi + pltpu.roll(mid, SUBLANES, axis=1) + pltpu.roll(lo, 2 * SUBLANES, axis=1)
    c = _dot(tri, packed.astype(BF16))
    c = c + pltpu.roll(c, LANES - SUBLANES, axis=1) + pltpu.roll(c, LANES - 2 * SUBLANES, axis=1)
    return jnp.where(head_lane, c, 0.0)


def _bias_columns(cum):
    hi, mid, lo = (p.astype(F32) for p in _split3(cum * BIAS_SCALE))
    roll = lambda x, g: pltpu.roll(x, g * SUBLANES, axis=1)
    cols = hi + roll(mid, 1) + roll(lo, 2) - roll(hi, 3) - roll(mid, 4) - roll(lo, 5)
    return cols.astype(BF16)


def _head_rmsnorm(z, gain, n_heads):
    outs = []
    for h in range(n_heads):
        outs.append(_rmsnorm_rows(z[:, h * HEAD_DIM:(h + 1) * HEAD_DIM], gain))
    return jnp.concatenate(outs, axis=-1)


def _window_means(e, w):
    assert w & (w - 1) == 0 and w <= N_META
    s, span = e, 1
    while span < w:
        s = s + pltpu.roll(s, span, axis=0)
        span *= 2
    return s[N_META:, :] / float(w)


def _inproj_kernel(ha_ref, hb_ref, hm_ref, gn_ref, w_ref, b_ref, qn_ref, kn_ref, pw_ref, ps_ref,
                   side_ref,
                   pool_ref, q_ref, k_ref, v_ref, aug_ref, km_ref, vm_ref, augm_ref, sideb_ref,
                   u_ref, pe_ref, pmeta_ref, ccol_ref, mcol_ref,
                   *, tm, tiles_a, tiles_per_batch, n_heads, d_pool, d_att, row_chunk):
    i = pl.program_id(0)
    gain = gn_ref[...]
    sideb_ref[...] = side_ref[...].astype(BF16)
    head_lane = lax.broadcasted_iota(jnp.int32, (1, LANES), 1) < n_heads
    w_pool = lambda: w_ref[:, 0:d_pool]
    w_q = lambda: w_ref[:, d_pool:d_pool + d_att]
    w_k = lambda: w_ref[:, d_pool + d_att:d_pool + 2 * d_att]
    w_vf = lambda: w_ref[:, d_pool + 2 * d_att:]

    def log_forget(zvf):
        lf = _log_sigmoid(zvf[:, d_att:] + b_ref[...])
        return jnp.where(head_lane, lf, 0.0)

    def tri(t):
        r_i = lax.broadcasted_iota(jnp.int32, (t, t), 0)
        c_i = lax.broadcasted_iota(jnp.int32, (t, t), 1)
        return (c_i <= r_i).astype(BF16)

    @pl.when(i == 0)
    def _():
        u_ref[0:META_PAD, :] = jnp.zeros((META_PAD, u_ref.shape[1]), BF16)
        u_ref[0:N_META, :] = _rmsnorm_rows(hm_ref[...], gain).astype(BF16)
        um = u_ref[0:META_PAD, :]
        row_valid = lax.broadcasted_iota(jnp.int32, (META_PAD, 1), 0) < N_META
        pmeta_ref[...] = _dot(um, w_pool())[0:N_META, :]
        km = _head_rmsnorm(_dot(um, w_k()), kn_ref[...], n_heads)
        km_ref[...] = jnp.where(row_valid, km, 0.0).astype(BF16)
        zvf = _dot(um, w_vf())
        vm_ref[...] = zvf[:, 0:d_att].astype(BF16)
        cum_m = _cumsum_rows(jnp.where(row_valid, log_forget(zvf), 0.0), tri(META_PAD), head_lane)
        augm_ref[...] = _bias_columns(cum_m)
        mcol_ref[...] = cum_m[META_PAD - 1:META_PAD, :]

    @pl.when(i % tiles_per_batch == 0)
    def _():
        ccol_ref[...] = mcol_ref[...]
        pe_ref[0:N_META, :] = pmeta_ref[...]

    for r in range(0, tm, row_chunk):
        h = jnp.where(i < tiles_a, ha_ref[r:r + row_chunk, :], hb_ref[r:r + row_chunk, :])
        u_ref[r:r + row_chunk, :] = _rmsnorm_rows(h, gain).astype(BF16)
    u = u_ref[0:tm, :]

    pe_ref[N_META:N_META + tm, :] = _dot(u, w_pool())
    q_ref[...] = _head_rmsnorm(_dot(u, w_q()), qn_ref[...], n_heads).astype(BF16)
    k_ref[...] = _head_rmsnorm(_dot(u, w_k()), kn_ref[...], n_heads).astype(BF16)
    zvf = _dot(u, w_vf())
    v_ref[...] = zvf[:, 0:d_att].astype(BF16)

    gw = d_pool // len(POOL_WINDOWS)
    for g, w in enumerate(POOL_WINDOWS):
        lanes = slice(g * gw, (g + 1) * gw)
        e = pe_ref[:, lanes]
        pooled = (_window_means(e, w) - e[N_META:, :]).astype(BF16)
        mixed = _dot(pooled, pw_ref[g]) * ps_ref[:, lanes]
        pool_ref[:, lanes] = mixed.astype(BF16)
    pe_ref[0:N_META, :] = pe_ref[tm:tm + N_META, :]

    cum_col = _cumsum_rows(log_forget(zvf), tri(tm), head_lane) + ccol_ref[...]
    aug_ref[...] = _bias_columns(cum_col)
    ccol_ref[...] = cum_col[tm - 1:tm, :]


def _const_spec(shape, index=None):
    nd = len(shape)
    index = (0,) * nd if index is None else index
    return pl.BlockSpec(shape, lambda i: index, pipeline_mode=pl.Buffered(1))


def _in_proj(h1, h1m, gn, w_in, b_gate, qn, kn, pw, ps, side, *, batch, d_pool, d_att, tm=512):
    h1a, h1b = h1
    d = h1a.shape[1]
    n = h1a.shape[0] + h1b.shape[0]
    assert h1a.shape[0] % tm == 0 and h1b.shape[0] % tm == 0
    tiles_a = h1a.shape[0] // tm
    seq = n // batch
    n_heads = d_att // HEAD_DIM
    assert seq % tm == 0 and tm >= META_PAD and d_pool == d_att and n_heads <= SUBLANES
    assert w_in.shape == (d, d_pool + 3 * d_att + LANES)
    tiles_per_batch = seq // tm
    side_rows = side.shape[0] // (n // tm)
    assert side_rows * (n // tm) == side.shape[0] and side_rows % SUBLANES == 0
    side_spec = pl.BlockSpec((side_rows, side.shape[1]), lambda i: (i, 0))
    rows = lambda w: pl.BlockSpec((tm, w), lambda i: (i, 0))
    in_specs = [
        pl.BlockSpec((tm, d), lambda i: (jnp.minimum(i, tiles_a - 1), 0)),
        pl.BlockSpec((tm, d), lambda i: (jnp.maximum(i - tiles_a, 0), 0)),
        _const_spec((N_META, d)),
        _const_spec((1, d)),
        _const_spec(w_in.shape),
        _const_spec((1, LANES)),
        _const_spec((1, HEAD_DIM)),
        _const_spec((1, HEAD_DIM)),
        _const_spec(pw.shape),
        _const_spec((1, d_pool)),
        side_spec,
    ]
    out_shape = [
        jax.ShapeDtypeStruct((n, d_pool), BF16),
        jax.ShapeDtypeStruct((n, d_att), BF16),
        jax.ShapeDtypeStruct((n, d_att), BF16),
        jax.ShapeDtypeStruct((n, d_att), BF16),
        jax.ShapeDtypeStruct((n, LANES), BF16),
        jax.ShapeDtypeStruct((META_PAD, d_att), BF16),
        jax.ShapeDtypeStruct((META_PAD, d_att), BF16),
        jax.ShapeDtypeStruct((META_PAD, LANES), BF16),
        jax.ShapeDtypeStruct(side.shape, BF16),
    ]
    out_specs = [
        rows(d_pool), rows(d_att), rows(d_att), rows(d_att), rows(LANES),
        pl.BlockSpec((META_PAD, d_att), lambda i: (0, 0)),
        pl.BlockSpec((META_PAD, d_att), lambda i: (0, 0)),
        pl.BlockSpec((META_PAD, LANES), lambda i: (0, 0)),
        side_spec,
    ]
    scratch = [
        pltpu.VMEM((tm, d), BF16),
        pltpu.VMEM((tm + N_META, d_pool), F32),
        pltpu.VMEM((N_META, d_pool), F32),
        pltpu.VMEM((1, LANES), F32),
        pltpu.VMEM((1, LANES), F32),
    ]
    kern = functools.partial(_inproj_kernel, tm=tm, tiles_a=tiles_a, tiles_per_batch=tiles_per_batch,
                             n_heads=n_heads, d_pool=d_pool, d_att=d_att, row_chunk=256)
    return pl.pallas_call(
        kern,
        grid=(n // tm,),
        in_specs=in_specs,
        out_specs=out_specs,
        out_shape=out_shape,
        scratch_shapes=scratch,
        compiler_params=_cparams(1),
        name="in_proj",
    )(h1a, h1b, h1m, gn, w_in, b_gate, qn, kn, pw, ps, side)


def _attn_kernel(q_ref, k_ref, v_ref, km_ref, vm_ref, aug_ref, augm_ref, o_ref, qa_ref, ka_ref,
                 va_ref, *, tq):
    h = pl.program_id(1)
    seq = q_ref.shape[0]
    lane = lax.broadcasted_iota(jnp.int32, (1, LANES), 1)
    group = lane >> 3
    mine = (lane & (SUBLANES - 1)) == h
    one_hot = lambda lo: jnp.where(mine & (group >= lo) & (group < lo + 3), 1.0, 0.0).astype(BF16)
    keep = lambda lo: jnp.where((group >= lo) & (group < lo + 3), 0.0, 1.0).astype(BF16)
    qa_ref[:, 0:HEAD_DIM] = q_ref[...]
    qa_ref[:, HEAD_DIM:] = aug_ref[...] * keep(3) + one_hot(3)
    ka_ref[:, 0:HEAD_DIM] = k_ref[...]
    ka_ref[:, HEAD_DIM:] = aug_ref[...] * keep(0) + one_hot(0)
    kma = jnp.concatenate([km_ref[...], augm_ref[...] * keep(0) + one_hot(0)], axis=1)
    va_ref[:, 0:HEAD_DIM] = v_ref[...]
    va_ref[:, HEAD_DIM:] = jnp.ones((seq, HEAD_DIM), BF16)
    vma = jnp.concatenate([vm_ref[...], jnp.ones((META_PAD, HEAD_DIM), BF16)], axis=1)

    row_i = lax.broadcasted_iota(jnp.int32, (tq, tq), 0)
    col_i = lax.broadcasted_iota(jnp.int32, (tq, tq), 1)
    causal = col_i <= row_i
    tiles = list(reversed(range(seq // tq)))
    scores = {}
    for qi in tiles:
        rows = slice(qi * tq, (qi + 1) * tq)
        q = qa_ref[rows, :]
        blocks = [slice(j * tq, (j + 1) * tq) for j in range(qi + 1)]
        raw_m = jnp.where(lane < N_META, _dot_nt(q, kma), NEG_INF)
        raws = [_dot_nt(q, ka_ref[cols, :]) for cols in blocks]
        raws[-1] = jnp.where(causal, raws[-1], NEG_INF)
        scores[qi] = (rows, blocks, raw_m, raws)
    probs = {}
    for qi in tiles:
        rows, blocks, raw_m, raws = scores[qi]
        m = jnp.maximum(jnp.max(functools.reduce(jnp.maximum, raws), axis=-1, keepdims=True),
                        jnp.max(raw_m, axis=-1, keepdims=True))
        p_m = jnp.exp2(EXP2_SCALE * (raw_m - m))
        ps = [jnp.exp2(EXP2_SCALE * (r - m)) for r in raws]
        probs[qi] = (rows, blocks, p_m, ps)
    for qi in tiles:
        rows, blocks, p_m, ps = probs[qi]
        acc = _dot(p_m.astype(BF16), vma)
        for p, cols in zip(ps, blocks):
            acc = acc + _dot(p.astype(BF16), va_ref[cols, :])
        o_ref[rows, :] = (acc[:, 0:HEAD_DIM] / acc[:, HEAD_DIM:]).astype(BF16)


def _attention(q, k, v, km, vm, aug, augm, *, batch, tq=512):
    n, d_att = q.shape
    seq = n // batch
    n_heads = d_att // HEAD_DIM
    tok = pl.BlockSpec((seq, HEAD_DIM), lambda b, h: (b, h))
    meta = pl.BlockSpec((META_PAD, HEAD_DIM), lambda b, h: (0, h))
    return pl.pallas_call(
        functools.partial(_attn_kernel, tq=tq),
        grid=(batch, n_heads),
        in_specs=[
            tok, tok, tok, meta, meta,
            pl.BlockSpec((seq, LANES), lambda b, h: (b, 0)),
            pl.BlockSpec((META_PAD, LANES), lambda b, h: (0, 0)),
        ],
        out_specs=tok,
        out_shape=jax.ShapeDtypeStruct((n, d_att), BF16),
        scratch_shapes=[pltpu.VMEM((seq, 2 * HEAD_DIM), BF16),
                        pltpu.VMEM((seq, 2 * HEAD_DIM), BF16),
                        pltpu.VMEM((seq, 2 * HEAD_DIM), BF16)],
        compiler_params=_cparams(2),
        name="fox_attn",
    )(q, k, v, km, vm, aug, augm)


def _outproj_kernel(ha_ref, hb_ref, pool_ref, att_ref, wop_ref, woa_ref, o_ref, *, tiles_a,
                    col_chunk):
    i = pl.program_id(0)
    d = o_ref.shape[1]
    for c in range(0, d, col_chunk):
        cols = slice(c, c + col_chunk)
        h = jnp.where(i < tiles_a, ha_ref[:, cols], hb_ref[:, cols])
        o_ref[:, cols] = (h + _dot(pool_ref[...], wop_ref[:, cols])
                          + _dot(att_ref[...], woa_ref[:, cols]))


def _out_proj(h1, pool, att, w_out, *, tm=512):
    h1a, h1b = h1
    d = h1a.shape[1]
    n = h1a.shape[0] + h1b.shape[0]
    assert h1a.shape[0] % tm == 0 and h1b.shape[0] % tm == 0
    tiles_a = h1a.shape[0] // tm
    d_pool, d_att = pool.shape[1], att.shape[1]
    assert d_pool == d_att
    rows = lambda w: pl.BlockSpec((tm, w), lambda i: (i, 0))
    return pl.pallas_call(
        functools.partial(_outproj_kernel, tiles_a=tiles_a, col_chunk=512),
        grid=(n // tm,),
        in_specs=[pl.BlockSpec((tm, d), lambda i: (jnp.minimum(i, tiles_a - 1), 0)),
                  pl.BlockSpec((tm, d), lambda i: (jnp.maximum(i - tiles_a, 0), 0)),
                  rows(d_pool), rows(d_att),
                  _const_spec((d_pool, d), (0, 0)), _const_spec((d_att, d), (1, 0))],
        out_specs=rows(d),
        out_shape=jax.ShapeDtypeStruct((n, d), F32),
        compiler_params=_cparams(1),
        name="out_proj",
    )(h1a, h1b, pool, att, w_out, w_out)


def kernel(x, meta_tokens, ffn1_norm, ffn1_w_gate, ffn1_w_up, ffn1_w_down, mix_norm, w_in,
           b_forget, q_norm, k_norm, pool_w, pool_scale, w_out, ffn2_norm, ffn2_w_gate,
           ffn2_w_up, ffn2_w_down):
    batch, seq, d = x.shape
    n_heads = b_forget.shape[1]
    d_att = n_heads * HEAD_DIM
    d_pool = pool_scale.shape[1]
    assert meta_tokens.shape[0] == N_META
    assert ffn1_norm.shape[0] == 1, "one layer: the meta rows past the mixers never reach the output"

    hx = x.reshape(batch * seq, d)
    hm = meta_tokens.astype(x.dtype)
    h1, hm, w_in_b = _ffn(hx, hm, ffn1_norm[0][None], ffn1_w_gate[0], ffn1_w_up[0],
                          ffn1_w_down[0], side_t=w_in[0].T, in_place=False)

    b_gate = jnp.pad(b_forget[0][None], ((0, 0), (0, LANES - n_heads)))
    pool, q, k, v, aug, km, vm, augm, w_out_b = _in_proj(
        h1, hm, mix_norm[0][None], w_in_b, b_gate, q_norm[0][None], k_norm[0][None],
        pool_w[0].astype(BF16), pool_scale[0][None], w_out[0],
        batch=batch, d_pool=d_pool, d_att=d_att)

    att = _attention(q, k, v, km, vm, aug, augm, batch=batch)
    h2 = _out_proj(h1, pool, att, w_out_b)

    hx, _ = _ffn(h2, None, ffn2_norm[0][None], ffn2_w_gate[0], ffn2_w_up[0], ffn2_w_down[0],
                 in_place=True)
    return hx.reshape(batch, seq, d)
```

```python
import functools

import jax
import jax.numpy as jnp
from jax import lax
from jax.experimental import pallas as pl
from jax.experimental.pallas import tpu as pltpu

F32 = jnp.float32
BF16 = jnp.bfloat16

EPS = 1e-6
N_META = 16
POOL_WINDOWS = (2, 4, 8, 16)
HEAD_DIM = 128
LANES = 128
SUBLANES = 8
META_PAD = 128
NEG_INF = float("-inf")
BIAS_SCALE = HEAD_DIM ** 0.5
EXP2_SCALE = 1.4426950408889634 / BIAS_SCALE

VMEM_LIMIT = 60 * 1024 * 1024


def _cparams(n_axes):
    return pltpu.CompilerParams(
        dimension_semantics=("arbitrary",) * n_axes,
        vmem_limit_bytes=VMEM_LIMIT,
    )


def _rmsnorm_rows(x, gain):
    ms = jnp.mean(x * x, axis=-1, keepdims=True)
    return x * lax.rsqrt(ms + EPS) * gain


def _dot(a, b):
    return jnp.dot(a, b, preferred_element_type=F32)


def _dot_nt(a, b):
    return lax.dot_general(a, b, (((1,), (1,)), ((), ())), preferred_element_type=F32)


def _swiglu_accumulate(xn_ref, wg, wu, wd_cols, o_ref, oe_ref, *, tm, col_chunk, init_ref=None):
    d = o_ref.shape[1]
    xn = xn_ref[...]
    gate = _dot(xn, wg)
    up = _dot(xn, wu)
    hmid = (gate / (1.0 + jnp.exp(-gate)) * up).astype(BF16)
    for c in range(0, d, col_chunk):
        cols = slice(c, c + col_chunk)
        part = _dot(hmid, wd_cols(c, c + col_chunk))
        base = o_ref if init_ref is None else init_ref
        o_ref[:, cols] = base[:, cols] + part[:tm]
        if oe_ref is not None:
            oe_ref[:, cols] += part[tm:]


def _ffn_head_kernel(*refs, tm, n_extra, row_chunk, col_chunk):
    if n_extra:
        (x_ref, e_ref, g_ref, wg_ref, wu_ref, wd_ref,
         o_ref, oe_ref, wgb_ref, wub_ref, wdb_ref, xn_ref) = refs
    else:
        x_ref, g_ref, wg_ref, wu_ref, wd_ref, o_ref, wgb_ref, wub_ref, wdb_ref, xn_ref = refs
        e_ref = oe_ref = None
    f = pl.program_id(0)

    @pl.when(f == 0)
    def _():
        gain = g_ref[...]
        for r in range(0, tm, row_chunk):
            xn_ref[r:r + row_chunk, :] = _rmsnorm_rows(x_ref[r:r + row_chunk, :], gain).astype(BF16)
            o_ref[r:r + row_chunk, :] = x_ref[r:r + row_chunk, :]
        if n_extra:
            xn_ref[tm:tm + n_extra, :] = _rmsnorm_rows(e_ref[...], gain).astype(BF16)
            oe_ref[...] = e_ref[...]

    wg = wg_ref[...].astype(BF16)
    wu = wu_ref[...].astype(BF16)
    wd = (0.5 * wd_ref[...]).astype(BF16)
    wgb_ref[...] = wg
    wub_ref[...] = wu
    wdb_ref[...] = wd
    _swiglu_accumulate(xn_ref, wg, wu, lambda c0, c1: wd[:, c0:c1], o_ref, oe_ref,
                       tm=tm, col_chunk=col_chunk)


def _ffn_tail_kernel(*refs, tm, side_rows, side_chunks, row_chunk, col_chunk):
    x_ref, g_ref, wg_ref, wu_ref, wd_ref = refs[:5]
    xn_ref = refs[-1]
    i, f, nf = pl.program_id(0), pl.program_id(1), pl.num_programs(1)
    if side_rows:
        side_ref, o_ref, sideb_ref = refs[5], refs[6], refs[7]
    else:
        o_ref = refs[5]

    @pl.when(f == 0)
    def _():
        gain = g_ref[...]
        for r in range(0, tm, row_chunk):
            xn_ref[r:r + row_chunk, :] = _rmsnorm_rows(x_ref[r:r + row_chunk, :], gain).astype(BF16)

    def step(first):
        if side_rows:
            chunk = jnp.minimum(i * nf + f, side_chunks - 1)
            row = (chunk * side_ref.shape[0]
                   + lax.broadcasted_iota(jnp.int32, (side_ref.shape[0], 1), 0))
            sideb_ref[...] = jnp.where(row < side_rows, side_ref[...], 0.0).T.astype(BF16)
        _swiglu_accumulate(xn_ref, wg_ref[...], wu_ref[...], lambda c0, c1: wd_ref[:, c0:c1],
                           o_ref, None, tm=tm, col_chunk=col_chunk,
                           init_ref=x_ref if first else None)

    pl.when(f == 0)(functools.partial(step, True))
    pl.when(f > 0)(functools.partial(step, False))


def _ffn_head(x, extra, gain, wg, wu, wd, *, tm=1024, tf=256):
    n, d = x.shape
    dff = wg.shape[1]
    n_extra = 0 if extra is None else extra.shape[0]
    assert n % tm == 0 and dff % tf == 0
    in_specs = [pl.BlockSpec((tm, d), lambda f: (0, 0), pipeline_mode=pl.Buffered(1))]
    args = [x]
    if n_extra:
        in_specs.append(pl.BlockSpec((n_extra, d), lambda f: (0, 0)))
        args.append(extra)
    in_specs += [
        pl.BlockSpec((1, d), lambda f: (0, 0)),
        pl.BlockSpec((d, tf), lambda f: (0, f)),
        pl.BlockSpec((d, tf), lambda f: (0, f)),
        pl.BlockSpec((tf, d), lambda f: (f, 0)),
    ]
    args += [gain, wg, wu, wd]
    out_shape = [jax.ShapeDtypeStruct((tm, d), F32)]
    out_specs = [pl.BlockSpec((tm, d), lambda f: (0, 0))]
    if n_extra:
        out_shape.append(jax.ShapeDtypeStruct((n_extra, d), F32))
        out_specs.append(pl.BlockSpec((n_extra, d), lambda f: (0, 0)))
    out_shape += [jax.ShapeDtypeStruct(wg.shape, BF16), jax.ShapeDtypeStruct(wu.shape, BF16),
                  jax.ShapeDtypeStruct(wd.shape, BF16)]
    out_specs += [pl.BlockSpec((d, tf), lambda f: (0, f)),
                  pl.BlockSpec((d, tf), lambda f: (0, f)),
                  pl.BlockSpec((tf, d), lambda f: (f, 0))]
    head = pl.pallas_call(
        functools.partial(_ffn_head_kernel, tm=tm, n_extra=n_extra, row_chunk=256, col_chunk=512),
        grid=(dff // tf,),
        in_specs=in_specs,
        out_specs=out_specs,
        out_shape=out_shape,
        scratch_shapes=[pltpu.VMEM((tm + n_extra, d), BF16)],
        compiler_params=_cparams(1),
        name="ffn_head",
    )(*args)
    return head[0], (head[1] if n_extra else None), tuple(head[-3:])


def _ffn_tail(x, gain, wgb, wub, wdb, *, first_tile, in_place, side_t=None, tm=1024, tf=512,
              side_chunk=LANES):
    n, d = x.shape
    dff = wgb.shape[1]
    assert n % tm == 0 and dff % tf == 0
    n_tiles = n // tm - first_tile
    nf = dff // tf
    in_specs = [
        pl.BlockSpec((tm, d), lambda i, f: (i + first_tile, 0)),
        pl.BlockSpec((1, d), lambda i, f: (0, 0)),
        pl.BlockSpec((d, tf), lambda i, f: (0, f)),
        pl.BlockSpec((d, tf), lambda i, f: (0, f)),
        pl.BlockSpec((tf, d), lambda i, f: (f, 0)),
    ]
    args = [x, gain, wgb, wub, wdb]
    out_shape = [jax.ShapeDtypeStruct((n if in_place else n_tiles * tm, d), F32)]
    out_specs = [pl.BlockSpec((tm, d), lambda i, f: (i + (first_tile if in_place else 0), 0))]
    n_chunks = 0
    if side_t is not None:
        cols, rows = side_t.shape
        n_chunks = pl.cdiv(cols, side_chunk)
        assert n_chunks <= n_tiles * nf
        chunk_of = lambda i, f: jnp.minimum(i * nf + f, n_chunks - 1)
        in_specs.append(pl.BlockSpec((side_chunk, rows), lambda i, f: (chunk_of(i, f), 0)))
        args.append(side_t)
        out_shape.append(jax.ShapeDtypeStruct((rows, n_chunks * side_chunk), BF16))
        out_specs.append(pl.BlockSpec((rows, side_chunk), lambda i, f: (0, chunk_of(i, f))))
    tail = pl.pallas_call(
        functools.partial(_ffn_tail_kernel, tm=tm,
                          side_rows=0 if side_t is None else side_t.shape[0],
                          side_chunks=n_chunks, row_chunk=256, col_chunk=512),
        grid=(n_tiles, nf),
        in_specs=in_specs,
        out_specs=out_specs,
        out_shape=out_shape,
        scratch_shapes=[pltpu.VMEM((tm, d), BF16)],
        input_output_aliases={0: 0} if in_place else {},
        compiler_params=_cparams(2),
        name="ffn_tail",
    )(*args)
    return tail[0] if side_t is None else (tail[0], tail[1])


def _log_sigmoid(x):
    return jnp.minimum(x, 0.0) - jnp.log1p(jnp.exp(-jnp.abs(x)))


def _split3(x):
    hi = x.astype(BF16)
    r1 = x - hi.astype(F32)
    mid = r1.astype(BF16)
    lo = (r1 - mid.astype(F32)).astype(BF16)
    return hi, mid, lo


def _cumsum_rows(x, tri, head_lane):
    hi, mid, lo = (p.astype(F32) for p in _split3(x))
    packed = hi + pltpu.roll(mid, SUBLANES, axis=1) + pltpu.roll(lo, 2 * SUBLANES, axis=1)
    c = _dot(tri, packed.astype(BF16))
    c = c + pltpu.roll(c, LANES - SUBLANES, axis=1) + pltpu.roll(c, LANES - 2 * SUBLANES, axis=1)
    return jnp.where(head_lane, c, 0.0)


def _bias_columns(cum):
    hi, mid, lo = (p.astype(F32) for p in _split3(cum * BIAS_SCALE))
    roll = lambda x, g: pltpu.roll(x, g * SUBLANES, axis=1)
    cols = hi + roll(mid, 1) + roll(lo, 2) - roll(hi, 3) - roll(mid, 4) - roll(lo, 5)
    return cols.astype(BF16)


def _head_rmsnorm(z, gain, n_heads):
    outs = []
    for h in range(n_heads):
        outs.append(_rmsnorm_rows(z[:, h * HEAD_DIM:(h + 1) * HEAD_DIM], gain))
    return jnp.concatenate(outs, axis=-1)


def _window_means(e, w):
    assert w & (w - 1) == 0 and w <= N_META
    s, span = e, 1
    while span < w:
        s = s + pltpu.roll(s, span, axis=0)
        span *= 2
    return s[N_META:, :] / float(w)


def _inproj_kernel(ha_ref, hb_ref, hm_ref, gn_ref, w_ref, b_ref, qn_ref, kn_ref, pw_ref, ps_ref,
                   side_ref,
                   pool_ref, q_ref, k_ref, v_ref, aug_ref, km_ref, vm_ref, augm_ref, sideb_ref,
                   u_ref, pe_ref, pmeta_ref, ccol_ref, mcol_ref,
                   *, tm, tiles_a, tiles_per_batch, n_heads, d_pool, d_att, row_chunk):
    i = pl.program_id(0)
    gain = gn_ref[...]
    sideb_ref[...] = side_ref[...].astype(BF16)
    head_lane = lax.broadcasted_iota(jnp.int32, (1, LANES), 1) < n_heads
    w_pool = lambda: w_ref[:, 0:d_pool]
    w_q = lambda: w_ref[:, d_pool:d_pool + d_att]
    w_k = lambda: w_ref[:, d_pool + d_att:d_pool + 2 * d_att]
    w_vf = lambda: w_ref[:, d_pool + 2 * d_att:]

    def log_forget(zvf):
        lf = _log_sigmoid(zvf[:, d_att:] + b_ref[...])
        return jnp.where(head_lane, lf, 0.0)

    def tri(t):
        r_i = lax.broadcasted_iota(jnp.int32, (t, t), 0)
        c_i = lax.broadcasted_iota(jnp.int32, (t, t), 1)
        return (c_i <= r_i).astype(BF16)

    @pl.when(i == 0)
    def _():
        u_ref[0:META_PAD, :] = jnp.zeros((META_PAD, u_ref.shape[1]), BF16)
        u_ref[0:N_META, :] = _rmsnorm_rows(hm_ref[...], gain).astype(BF16)
        um = u_ref[0:META_PAD, :]
        row_valid = lax.broadcasted_iota(jnp.int32, (META_PAD, 1), 0) < N_META
        pmeta_ref[...] = _dot(um, w_pool())[0:N_META, :]
        km = _head_rmsnorm(_dot(um, w_k()), kn_ref[...], n_heads)
        km_ref[...] = jnp.where(row_valid, km, 0.0).astype(BF16)
        zvf = _dot(um, w_vf())
        vm_ref[...] = zvf[:, 0:d_att].astype(BF16)
        cum_m = _cumsum_rows(jnp.where(row_valid, log_forget(zvf), 0.0), tri(META_PAD), head_lane)
        augm_ref[...] = _bias_columns(cum_m)
        mcol_ref[...] = cum_m[META_PAD - 1:META_PAD, :]

    @pl.when(i % tiles_per_batch == 0)
    def _():
        ccol_ref[...] = mcol_ref[...]
        pe_ref[0:N_META, :] = pmeta_ref[...]

    for r in range(0, tm, row_chunk):
        h = jnp.where(i < tiles_a, ha_ref[r:r + row_chunk, :], hb_ref[r:r + row_chunk, :])
        u_ref[r:r + row_chunk, :] = _rmsnorm_rows(h, gain).astype(BF16)
    u = u_ref[0:tm, :]

    pe_ref[N_META:N_META + tm, :] = _dot(u, w_pool())
    q_ref[...] = _head_rmsnorm(_dot(u, w_q()), qn_ref[...], n_heads).astype(BF16)
    k_ref[...] = _head_rmsnorm(_dot(u, w_k()), kn_ref[...], n_heads).astype(BF16)
    zvf = _dot(u, w_vf())
    v_ref[...] = zvf[:, 0:d_att].astype(BF16)

    gw = d_pool // len(POOL_WINDOWS)
    for g, w in enumerate(POOL_WINDOWS):
        lanes = slice(g * gw, (g + 1) * gw)
        e = pe_ref[:, lanes]
        pooled = (_window_means(e, w) - e[N_META:, :]).astype(BF16)
        mixed = _dot(pooled, pw_ref[g]) * ps_ref[:, lanes]
        pool_ref[:, lanes] = mixed.astype(BF16)
    pe_ref[0:N_META, :] = pe_ref[tm:tm + N_META, :]

    cum_col = _cumsum_rows(log_forget(zvf), tri(tm), head_lane) + ccol_ref[...]
    aug_ref[...] = _bias_columns(cum_col)
    ccol_ref[...] = cum_col[tm - 1:tm, :]


def _const_spec(shape, index=None):
    nd = len(shape)
    index = (0,) * nd if index is None else index
    return pl.BlockSpec(shape, lambda i: index, pipeline_mode=pl.Buffered(1))


def _in_proj(h1, h1m, gn, w_in, b_gate, qn, kn, pw, ps, side, *, batch, d_pool, d_att, tm=512):
    h1a, h1b = h1
    d = h1a.shape[1]
    n = h1a.shape[0] + h1b.shape[0]
    assert h1a.shape[0] % tm == 0 and h1b.shape[0] % tm == 0
    tiles_a = h1a.shape[0] // tm
    seq = n // batch
    n_heads = d_att // HEAD_DIM
    assert seq % tm == 0 and tm >= META_PAD and d_pool == d_att and n_heads <= SUBLANES
    assert w_in.shape == (d, d_pool + 3 * d_att + LANES)
    tiles_per_batch = seq // tm
    side_rows = side.shape[0] // (n // tm)
    assert side_rows * (n // tm) == side.shape[0] and side_rows % SUBLANES == 0
    side_spec = pl.BlockSpec((side_rows, side.shape[1]), lambda i: (i, 0))
    rows = lambda w: pl.BlockSpec((tm, w), lambda i: (i, 0))
    in_specs = [
        pl.BlockSpec((tm, d), lambda i: (jnp.minimum(i, tiles_a - 1), 0)),
        pl.BlockSpec((tm, d), lambda i: (jnp.maximum(i - tiles_a, 0), 0)),
        _const_spec((N_META, d)),
        _const_spec((1, d)),
        _const_spec(w_in.shape),
        _const_spec((1, LANES)),
        _const_spec((1, HEAD_DIM)),
        _const_spec((1, HEAD_DIM)),
        _const_spec(pw.shape),
        _const_spec((1, d_pool)),
        side_spec,
    ]
    out_shape = [
        jax.ShapeDtypeStruct((n, d_pool), BF16),
        jax.ShapeDtypeStruct((n, d_att), BF16),
        jax.ShapeDtypeStruct((n, d_att), BF16),
        jax.ShapeDtypeStruct((n, d_att), BF16),
        jax.ShapeDtypeStruct((n, LANES), BF16),
        jax.ShapeDtypeStruct((META_PAD, d_att), BF16),
        jax.ShapeDtypeStruct((META_PAD, d_att), BF16),
        jax.ShapeDtypeStruct((META_PAD, LANES), BF16),
        jax.ShapeDtypeStruct(side.shape, BF16),
    ]
    out_specs = [
        rows(d_pool), rows(d_att), rows(d_att), rows(d_att), rows(LANES),
        pl.BlockSpec((META_PAD, d_att), lambda i: (0, 0)),
        pl.BlockSpec((META_PAD, d_att), lambda i: (0, 0)),
        pl.BlockSpec((META_PAD, LANES), lambda i: (0, 0)),
        side_spec,
    ]
    scratch = [
        pltpu.VMEM((tm, d), BF16),
        pltpu.VMEM((tm + N_META, d_pool), F32),
        pltpu.VMEM((N_META, d_pool), F32),
        pltpu.VMEM((1, LANES), F32),
        pltpu.VMEM((1, LANES), F32),
    ]
    kern = functools.partial(_inproj_kernel, tm=tm, tiles_a=tiles_a, tiles_per_batch=tiles_per_batch,
                             n_heads=n_heads, d_pool=d_pool, d_att=d_att, row_chunk=256)
    return pl.pallas_call(
        kern,
        grid=(n // tm,),
        in_specs=in_specs,
        out_specs=out_specs,
        out_shape=out_shape,
        scratch_shapes=scratch,
        compiler_params=_cparams(1),
        name="in_proj",
    )(h1a, h1b, h1m, gn, w_in, b_gate, qn, kn, pw, ps, side)


def _attn_kernel(q_ref, k_ref, v_ref, km_ref, vm_ref, aug_ref, augm_ref, wg_ref, wu_ref, wd_ref,
                 o_ref, wgb_ref, wub_ref, wdb_ref, qa_ref, ka_ref, va_ref, *, tq):
    h = pl.program_id(1)
    seq = q_ref.shape[0]
    lane = lax.broadcasted_iota(jnp.int32, (1, LANES), 1)
    group = lane >> 3
    mine = (lane & (SUBLANES - 1)) == h
    one_hot = lambda lo: jnp.where(mine & (group >= lo) & (group < lo + 3), 1.0, 0.0).astype(BF16)
    keep = lambda lo: jnp.where((group >= lo) & (group < lo + 3), 0.0, 1.0).astype(BF16)
    qa_ref[:, 0:HEAD_DIM] = q_ref[...]
    qa_ref[:, HEAD_DIM:] = aug_ref[...] * keep(3) + one_hot(3)
    ka_ref[:, 0:HEAD_DIM] = k_ref[...]
    ka_ref[:, HEAD_DIM:] = aug_ref[...] * keep(0) + one_hot(0)
    kma = jnp.concatenate([km_ref[...], augm_ref[...] * keep(0) + one_hot(0)], axis=1)
    va_ref[:, 0:HEAD_DIM] = v_ref[...]
    va_ref[:, HEAD_DIM:] = jnp.ones((seq, HEAD_DIM), BF16)
    vma = jnp.concatenate([vm_ref[...], jnp.ones((META_PAD, HEAD_DIM), BF16)], axis=1)

    row_i = lax.broadcasted_iota(jnp.int32, (tq, tq), 0)
    col_i = lax.broadcasted_iota(jnp.int32, (tq, tq), 1)
    causal = col_i <= row_i
    tiles = list(reversed(range(seq // tq)))
    scores = {}
    for qi in tiles:
        rows = slice(qi * tq, (qi + 1) * tq)
        q = qa_ref[rows, :]
        blocks = [slice(j * tq, (j + 1) * tq) for j in range(qi + 1)]
        raw_m = jnp.where(lane < N_META, _dot_nt(q, kma), NEG_INF)
        raws = [_dot_nt(q, ka_ref[cols, :]) for cols in blocks]
        raws[-1] = jnp.where(causal, raws[-1], NEG_INF)
        scores[qi] = (rows, blocks, raw_m, raws)
    probs = {}
    for qi in tiles:
        rows, blocks, raw_m, raws = scores[qi]
        m = jnp.maximum(jnp.max(functools.reduce(jnp.maximum, raws), axis=-1, keepdims=True),
                        jnp.max(raw_m, axis=-1, keepdims=True))
        p_m = jnp.exp2(EXP2_SCALE * (raw_m - m))
        ps = [jnp.exp2(EXP2_SCALE * (r - m)) for r in raws]
        probs[qi] = (rows, blocks, p_m, ps)
    wgb_ref[...] = wg_ref[...].astype(BF16)
    wub_ref[...] = wu_ref[...].astype(BF16)
    wdb_ref[...] = (0.5 * wd_ref[...]).astype(BF16)
    for qi in tiles:
        rows, blocks, p_m, ps = probs[qi]
        acc = _dot(p_m.astype(BF16), vma)
        for p, cols in zip(ps, blocks):
            acc = acc + _dot(p.astype(BF16), va_ref[cols, :])
        o_ref[rows, :] = (acc[:, 0:HEAD_DIM] / acc[:, HEAD_DIM:]).astype(BF16)


def _attention(q, k, v, km, vm, aug, augm, next_ffn, *, batch, tq=512):
    n, d_att = q.shape
    seq = n // batch
    n_heads = d_att // HEAD_DIM
    steps = batch * n_heads
    tok = pl.BlockSpec((seq, HEAD_DIM), lambda b, h: (b, h))
    meta = pl.BlockSpec((META_PAD, HEAD_DIM), lambda b, h: (0, h))

    def chunk(w):
        assert w.shape[0] % (steps * 2 * SUBLANES) == 0
        return pl.BlockSpec((w.shape[0] // steps, w.shape[1]), lambda b, h: (b * n_heads + h, 0))

    side_specs = [chunk(w) for w in next_ffn]
    return pl.pallas_call(
        functools.partial(_attn_kernel, tq=tq),
        grid=(batch, n_heads),
        in_specs=[
            tok, tok, tok, meta, meta,
            pl.BlockSpec((seq, LANES), lambda b, h: (b, 0)),
            pl.BlockSpec((META_PAD, LANES), lambda b, h: (0, 0)),
        ] + side_specs,
        out_specs=[tok] + side_specs,
        out_shape=[jax.ShapeDtypeStruct((n, d_att), BF16)]
        + [jax.ShapeDtypeStruct(w.shape, BF16) for w in next_ffn],
        scratch_shapes=[pltpu.VMEM((seq, 2 * HEAD_DIM), BF16),
                        pltpu.VMEM((seq, 2 * HEAD_DIM), BF16),
                        pltpu.VMEM((seq, 2 * HEAD_DIM), BF16)],
        compiler_params=_cparams(2),
        name="fox_attn",
    )(q, k, v, km, vm, aug, augm, *next_ffn)


def _outproj_kernel(ha_ref, hb_ref, pool_ref, att_ref, wop_ref, woa_ref, o_ref, *, tiles_a,
                    col_chunk):
    i = pl.program_id(0)
    d = o_ref.shape[1]
    for c in range(0, d, col_chunk):
        cols = slice(c, c + col_chunk)
        h = jnp.where(i < tiles_a, ha_ref[:, cols], hb_ref[:, cols])
        o_ref[:, cols] = (h + _dot(pool_ref[...], wop_ref[:, cols])
                          + _dot(att_ref[...], woa_ref[:, cols]))


def _out_proj(h1, pool, att, w_out, *, tm=512):
    h1a, h1b = h1
    d = h1a.shape[1]
    n = h1a.shape[0] + h1b.shape[0]
    assert h1a.shape[0] % tm == 0 and h1b.shape[0] % tm == 0
    tiles_a = h1a.shape[0] // tm
    d_pool, d_att = pool.shape[1], att.shape[1]
    assert d_pool == d_att
    rows = lambda w: pl.BlockSpec((tm, w), lambda i: (i, 0))
    return pl.pallas_call(
        functools.partial(_outproj_kernel, tiles_a=tiles_a, col_chunk=512),
        grid=(n // tm,),
        in_specs=[pl.BlockSpec((tm, d), lambda i: (jnp.minimum(i, tiles_a - 1), 0)),
                  pl.BlockSpec((tm, d), lambda i: (jnp.maximum(i - tiles_a, 0), 0)),
                  rows(d_pool), rows(d_att),
                  _const_spec((d_pool, d), (0, 0)), _const_spec((d_att, d), (1, 0))],
        out_specs=rows(d),
        out_shape=jax.ShapeDtypeStruct((n, d), F32),
        compiler_params=_cparams(1),
        name="out_proj",
    )(h1a, h1b, pool, att, w_out, w_out)


def kernel(x, meta_tokens, ffn1_norm, ffn1_w_gate, ffn1_w_up, ffn1_w_down, mix_norm, w_in,
           b_forget, q_norm, k_norm, pool_w, pool_scale, w_out, ffn2_norm, ffn2_w_gate,
           ffn2_w_up, ffn2_w_down):
    batch, seq, d = x.shape
    n_heads = b_forget.shape[1]
    d_att = n_heads * HEAD_DIM
    d_pool = pool_scale.shape[1]
    assert meta_tokens.shape[0] == N_META
    assert ffn1_norm.shape[0] == 1, "one layer: the meta rows past the mixers never reach the output"

    hx = x.reshape(batch * seq, d)
    hm = meta_tokens.astype(x.dtype)
    g1 = ffn1_norm[0][None]
    h1_head, hm, w1_b = _ffn_head(hx, hm, g1, ffn1_w_gate[0], ffn1_w_up[0], ffn1_w_down[0])
    h1_tail, w_in_b = _ffn_tail(hx, g1, *w1_b, first_tile=1, in_place=False, side_t=w_in[0].T)
    h1 = (h1_head, h1_tail)

    b_gate = jnp.pad(b_forget[0][None], ((0, 0), (0, LANES - n_heads)))
    pool, q, k, v, aug, km, vm, augm, w_out_b = _in_proj(
        h1, hm, mix_norm[0][None], w_in_b, b_gate, q_norm[0][None], k_norm[0][None],
        pool_w[0].astype(BF16), pool_scale[0][None], w_out[0],
        batch=batch, d_pool=d_pool, d_att=d_att)

    att, *w2_b = _attention(q, k, v, km, vm, aug, augm,
                            (ffn2_w_gate[0], ffn2_w_up[0], ffn2_w_down[0]), batch=batch)
    h2 = _out_proj(h1, pool, att, w_out_b)

    out = _ffn_tail(h2, ffn2_norm[0][None], *w2_b, first_tile=0, in_place=True)
    return out.reshape(batch, seq, d)
```

```python
import functools

import jax
import jax.numpy as jnp
from jax import lax
from jax.experimental import pallas as pl
from jax.experimental.pallas import tpu as pltpu

F32 = jnp.float32
BF16 = jnp.bfloat16

EPS = 1e-6
N_META = 16
POOL_WINDOWS = (2, 4, 8, 16)
HEAD_DIM = 128
LANES = 128
SUBLANES = 8
META_PAD = 128
NEG_INF = float("-inf")
BIAS_SCALE = HEAD_DIM ** 0.5
EXP2_SCALE = 1.4426950408889634 / BIAS_SCALE

VMEM_LIMIT = 60 * 1024 * 1024


def _cparams(n_axes):
    return pltpu.CompilerParams(
        dimension_semantics=("arbitrary",) * n_axes,
        vmem_limit_bytes=VMEM_LIMIT,
    )


def _rmsnorm_rows(x, gain):
    ms = jnp.mean(x * x, axis=-1, keepdims=True)
    return x * lax.rsqrt(ms + EPS) * gain


def _dot(a, b):
    return jnp.dot(a, b, preferred_element_type=F32)


def _dot_nt(a, b):
    return lax.dot_general(a, b, (((1,), (1,)), ((), ())), preferred_element_type=F32)


def _swiglu_accumulate(xn_ref, wg, wu, wd_cols, o_ref, oe_ref, *, tm, col_chunk, init_ref=None,
                       fill_xn=None):
    d = o_ref.shape[1]
    m = xn_ref.shape[0]
    bounds = (0, m) if fill_xn is None else (0, m // 4, m // 2, m)
    hmid = []
    for r0, r1 in zip(bounds[:-1], bounds[1:]):
        if fill_xn is not None:
            fill_xn(r0, r1)
        xn = xn_ref[r0:r1, :]
        gate = _dot(xn, wg)
        up = _dot(xn, wu)
        hmid.append((gate / (1.0 + jnp.exp(-gate)) * up).astype(BF16))
    hmid = hmid[0] if len(hmid) == 1 else jnp.concatenate(hmid, axis=0)
    for c in range(0, d, col_chunk):
        cols = slice(c, c + col_chunk)
        part = _dot(hmid, wd_cols(c, c + col_chunk))
        base = o_ref if init_ref is None else init_ref
        o_ref[:, cols] = base[:, cols] + part[:tm]
        if oe_ref is not None:
            oe_ref[:, cols] += part[tm:]


def _ffn_head_kernel(*refs, tm, n_extra, row_chunk, col_chunk):
    if n_extra:
        (x_ref, e_ref, g_ref, wg_ref, wu_ref, wd_ref,
         o_ref, oe_ref, wgb_ref, wub_ref, wdb_ref, xn_ref) = refs
    else:
        x_ref, g_ref, wg_ref, wu_ref, wd_ref, o_ref, wgb_ref, wub_ref, wdb_ref, xn_ref = refs
        e_ref = oe_ref = None
    f = pl.program_id(0)

    @pl.when(f == 0)
    def _():
        gain = g_ref[...]
        for r in range(0, tm, row_chunk):
            xn_ref[r:r + row_chunk, :] = _rmsnorm_rows(x_ref[r:r + row_chunk, :], gain).astype(BF16)
            o_ref[r:r + row_chunk, :] = x_ref[r:r + row_chunk, :]
        if n_extra:
            xn_ref[tm:tm + n_extra, :] = _rmsnorm_rows(e_ref[...], gain).astype(BF16)
            oe_ref[...] = e_ref[...]

    wg = wg_ref[...].astype(BF16)
    wu = wu_ref[...].astype(BF16)
    wd = (0.5 * wd_ref[...]).astype(BF16)
    wgb_ref[...] = wg
    wub_ref[...] = wu
    wdb_ref[...] = wd
    _swiglu_accumulate(xn_ref, wg, wu, lambda c0, c1: wd[:, c0:c1], o_ref, oe_ref,
                       tm=tm, col_chunk=col_chunk)


def _ffn_tail_kernel(*refs, tm, side_rows, side_chunks, row_chunk, col_chunk):
    x_ref, g_ref, wg_ref, wu_ref, wd_ref = refs[:5]
    xn_ref = refs[-1]
    i, f, nf = pl.program_id(0), pl.program_id(1), pl.num_programs(1)
    if side_rows:
        side_ref, o_ref, sideb_ref = refs[5], refs[6], refs[7]
    else:
        o_ref = refs[5]

    def normalise(r0, r1):
        gain = g_ref[...]
        for r in range(r0, r1, row_chunk):
            xn_ref[r:r + row_chunk, :] = _rmsnorm_rows(x_ref[r:r + row_chunk, :], gain).astype(BF16)

    def step(first):
        if side_rows:
            chunk = jnp.minimum(i * nf + f, side_chunks - 1)
            row = (chunk * side_ref.shape[0]
                   + lax.broadcasted_iota(jnp.int32, (side_ref.shape[0], 1), 0))
            sideb_ref[...] = jnp.where(row < side_rows, side_ref[...], 0.0).T.astype(BF16)
        _swiglu_accumulate(xn_ref, wg_ref[...], wu_ref[...], lambda c0, c1: wd_ref[:, c0:c1],
                           o_ref, None, tm=tm, col_chunk=col_chunk,
                           init_ref=x_ref if first else None,
                           fill_xn=normalise if first else None)

    pl.when(f == 0)(functools.partial(step, True))
    pl.when(f > 0)(functools.partial(step, False))


def _ffn_head(x, extra, gain, wg, wu, wd, *, tm=1024, tf=256):
    n, d = x.shape
    dff = wg.shape[1]
    n_extra = 0 if extra is None else extra.shape[0]
    assert n % tm == 0 and dff % tf == 0
    in_specs = [pl.BlockSpec((tm, d), lambda f: (0, 0), pipeline_mode=pl.Buffered(1))]
    args = [x]
    if n_extra:
        in_specs.append(pl.BlockSpec((n_extra, d), lambda f: (0, 0)))
        args.append(extra)
    in_specs += [
        pl.BlockSpec((1, d), lambda f: (0, 0)),
        pl.BlockSpec((d, tf), lambda f: (0, f)),
        pl.BlockSpec((d, tf), lambda f: (0, f)),
        pl.BlockSpec((tf, d), lambda f: (f, 0)),
    ]
    args += [gain, wg, wu, wd]
    out_shape = [jax.ShapeDtypeStruct((tm, d), F32)]
    out_specs = [pl.BlockSpec((tm, d), lambda f: (0, 0))]
    if n_extra:
        out_shape.append(jax.ShapeDtypeStruct((n_extra, d), F32))
        out_specs.append(pl.BlockSpec((n_extra, d), lambda f: (0, 0)))
    out_shape += [jax.ShapeDtypeStruct(wg.shape, BF16), jax.ShapeDtypeStruct(wu.shape, BF16),
                  jax.ShapeDtypeStruct(wd.shape, BF16)]
    out_specs += [pl.BlockSpec((d, tf), lambda f: (0, f)),
                  pl.BlockSpec((d, tf), lambda f: (0, f)),
                  pl.BlockSpec((tf, d), lambda f: (f, 0))]
    head = pl.pallas_call(
        functools.partial(_ffn_head_kernel, tm=tm, n_extra=n_extra, row_chunk=256, col_chunk=512),
        grid=(dff // tf,),
        in_specs=in_specs,
        out_specs=out_specs,
        out_shape=out_shape,
        scratch_shapes=[pltpu.VMEM((tm + n_extra, d), BF16)],
        compiler_params=_cparams(1),
        name="ffn_head",
    )(*args)
    return head[0], (head[1] if n_extra else None), tuple(head[-3:])


def _ffn_tail(x, gain, wgb, wub, wdb, *, first_tile, in_place, side_t=None, tm=1024, tf=512,
              side_chunk=LANES):
    n, d = x.shape
    dff = wgb.shape[1]
    assert n % tm == 0 and dff % tf == 0
    n_tiles = n // tm - first_tile
    nf = dff // tf
    in_specs = [
        pl.BlockSpec((tm, d), lambda i, f: (i + first_tile, 0)),
        pl.BlockSpec((1, d), lambda i, f: (0, 0)),
        pl.BlockSpec((d, tf), lambda i, f: (0, f)),
        pl.BlockSpec((d, tf), lambda i, f: (0, f)),
        pl.BlockSpec((tf, d), lambda i, f: (f, 0)),
    ]
    args = [x, gain, wgb, wub, wdb]
    out_shape = [jax.ShapeDtypeStruct((n if in_place else n_tiles * tm, d), F32)]
    out_specs = [pl.BlockSpec((tm, d), lambda i, f: (i + (first_tile if in_place else 0), 0))]
    n_chunks = 0
    if side_t is not None:
        cols, rows = side_t.shape
        n_chunks = pl.cdiv(cols, side_chunk)
        assert n_chunks <= n_tiles * nf
        chunk_of = lambda i, f: jnp.minimum(i * nf + f, n_chunks - 1)
        in_specs.append(pl.BlockSpec((side_chunk, rows), lambda i, f: (chunk_of(i, f), 0)))
        args.append(side_t)
        out_shape.append(jax.ShapeDtypeStruct((rows, n_chunks * side_chunk), BF16))
        out_specs.append(pl.BlockSpec((rows, side_chunk), lambda i, f: (0, chunk_of(i, f))))
    tail = pl.pallas_call(
        functools.partial(_ffn_tail_kernel, tm=tm,
                          side_rows=0 if side_t is None else side_t.shape[0],
                          side_chunks=n_chunks, row_chunk=256, col_chunk=512),
        grid=(n_tiles, nf),
        in_specs=in_specs,
        out_specs=out_specs,
        out_shape=out_shape,
        scratch_shapes=[pltpu.VMEM((tm, d), BF16)],
        input_output_aliases={0: 0} if in_place else {},
        compiler_params=_cparams(2),
        name="ffn_tail",
    )(*args)
    return tail[0] if side_t is None else (tail[0], tail[1])


def _log_sigmoid(x):
    return jnp.minimum(x, 0.0) - jnp.log1p(jnp.exp(-jnp.abs(x)))


def _split3(x):
    hi = x.astype(BF16)
    r1 = x - hi.astype(F32)
    mid = r1.astype(BF16)
    lo = (r1 - mid.astype(F32)).astype(BF16)
    return hi, mid, lo


def _cumsum_rows(x, tri, head_lane):
    hi, mid, lo = (p.astype(F32) for p in _split3(x))
    packed = hi + pltpu.roll(mid, SUBLANES, axis=1) + pltpu.roll(lo, 2 * SUBLANES, axis=1)
    c = _dot(tri, packed.astype(BF16))
    c = c + pltpu.roll(c, LANES - SUBLANES, axis=1) + pltpu.roll(c, LANES - 2 * SUBLANES, axis=1)
    return jnp.where(head_lane, c, 0.0)


def _bias_columns(cum):
    hi, mid, lo = (p.astype(F32) for p in _split3(cum * BIAS_SCALE))
    roll = lambda x, g: pltpu.roll(x, g * SUBLANES, axis=1)
    cols = hi + roll(mid, 1) + roll(lo, 2) - roll(hi, 3) - roll(mid, 4) - roll(lo, 5)
    return cols.astype(BF16)


def _head_rmsnorm(z, gain, n_heads):
    outs = []
    for h in range(n_heads):
        outs.append(_rmsnorm_rows(z[:, h * HEAD_DIM:(h + 1) * HEAD_DIM], gain))
    return jnp.concatenate(outs, axis=-1)


def _window_means(e, w):
    assert w & (w - 1) == 0 and w <= N_META
    s, span = e, 1
    while span < w:
        s = s + pltpu.roll(s, span, axis=0)
        span *= 2
    return s[N_META:, :] / float(w)


def _inproj_kernel(ha_ref, hb_ref, hm_ref, gn_ref, w_ref, b_ref, qn_ref, kn_ref, pw_ref, ps_ref,
                   side_ref,
                   pool_ref, q_ref, k_ref, v_ref, aug_ref, km_ref, vm_ref, augm_ref, sideb_ref,
                   u_ref, pe_ref, pmeta_ref, ccol_ref, mcol_ref,
                   *, tm, tiles_a, tiles_per_batch, n_heads, d_pool, d_att, row_chunk):
    i = pl.program_id(0)
    gain = gn_ref[...]
    sideb_ref[...] = side_ref[...].astype(BF16)
    head_lane = lax.broadcasted_iota(jnp.int32, (1, LANES), 1) < n_heads
    w_pool = lambda: w_ref[:, 0:d_pool]
    w_q = lambda: w_ref[:, d_pool:d_pool + d_att]
    w_k = lambda: w_ref[:, d_pool + d_att:d_pool + 2 * d_att]
    w_vf = lambda: w_ref[:, d_pool + 2 * d_att:]

    def log_forget(zvf):
        lf = _log_sigmoid(zvf[:, d_att:] + b_ref[...])
        return jnp.where(head_lane, lf, 0.0)

    def tri(t):
        r_i = lax.broadcasted_iota(jnp.int32, (t, t), 0)
        c_i = lax.broadcasted_iota(jnp.int32, (t, t), 1)
        return (c_i <= r_i).astype(BF16)

    @pl.when(i == 0)
    def _():
        u_ref[0:META_PAD, :] = jnp.zeros((META_PAD, u_ref.shape[1]), BF16)
        u_ref[0:N_META, :] = _rmsnorm_rows(hm_ref[...], gain).astype(BF16)
        um = u_ref[0:META_PAD, :]
        row_valid = lax.broadcasted_iota(jnp.int32, (META_PAD, 1), 0) < N_META
        pmeta_ref[...] = _dot(um, w_pool())[0:N_META, :]
        km = _head_rmsnorm(_dot(um, w_k()), kn_ref[...], n_heads)
        km_ref[...] = jnp.where(row_valid, km, 0.0).astype(BF16)
        zvf = _dot(um, w_vf())
        vm_ref[...] = zvf[:, 0:d_att].astype(BF16)
        cum_m = _cumsum_rows(jnp.where(row_valid, log_forget(zvf), 0.0), tri(META_PAD), head_lane)
        augm_ref[...] = _bias_columns(cum_m)
        mcol_ref[...] = cum_m[META_PAD - 1:META_PAD, :]

    @pl.when(i % tiles_per_batch == 0)
    def _():
        ccol_ref[...] = mcol_ref[...]
        pe_ref[0:N_META, :] = pmeta_ref[...]

    zq, zk, zvf = [], [], []
    for r0 in range(0, tm, tm // 2):
        for r in range(r0, r0 + tm // 2, row_chunk):
            h = jnp.where(i < tiles_a, ha_ref[r:r + row_chunk, :], hb_ref[r:r + row_chunk, :])
            u_ref[r:r + row_chunk, :] = _rmsnorm_rows(h, gain).astype(BF16)
        u = u_ref[r0:r0 + tm // 2, :]
        pe_ref[N_META + r0:N_META + r0 + tm // 2, :] = _dot(u, w_pool())
        zq.append(_dot(u, w_q()))
        zk.append(_dot(u, w_k()))
        zvf.append(_dot(u, w_vf()))
    zq, zk, zvf = (jnp.concatenate(z, axis=0) for z in (zq, zk, zvf))
    q_ref[...] = _head_rmsnorm(zq, qn_ref[...], n_heads).astype(BF16)
    k_ref[...] = _head_rmsnorm(zk, kn_ref[...], n_heads).astype(BF16)
    v_ref[...] = zvf[:, 0:d_att].astype(BF16)

    gw = d_pool // len(POOL_WINDOWS)
    for g, w in enumerate(POOL_WINDOWS):
        lanes = slice(g * gw, (g + 1) * gw)
        e = pe_ref[:, lanes]
        pooled = (_window_means(e, w) - e[N_META:, :]).astype(BF16)
        mixed = _dot(pooled, pw_ref[g]) * ps_ref[:, lanes]
        pool_ref[:, lanes] = mixed.astype(BF16)
    pe_ref[0:N_META, :] = pe_ref[tm:tm + N_META, :]

    cum_col = _cumsum_rows(log_forget(zvf), tri(tm), head_lane) + ccol_ref[...]
    aug_ref[...] = _bias_columns(cum_col)
    ccol_ref[...] = cum_col[tm - 1:tm, :]


def _const_spec(shape, index=None):
    nd = len(shape)
    index = (0,) * nd if index is None else index
    return pl.BlockSpec(shape, lambda i: index, pipeline_mode=pl.Buffered(1))


def _in_proj(h1, h1m, gn, w_in, b_gate, qn, kn, pw, ps, side, *, batch, d_pool, d_att, tm=512):
    h1a, h1b = h1
    d = h1a.shape[1]
    n = h1a.shape[0] + h1b.shape[0]
    assert h1a.shape[0] % tm == 0 and h1b.shape[0] % tm == 0
    tiles_a = h1a.shape[0] // tm
    seq = n // batch
    n_heads = d_att // HEAD_DIM
    assert seq % tm == 0 and tm >= META_PAD and d_pool == d_att and n_heads <= SUBLANES
    assert w_in.shape == (d, d_pool + 3 * d_att + LANES)
    tiles_per_batch = seq // tm
    side_rows = side.shape[0] // (n // tm)
    assert side_rows * (n // tm) == side.shape[0] and side_rows % SUBLANES == 0
    side_spec = pl.BlockSpec((side_rows, side.shape[1]), lambda i: (i, 0))
    rows = lambda w: pl.BlockSpec((tm, w), lambda i: (i, 0))
    in_specs = [
        pl.BlockSpec((tm, d), lambda i: (jnp.minimum(i, tiles_a - 1), 0)),
        pl.BlockSpec((tm, d), lambda i: (jnp.maximum(i - tiles_a, 0), 0)),
        _const_spec((N_META, d)),
        _const_spec((1, d)),
        _const_spec(w_in.shape),
        _const_spec((1, LANES)),
        _const_spec((1, HEAD_DIM)),
        _const_spec((1, HEAD_DIM)),
        _const_spec(pw.shape),
        _const_spec((1, d_pool)),
        side_spec,
    ]
    out_shape = [
        jax.ShapeDtypeStruct((n, d_pool), BF16),
        jax.ShapeDtypeStruct((n, d_att), BF16),
        jax.ShapeDtypeStruct((n, d_att), BF16),
        jax.ShapeDtypeStruct((n, d_att), BF16),
        jax.ShapeDtypeStruct((n, LANES), BF16),
        jax.ShapeDtypeStruct((META_PAD, d_att), BF16),
        jax.ShapeDtypeStruct((META_PAD, d_att), BF16),
        jax.ShapeDtypeStruct((META_PAD, LANES), BF16),
        jax.ShapeDtypeStruct(side.shape, BF16),
    ]
    out_specs = [
        rows(d_pool), rows(d_att), rows(d_att), rows(d_att), rows(LANES),
        pl.BlockSpec((META_PAD, d_att), lambda i: (0, 0)),
        pl.BlockSpec((META_PAD, d_att), lambda i: (0, 0)),
        pl.BlockSpec((META_PAD, LANES), lambda i: (0, 0)),
        side_spec,
    ]
    scratch = [
        pltpu.VMEM((tm, d), BF16),
        pltpu.VMEM((tm + N_META, d_pool), F32),
        pltpu.VMEM((N_META, d_pool), F32),
        pltpu.VMEM((1, LANES), F32),
        pltpu.VMEM((1, LANES), F32),
    ]
    kern = functools.partial(_inproj_kernel, tm=tm, tiles_a=tiles_a, tiles_per_batch=tiles_per_batch,
                             n_heads=n_heads, d_pool=d_pool, d_att=d_att, row_chunk=256)
    return pl.pallas_call(
        kern,
        grid=(n // tm,),
        in_specs=in_specs,
        out_specs=out_specs,
        out_shape=out_shape,
        scratch_shapes=scratch,
        compiler_params=_cparams(1),
        name="in_proj",
    )(h1a, h1b, h1m, gn, w_in, b_gate, qn, kn, pw, ps, side)


def _attn_kernel(q_ref, k_ref, v_ref, km_ref, vm_ref, aug_ref, augm_ref, wg_ref, wu_ref, wd_ref,
                 o_ref, wgb_ref, wub_ref, wdb_ref, qa_ref, ka_ref, va_ref, *, tq):
    h = pl.program_id(1)
    seq = q_ref.shape[0]
    lane = lax.broadcasted_iota(jnp.int32, (1, LANES), 1)
    group = lane >> 3
    mine = (lane & (SUBLANES - 1)) == h
    one_hot = lambda lo: jnp.where(mine & (group >= lo) & (group < lo + 3), 1.0, 0.0).astype(BF16)
    keep = lambda lo: jnp.where((group >= lo) & (group < lo + 3), 0.0, 1.0).astype(BF16)
    qa_ref[:, 0:HEAD_DIM] = q_ref[...]
    qa_ref[:, HEAD_DIM:] = aug_ref[...] * keep(3) + one_hot(3)
    ka_ref[:, 0:HEAD_DIM] = k_ref[...]
    ka_ref[:, HEAD_DIM:] = aug_ref[...] * keep(0) + one_hot(0)
    kma = jnp.concatenate([km_ref[...], augm_ref[...] * keep(0) + one_hot(0)], axis=1)
    va_ref[:, 0:HEAD_DIM] = v_ref[...]
    va_ref[:, HEAD_DIM:] = jnp.ones((seq, HEAD_DIM), BF16)
    vma = jnp.concatenate([vm_ref[...], jnp.ones((META_PAD, HEAD_DIM), BF16)], axis=1)

    row_i = lax.broadcasted_iota(jnp.int32, (tq, tq), 0)
    col_i = lax.broadcasted_iota(jnp.int32, (tq, tq), 1)
    causal = col_i <= row_i
    tiles = list(reversed(range(seq // tq)))
    scores = {}
    for qi in tiles:
        rows = slice(qi * tq, (qi + 1) * tq)
        q = qa_ref[rows, :]
        blocks = [slice(j * tq, (j + 1) * tq) for j in range(qi + 1)]
        raw_m = jnp.where(lane < N_META, _dot_nt(q, kma), NEG_INF)
        raws = [_dot_nt(q, ka_ref[cols, :]) for cols in blocks]
        raws[-1] = jnp.where(causal, raws[-1], NEG_INF)
        scores[qi] = (rows, blocks, raw_m, raws)
    probs = {}
    for qi in tiles:
        rows, blocks, raw_m, raws = scores[qi]
        m = jnp.maximum(jnp.max(functools.reduce(jnp.maximum, raws), axis=-1, keepdims=True),
                        jnp.max(raw_m, axis=-1, keepdims=True))
        p_m = jnp.exp2(EXP2_SCALE * (raw_m - m))
        ps = [jnp.exp2(EXP2_SCALE * (r - m)) for r in raws]
        probs[qi] = (rows, blocks, p_m, ps)
    wgb_ref[...] = wg_ref[...].astype(BF16)
    wub_ref[...] = wu_ref[...].astype(BF16)
    wdb_ref[...] = (0.5 * wd_ref[...]).astype(BF16)
    for qi in tiles:
        rows, blocks, p_m, ps = probs[qi]
        acc = _dot(p_m.astype(BF16), vma)
        for p, cols in zip(ps, blocks):
            acc = acc + _dot(p.astype(BF16), va_ref[cols, :])
        o_ref[rows, :] = (acc[:, 0:HEAD_DIM] / acc[:, HEAD_DIM:]).astype(BF16)


def _attention(q, k, v, km, vm, aug, augm, next_ffn, *, batch, tq=512):
    n, d_att = q.shape
    seq = n // batch
    n_heads = d_att // HEAD_DIM
    steps = batch * n_heads
    tok = pl.BlockSpec((seq, HEAD_DIM), lambda b, h: (b, h))
    meta = pl.BlockSpec((META_PAD, HEAD_DIM), lambda b, h: (0, h))

    def chunk(w):
        assert w.shape[0] % (steps * 2 * SUBLANES) == 0
        return pl.BlockSpec((w.shape[0] // steps, w.shape[1]), lambda b, h: (b * n_heads + h, 0))

    side_specs = [chunk(w) for w in next_ffn]
    return pl.pallas_call(
        functools.partial(_attn_kernel, tq=tq),
        grid=(batch, n_heads),
        in_specs=[
            tok, tok, tok, meta, meta,
            pl.BlockSpec((seq, LANES), lambda b, h: (b, 0)),
            pl.BlockSpec((META_PAD, LANES), lambda b, h: (0, 0)),
        ] + side_specs,
        out_specs=[tok] + side_specs,
        out_shape=[jax.ShapeDtypeStruct((n, d_att), BF16)]
        + [jax.ShapeDtypeStruct(w.shape, BF16) for w in next_ffn],
        scratch_shapes=[pltpu.VMEM((seq, 2 * HEAD_DIM), BF16),
                        pltpu.VMEM((seq, 2 * HEAD_DIM), BF16),
                        pltpu.VMEM((seq, 2 * HEAD_DIM), BF16)],
        compiler_params=_cparams(2),
        name="fox_attn",
    )(q, k, v, km, vm, aug, augm, *next_ffn)


def _outproj_kernel(ha_ref, hb_ref, pool_ref, att_ref, wop_ref, woa_ref, o_ref, *, tiles_a,
                    col_chunk):
    i = pl.program_id(0)
    d = o_ref.shape[1]
    for c in range(0, d, col_chunk):
        cols = slice(c, c + col_chunk)
        h = jnp.where(i < tiles_a, ha_ref[:, cols], hb_ref[:, cols])
        o_ref[:, cols] = (h + _dot(pool_ref[...], wop_ref[:, cols])
                          + _dot(att_ref[...], woa_ref[:, cols]))


def _out_proj(h1, pool, att, w_out, *, tm=512):
    h1a, h1b = h1
    d = h1a.shape[1]
    n = h1a.shape[0] + h1b.shape[0]
    assert h1a.shape[0] % tm == 0 and h1b.shape[0] % tm == 0
    tiles_a = h1a.shape[0] // tm
    d_pool, d_att = pool.shape[1], att.shape[1]
    assert d_pool == d_att
    rows = lambda w: pl.BlockSpec((tm, w), lambda i: (i, 0))
    return pl.pallas_call(
        functools.partial(_outproj_kernel, tiles_a=tiles_a, col_chunk=512),
        grid=(n // tm,),
        in_specs=[pl.BlockSpec((tm, d), lambda i: (jnp.minimum(i, tiles_a - 1), 0)),
                  pl.BlockSpec((tm, d), lambda i: (jnp.maximum(i - tiles_a, 0), 0)),
                  rows(d_pool), rows(d_att),
                  _const_spec((d_pool, d), (0, 0)), _const_spec((d_att, d), (1, 0))],
        out_specs=rows(d),
        out_shape=jax.ShapeDtypeStruct((n, d), F32),
        compiler_params=_cparams(1),
        name="out_proj",
    )(h1a, h1b, pool, att, w_out, w_out)


def kernel(x, meta_tokens, ffn1_norm, ffn1_w_gate, ffn1_w_up, ffn1_w_down, mix_norm, w_in,
           b_forget, q_norm, k_norm, pool_w, pool_scale, w_out, ffn2_norm, ffn2_w_gate,
           ffn2_w_up, ffn2_w_down):
    batch, seq, d = x.shape
    n_heads = b_forget.shape[1]
    d_att = n_heads * HEAD_DIM
    d_pool = pool_scale.shape[1]
    assert meta_tokens.shape[0] == N_META
    assert ffn1_norm.shape[0] == 1, "one layer: the meta rows past the mixers never reach the output"

    hx = x.reshape(batch * seq, d)
    hm = meta_tokens.astype(x.dtype)
    g1 = ffn1_norm[0][None]
    h1_head, hm, w1_b = _ffn_head(hx, hm, g1, ffn1_w_gate[0], ffn1_w_up[0], ffn1_w_down[0])
    h1_tail, w_in_b = _ffn_tail(hx, g1, *w1_b, first_tile=1, in_place=False, side_t=w_in[0].T)
    h1 = (h1_head, h1_tail)

    b_gate = jnp.pad(b_forget[0][None], ((0, 0), (0, LANES - n_heads)))
    pool, q, k, v, aug, km, vm, augm, w_out_b = _in_proj(
        h1, hm, mix_norm[0][None], w_in_b, b_gate, q_norm[0][None], k_norm[0][None],
        pool_w[0].astype(BF16), pool_scale[0][None], w_out[0],
        batch=batch, d_pool=d_pool, d_att=d_att)

    att, *w2_b = _attention(q, k, v, km, vm, aug, augm,
                            (ffn2_w_gate[0], ffn2_w_up[0], ffn2_w_down[0]), batch=batch)
    h2 = _out_proj(h1, pool, att, w_out_b)

    out = _ffn_tail(h2, ffn2_norm[0][None], *w2_b, first_tile=0, in_place=True)
    return out.reshape(batch, seq, d)
```

```python
import functools

import jax
import jax.numpy as jnp
from jax import lax
from jax.experimental import pallas as pl
from jax.experimental.pallas import tpu as pltpu

F32 = jnp.float32
BF16 = jnp.bfloat16

EPS = 1e-6
N_META = 16
POOL_WINDOWS = (2, 4, 8, 16)
HEAD_DIM = 128
LANES = 128
SUBLANES = 8
META_PAD = 128
NEG_INF = float("-inf")
BIAS_SCALE = HEAD_DIM ** 0.5
EXP2_SCALE = 1.4426950408889634 / BIAS_SCALE

VMEM_LIMIT = 60 * 1024 * 1024


def _cparams(n_axes):
    return pltpu.CompilerParams(
        dimension_semantics=("arbitrary",) * n_axes,
        vmem_limit_bytes=VMEM_LIMIT,
    )


def _rmsnorm_rows(x, gain):
    ms = jnp.mean(x * x, axis=-1, keepdims=True)
    return x * lax.rsqrt(ms + EPS) * gain


def _dot(a, b):
    return jnp.dot(a, b, preferred_element_type=F32)


def _dot_nt(a, b):
    return lax.dot_general(a, b, (((1,), (1,)), ((), ())), preferred_element_type=F32)


def _swiglu_accumulate(xn_ref, wg, wu, wd_cols, o_ref, oe_ref, *, tm, col_chunk, init_ref=None):
    d = o_ref.shape[1]
    xn = xn_ref[...]
    gate = _dot(xn, wg)
    up = _dot(xn, wu)
    hmid = (gate / (1.0 + jnp.exp(-gate)) * up).astype(BF16)
    for c in range(0, d, col_chunk):
        cols = slice(c, c + col_chunk)
        part = _dot(hmid, wd_cols(c, c + col_chunk))
        base = o_ref if init_ref is None else init_ref
        o_ref[:, cols] = base[:, cols] + part[:tm]
        if oe_ref is not None:
            oe_ref[:, cols] += part[tm:]


def _ffn_head_kernel(*refs, tm, n_extra, row_chunk, col_chunk):
    if n_extra:
        (x_ref, e_ref, g_ref, wg_ref, wu_ref, wd_ref,
         o_ref, oe_ref, wgb_ref, wub_ref, wdb_ref, xn_ref) = refs
    else:
        x_ref, g_ref, wg_ref, wu_ref, wd_ref, o_ref, wgb_ref, wub_ref, wdb_ref, xn_ref = refs
        e_ref = oe_ref = None
    f = pl.program_id(0)

    @pl.when(f == 0)
    def _():
        gain = g_ref[...]
        for r in range(0, tm, row_chunk):
            xn_ref[r:r + row_chunk, :] = _rmsnorm_rows(x_ref[r:r + row_chunk, :], gain).astype(BF16)
            o_ref[r:r + row_chunk, :] = x_ref[r:r + row_chunk, :]
        if n_extra:
            xn_ref[tm:tm + n_extra, :] = _rmsnorm_rows(e_ref[...], gain).astype(BF16)
            oe_ref[...] = e_ref[...]

    wg = wg_ref[...].astype(BF16)
    wu = wu_ref[...].astype(BF16)
    wd = (0.5 * wd_ref[...]).astype(BF16)
    wgb_ref[...] = wg
    wub_ref[...] = wu
    wdb_ref[...] = wd
    _swiglu_accumulate(xn_ref, wg, wu, lambda c0, c1: wd[:, c0:c1], o_ref, oe_ref,
                       tm=tm, col_chunk=col_chunk)


def _ffn_tail_kernel(*refs, tm, side_rows, side_chunks, row_chunk, col_chunk):
    x_ref, g_ref, wg_ref, wu_ref, wd_ref = refs[:5]
    xn_ref = refs[-1]
    i, f, nf = pl.program_id(0), pl.program_id(1), pl.num_programs(1)
    if side_rows:
        side_ref, o_ref, sideb_ref = refs[5], refs[6], refs[7]
    else:
        o_ref = refs[5]

    @pl.when(f == 0)
    def _():
        gain = g_ref[...]
        for r in range(0, tm, row_chunk):
            xn_ref[r:r + row_chunk, :] = _rmsnorm_rows(x_ref[r:r + row_chunk, :], gain).astype(BF16)

    def step(first):
        if side_rows:
            chunk = jnp.minimum(i * nf + f, side_chunks - 1)
            row = (chunk * side_ref.shape[0]
                   + lax.broadcasted_iota(jnp.int32, (side_ref.shape[0], 1), 0))
            sideb_ref[...] = jnp.where(row < side_rows, side_ref[...], 0.0).T.astype(BF16)
        _swiglu_accumulate(xn_ref, wg_ref[...], wu_ref[...], lambda c0, c1: wd_ref[:, c0:c1],
                           o_ref, None, tm=tm, col_chunk=col_chunk,
                           init_ref=x_ref if first else None)

    pl.when(f == 0)(functools.partial(step, True))
    pl.when(f > 0)(functools.partial(step, False))


def _ffn_head(x, extra, gain, wg, wu, wd, *, tm=1024, tf=256):
    n, d = x.shape
    dff = wg.shape[1]
    n_extra = 0 if extra is None else extra.shape[0]
    assert n % tm == 0 and dff % tf == 0
    in_specs = [pl.BlockSpec((tm, d), lambda f: (0, 0), pipeline_mode=pl.Buffered(1))]
    args = [x]
    if n_extra:
        in_specs.append(pl.BlockSpec((n_extra, d), lambda f: (0, 0)))
        args.append(extra)
    in_specs += [
        pl.BlockSpec((1, d), lambda f: (0, 0)),
        pl.BlockSpec((d, tf), lambda f: (0, f)),
        pl.BlockSpec((d, tf), lambda f: (0, f)),
        pl.BlockSpec((tf, d), lambda f: (f, 0)),
    ]
    args += [gain, wg, wu, wd]
    out_shape = [jax.ShapeDtypeStruct((tm, d), F32)]
    out_specs = [pl.BlockSpec((tm, d), lambda f: (0, 0))]
    if n_extra:
        out_shape.append(jax.ShapeDtypeStruct((n_extra, d), F32))
        out_specs.append(pl.BlockSpec((n_extra, d), lambda f: (0, 0)))
    out_shape += [jax.ShapeDtypeStruct(wg.shape, BF16), jax.ShapeDtypeStruct(wu.shape, BF16),
                  jax.ShapeDtypeStruct(wd.shape, BF16)]
    out_specs += [pl.BlockSpec((d, tf), lambda f: (0, f)),
                  pl.BlockSpec((d, tf), lambda f: (0, f)),
                  pl.BlockSpec((tf, d), lambda f: (f, 0))]
    head = pl.pallas_call(
        functools.partial(_ffn_head_kernel, tm=tm, n_extra=n_extra, row_chunk=256, col_chunk=512),
        grid=(dff // tf,),
        in_specs=in_specs,
        out_specs=out_specs,
        out_shape=out_shape,
        scratch_shapes=[pltpu.VMEM((tm + n_extra, d), BF16)],
        compiler_params=_cparams(1),
        name="ffn_head",
    )(*args)
    return head[0], (head[1] if n_extra else None), tuple(head[-3:])


def _ffn_tail(x, gain, wgb, wub, wdb, *, first_tile, in_place, side_t=None, tm=1024, tf=512,
              side_chunk=LANES):
    n, d = x.shape
    dff = wgb.shape[1]
    assert n % tm == 0 and dff % tf == 0
    n_tiles = n // tm - first_tile
    nf = dff // tf
    in_specs = [
        pl.BlockSpec((tm, d), lambda i, f: (i + first_tile, 0)),
        pl.BlockSpec((1, d), lambda i, f: (0, 0)),
        pl.BlockSpec((d, tf), lambda i, f: (0, f)),
        pl.BlockSpec((d, tf), lambda i, f: (0, f)),
        pl.BlockSpec((tf, d), lambda i, f: (f, 0)),
    ]
    args = [x, gain, wgb, wub, wdb]
    out_shape = [jax.ShapeDtypeStruct((n if in_place else n_tiles * tm, d), F32)]
    out_specs = [pl.BlockSpec((tm, d), lambda i, f: (i + (first_tile if in_place else 0), 0))]
    n_chunks = 0
    if side_t is not None:
        cols, rows = side_t.shape
        n_chunks = pl.cdiv(cols, side_chunk)
        assert n_chunks <= n_tiles * nf
        chunk_of = lambda i, f: jnp.minimum(i * nf + f, n_chunks - 1)
        in_specs.append(pl.BlockSpec((side_chunk, rows), lambda i, f: (chunk_of(i, f), 0)))
        args.append(side_t)
        out_shape.append(jax.ShapeDtypeStruct((rows, n_chunks * side_chunk), BF16))
        out_specs.append(pl.BlockSpec((rows, side_chunk), lambda i, f: (0, chunk_of(i, f))))
    tail = pl.pallas_call(
        functools.partial(_ffn_tail_kernel, tm=tm,
                          side_rows=0 if side_t is None else side_t.shape[0],
                          side_chunks=n_chunks, row_chunk=256, col_chunk=512),
        grid=(n_tiles, nf),
        in_specs=in_specs,
        out_specs=out_specs,
        out_shape=out_shape,
        scratch_shapes=[pltpu.VMEM((tm, d), BF16)],
        input_output_aliases={0: 0} if in_place else {},
        compiler_params=_cparams(2),
        name="ffn_tail",
    )(*args)
    return tail[0] if side_t is None else (tail[0], tail[1])


def _log_sigmoid(x):
    return jnp.minimum(x, 0.0) - jnp.log1p(jnp.exp(-jnp.abs(x)))


def _split3(x):
    hi = x.astype(BF16)
    r1 = x - hi.astype(F32)
    mid = r1.astype(BF16)
    lo = (r1 - mid.astype(F32)).astype(BF16)
    return hi, mid, lo


def _cumsum_rows(x, tri, head_lane):
    hi, mid, lo = (p.astype(F32) for p in _split3(x))
    packed = hi + pltpu.roll(mid, SUBLANES, axis=1) + pltpu.roll(lo, 2 * SUBLANES, axis=1)
    c = _dot(tri, packed.astype(BF16))
    c = c + pltpu.roll(c, LANES - SUBLANES, axis=1) + pltpu.roll(c, LANES - 2 * SUBLANES, axis=1)
    return jnp.where(head_lane, c, 0.0)


def _bias_columns(cum):
    hi, mid, lo = (p.astype(F32) for p in _split3(cum * BIAS_SCALE))
    roll = lambda x, g: pltpu.roll(x, g * SUBLANES, axis=1)
    cols = hi + roll(mid, 1) + roll(lo, 2) - roll(hi, 3) - roll(mid, 4) - roll(lo, 5)
    return cols.astype(BF16)


def _head_rmsnorm(z, gain, n_heads):
    outs = []
    for h in range(n_heads):
        outs.append(_rmsnorm_rows(z[:, h * HEAD_DIM:(h + 1) * HEAD_DIM], gain))
    return jnp.concatenate(outs, axis=-1)


def _window_means(e, w):
    assert w & (w - 1) == 0 and w <= N_META
    s, span = e, 1
    while span < w:
        s = s + pltpu.roll(s, span, axis=0)
        span *= 2
    return s[N_META:, :] / float(w)


def _inproj_kernel(ha_ref, hb_ref, hm_ref, gn_ref, w_ref, b_ref, qn_ref, kn_ref, pw_ref, ps_ref,
                   side_ref,
                   pool_ref, q_ref, k_ref, v_ref, aug_ref, km_ref, vm_ref, augm_ref, sideb_ref,
                   u_ref, pe_ref, pmeta_ref, ccol_ref, mcol_ref,
                   *, tm, tiles_a, tiles_per_batch, n_heads, d_pool, d_att, row_chunk):
    i = pl.program_id(0)
    gain = gn_ref[...]
    sideb_ref[...] = side_ref[...].astype(BF16)
    head_lane = lax.broadcasted_iota(jnp.int32, (1, LANES), 1) < n_heads
    w_pool = lambda: w_ref[:, 0:d_pool]
    w_q = lambda: w_ref[:, d_pool:d_pool + d_att]
    w_k = lambda: w_ref[:, d_pool + d_att:d_pool + 2 * d_att]
    w_vf = lambda: w_ref[:, d_pool + 2 * d_att:]

    def log_forget(zvf):
        lf = _log_sigmoid(zvf[:, d_att:] + b_ref[...])
        return jnp.where(head_lane, lf, 0.0)

    def tri(t):
        r_i = lax.broadcasted_iota(jnp.int32, (t, t), 0)
        c_i = lax.broadcasted_iota(jnp.int32, (t, t), 1)
        return (c_i <= r_i).astype(BF16)

    @pl.when(i == 0)
    def _():
        u_ref[0:META_PAD, :] = jnp.zeros((META_PAD, u_ref.shape[1]), BF16)
        u_ref[0:N_META, :] = _rmsnorm_rows(hm_ref[...], gain).astype(BF16)
        um = u_ref[0:META_PAD, :]
        row_valid = lax.broadcasted_iota(jnp.int32, (META_PAD, 1), 0) < N_META
        pmeta_ref[...] = _dot(um, w_pool())[0:N_META, :]
        km = _head_rmsnorm(_dot(um, w_k()), kn_ref[...], n_heads)
        km_ref[...] = jnp.where(row_valid, km, 0.0).astype(BF16)
        zvf = _dot(um, w_vf())
        vm_ref[...] = zvf[:, 0:d_att].astype(BF16)
        cum_m = _cumsum_rows(jnp.where(row_valid, log_forget(zvf), 0.0), tri(META_PAD), head_lane)
        augm_ref[...] = _bias_columns(cum_m)
        mcol_ref[...] = cum_m[META_PAD - 1:META_PAD, :]

    @pl.when(i % tiles_per_batch == 0)
    def _():
        ccol_ref[...] = mcol_ref[...]
        pe_ref[0:N_META, :] = pmeta_ref[...]

    for r in range(0, tm, row_chunk):
        h = jnp.where(i < tiles_a, ha_ref[r:r + row_chunk, :], hb_ref[r:r + row_chunk, :])
        u_ref[r:r + row_chunk, :] = _rmsnorm_rows(h, gain).astype(BF16)
    u = u_ref[0:tm, :]

    pe_ref[N_META:N_META + tm, :] = _dot(u, w_pool())
    q_ref[...] = _head_rmsnorm(_dot(u, w_q()), qn_ref[...], n_heads).astype(BF16)
    k_ref[...] = _head_rmsnorm(_dot(u, w_k()), kn_ref[...], n_heads).astype(BF16)
    zvf = _dot(u, w_vf())
    v_ref[...] = zvf[:, 0:d_att].astype(BF16)

    gw = d_pool // len(POOL_WINDOWS)
    for g, w in enumerate(POOL_WINDOWS):
        lanes = slice(g * gw, (g + 1) * gw)
        e = pe_ref[:, lanes]
        pooled = (_window_means(e, w) - e[N_META:, :]).astype(BF16)
        mixed = _dot(pooled, pw_ref[g]) * ps_ref[:, lanes]
        pool_ref[:, lanes] = mixed.astype(BF16)
    pe_ref[0:N_META, :] = pe_ref[tm:tm + N_META, :]

    cum_col = _cumsum_rows(log_forget(zvf), tri(tm), head_lane) + ccol_ref[...]
    aug_ref[...] = _bias_columns(cum_col)
    ccol_ref[...] = cum_col[tm - 1:tm, :]


def _const_spec(shape, index=None):
    nd = len(shape)
    index = (0,) * nd if index is None else index
    return pl.BlockSpec(shape, lambda i: index, pipeline_mode=pl.Buffered(1))


def _in_proj(h1, h1m, gn, w_in, b_gate, qn, kn, pw, ps, side, *, batch, d_pool, d_att, tm=512):
    h1a, h1b = h1
    d = h1a.shape[1]
    n = h1a.shape[0] + h1b.shape[0]
    assert h1a.shape[0] % tm == 0 and h1b.shape[0] % tm == 0
    tiles_a = h1a.shape[0] // tm
    seq = n // batch
    n_heads = d_att // HEAD_DIM
    assert seq % tm == 0 and tm >= META_PAD and d_pool == d_att and n_heads <= SUBLANES
    assert w_in.shape == (d, d_pool + 3 * d_att + LANES)
    tiles_per_batch = seq // tm
    side_rows = side.shape[0] // (n // tm)
    assert side_rows * (n // tm) == side.shape[0] and side_rows % SUBLANES == 0
    side_spec = pl.BlockSpec((side_rows, side.shape[1]), lambda i: (i, 0))
    rows = lambda w: pl.BlockSpec((tm, w), lambda i: (i, 0))
    in_specs = [
        pl.BlockSpec((tm, d), lambda i: (jnp.minimum(i, tiles_a - 1), 0)),
        pl.BlockSpec((tm, d), lambda i: (jnp.maximum(i - tiles_a, 0), 0)),
        _const_spec((N_META, d)),
        _const_spec((1, d)),
        _const_spec(w_in.shape),
        _const_spec((1, LANES)),
        _const_spec((1, HEAD_DIM)),
        _const_spec((1, HEAD_DIM)),
        _const_spec(pw.shape),
        _const_spec((1, d_pool)),
        side_spec,
    ]
    out_shape = [
        jax.ShapeDtypeStruct((n, d_pool), BF16),
        jax.ShapeDtypeStruct((n, d_att), BF16),
        jax.ShapeDtypeStruct((n, d_att), BF16),
        jax.ShapeDtypeStruct((n, d_att), BF16),
        jax.ShapeDtypeStruct((n, LANES), BF16),
        jax.ShapeDtypeStruct((META_PAD, d_att), BF16),
        jax.ShapeDtypeStruct((META_PAD, d_att), BF16),
        jax.ShapeDtypeStruct((META_PAD, LANES), BF16),
        jax.ShapeDtypeStruct(side.shape, BF16),
    ]
    out_specs = [
        rows(d_pool), rows(d_att), rows(d_att), rows(d_att), rows(LANES),
        pl.BlockSpec((META_PAD, d_att), lambda i: (0, 0)),
        pl.BlockSpec((META_PAD, d_att), lambda i: (0, 0)),
        pl.BlockSpec((META_PAD, LANES), lambda i: (0, 0)),
        side_spec,
    ]
    scratch = [
        pltpu.VMEM((tm, d), BF16),
        pltpu.VMEM((tm + N_META, d_pool), F32),
        pltpu.VMEM((N_META, d_pool), F32),
        pltpu.VMEM((1, LANES), F32),
        pltpu.VMEM((1, LANES), F32),
    ]
    kern = functools.partial(_inproj_kernel, tm=tm, tiles_a=tiles_a, tiles_per_batch=tiles_per_batch,
                             n_heads=n_heads, d_pool=d_pool, d_att=d_att, row_chunk=256)
    return pl.pallas_call(
        kern,
        grid=(n // tm,),
        in_specs=in_specs,
        out_specs=out_specs,
        out_shape=out_shape,
        scratch_shapes=scratch,
        compiler_params=_cparams(1),
        name="in_proj",
    )(h1a, h1b, h1m, gn, w_in, b_gate, qn, kn, pw, ps, side)


def _attn_kernel(q_ref, k_ref, v_ref, km_ref, vm_ref, aug_ref, augm_ref, wg_ref, wu_ref, wd_ref,
                 o_ref, wgb_ref, wub_ref, wdb_ref, qa_ref, ka_ref, va_ref, *, tq):
    h = pl.program_id(1)
    seq = q_ref.shape[0]
    lane = lax.broadcasted_iota(jnp.int32, (1, LANES), 1)
    group = lane >> 3
    mine = (lane & (SUBLANES - 1)) == h
    one_hot = lambda lo: jnp.where(mine & (group >= lo) & (group < lo + 3), 1.0, 0.0).astype(BF16)
    keep = lambda lo: jnp.where((group >= lo) & (group < lo + 3), 0.0, 1.0).astype(BF16)
    qa_ref[:, 0:HEAD_DIM] = q_ref[...]
    qa_ref[:, HEAD_DIM:] = aug_ref[...] * keep(3) + one_hot(3)
    ka_ref[:, 0:HEAD_DIM] = k_ref[...]
    ka_ref[:, HEAD_DIM:] = aug_ref[...] * keep(0) + one_hot(0)
    kma = jnp.concatenate([km_ref[...], augm_ref[...] * keep(0) + one_hot(0)], axis=1)
    va_ref[:, 0:HEAD_DIM] = v_ref[...]
    va_ref[:, HEAD_DIM:] = jnp.ones((seq, HEAD_DIM), BF16)
    vma = jnp.concatenate([vm_ref[...], jnp.ones((META_PAD, HEAD_DIM), BF16)], axis=1)

    hq = tq // 2
    rows_of = lambda w: lax.broadcasted_iota(jnp.int32, (hq, w), 0)
    cols_of = lambda w: lax.broadcasted_iota(jnp.int32, (hq, w), 1)
    causal_top = cols_of(hq) <= rows_of(hq)
    causal_bot = cols_of(tq) <= rows_of(tq) + hq
    tiles = list(reversed(range(seq // tq)))
    scores = {}
    for qi in tiles:
        rows = slice(qi * tq, (qi + 1) * tq)
        q = qa_ref[rows, :]
        blocks = [slice(j * tq, (j + 1) * tq) for j in range(qi)]
        raw_m = jnp.where(lane < N_META, _dot_nt(q, kma), NEG_INF)
        raws = [_dot_nt(q, ka_ref[cols, :]) for cols in blocks]
        raw_t = jnp.where(causal_top, _dot_nt(q[0:hq], ka_ref[qi * tq:qi * tq + hq, :]), NEG_INF)
        raw_b = jnp.where(causal_bot, _dot_nt(q[hq:], ka_ref[rows, :]), NEG_INF)
        scores[qi] = (rows, blocks, raw_m, raws, raw_t, raw_b)
    probs = {}
    for qi in tiles:
        rows, blocks, raw_m, raws, raw_t, raw_b = scores[qi]
        halves = []
        for r0, raw_d in ((0, raw_t), (hq, raw_b)):
            part = slice(r0, r0 + hq)
            full = [raw_m[part]] + [r[part] for r in raws]
            m = jnp.maximum(jnp.max(full[0], axis=-1, keepdims=True),
                            jnp.max(raw_d, axis=-1, keepdims=True))
            if raws:
                m = jnp.maximum(m, jnp.max(functools.reduce(jnp.maximum, full[1:]),
                                           axis=-1, keepdims=True))
            halves.append([jnp.exp2(EXP2_SCALE * (r - m)) for r in full + [raw_d]])
        top, bot = halves
        p_m = jnp.concatenate([top[0], bot[0]], axis=0)
        ps = [jnp.concatenate([t, b], axis=0) for t, b in zip(top[1:-1], bot[1:-1])]
        probs[qi] = (rows, blocks, p_m, ps, top[-1], bot[-1])
    wgb_ref[...] = wg_ref[...].astype(BF16)
    wub_ref[...] = wu_ref[...].astype(BF16)
    wdb_ref[...] = (0.5 * wd_ref[...]).astype(BF16)
    for qi in tiles:
        rows, blocks, p_m, ps, p_t, p_b = probs[qi]
        acc = _dot(p_m.astype(BF16), vma)
        for p, cols in zip(ps, blocks):
            acc = acc + _dot(p.astype(BF16), va_ref[cols, :])
        acc = acc + jnp.concatenate(
            [_dot(p_t.astype(BF16), va_ref[qi * tq:qi * tq + hq, :]),
             _dot(p_b.astype(BF16), va_ref[rows, :])], axis=0)
        o_ref[rows, :] = (acc[:, 0:HEAD_DIM] / acc[:, HEAD_DIM:]).astype(BF16)


def _attention(q, k, v, km, vm, aug, augm, next_ffn, *, batch, tq=512):
    n, d_att = q.shape
    seq = n // batch
    n_heads = d_att // HEAD_DIM
    steps = batch * n_heads
    tok = pl.BlockSpec((seq, HEAD_DIM), lambda b, h: (b, h))
    meta = pl.BlockSpec((META_PAD, HEAD_DIM), lambda b, h: (0, h))

    def chunk(w):
        assert w.shape[0] % (steps * 2 * SUBLANES) == 0
        return pl.BlockSpec((w.shape[0] // steps, w.shape[1]), lambda b, h: (b * n_heads + h, 0))

    side_specs = [chunk(w) for w in next_ffn]
    return pl.pallas_call(
        functools.partial(_attn_kernel, tq=tq),
        grid=(batch, n_heads),
        in_specs=[
            tok, tok, tok, meta, meta,
            pl.BlockSpec((seq, LANES), lambda b, h: (b, 0)),
            pl.BlockSpec((META_PAD, LANES), lambda b, h: (0, 0)),
        ] + side_specs,
        out_specs=[tok] + side_specs,
        out_shape=[jax.ShapeDtypeStruct((n, d_att), BF16)]
        + [jax.ShapeDtypeStruct(w.shape, BF16) for w in next_ffn],
        scratch_shapes=[pltpu.VMEM((seq, 2 * HEAD_DIM), BF16),
                        pltpu.VMEM((seq, 2 * HEAD_DIM), BF16),
                        pltpu.VMEM((seq, 2 * HEAD_DIM), BF16)],
        compiler_params=_cparams(2),
        name="fox_attn",
    )(q, k, v, km, vm, aug, augm, *next_ffn)


def _outproj_kernel(ha_ref, hb_ref, pool_ref, att_ref, wop_ref, woa_ref, o_ref, *, tiles_a,
                    col_chunk):
    i = pl.program_id(0)
    d = o_ref.shape[1]
    for c in range(0, d, col_chunk):
        cols = slice(c, c + col_chunk)
        h = jnp.where(i < tiles_a, ha_ref[:, cols], hb_ref[:, cols])
        o_ref[:, cols] = (h + _dot(pool_ref[...], wop_ref[:, cols])
                          + _dot(att_ref[...], woa_ref[:, cols]))


def _out_proj(h1, pool, att, w_out, *, tm=512):
    h1a, h1b = h1
    d = h1a.shape[1]
    n = h1a.shape[0] + h1b.shape[0]
    assert h1a.shape[0] % tm == 0 and h1b.shape[0] % tm == 0
    tiles_a = h1a.shape[0] // tm
    d_pool, d_att = pool.shape[1], att.shape[1]
    assert d_pool == d_att
    rows = lambda w: pl.BlockSpec((tm, w), lambda i: (i, 0))
    return pl.pallas_call(
        functools.partial(_outproj_kernel, tiles_a=tiles_a, col_chunk=512),
        grid=(n // tm,),
        in_specs=[pl.BlockSpec((tm, d), lambda i: (jnp.minimum(i, tiles_a - 1), 0)),
                  pl.BlockSpec((tm, d), lambda i: (jnp.maximum(i - tiles_a, 0), 0)),
                  rows(d_pool), rows(d_att),
                  _const_spec((d_pool, d), (0, 0)), _const_spec((d_att, d), (1, 0))],
        out_specs=rows(d),
        out_shape=jax.ShapeDtypeStruct((n, d), F32),
        compiler_params=_cparams(1),
        name="out_proj",
    )(h1a, h1b, pool, att, w_out, w_out)


def kernel(x, meta_tokens, ffn1_norm, ffn1_w_gate, ffn1_w_up, ffn1_w_down, mix_norm, w_in,
           b_forget, q_norm, k_norm, pool_w, pool_scale, w_out, ffn2_norm, ffn2_w_gate,
           ffn2_w_up, ffn2_w_down):
    batch, seq, d = x.shape
    n_heads = b_forget.shape[1]
    d_att = n_heads * HEAD_DIM
    d_pool = pool_scale.shape[1]
    assert meta_tokens.shape[0] == N_META
    assert ffn1_norm.shape[0] == 1, "one layer: the meta rows past the mixers never reach the output"

    hx = x.reshape(batch * seq, d)
    hm = meta_tokens.astype(x.dtype)
    g1 = ffn1_norm[0][None]
    h1_head, hm, w1_b = _ffn_head(hx, hm, g1, ffn1_w_gate[0], ffn1_w_up[0], ffn1_w_down[0])
    h1_tail, w_in_b = _ffn_tail(hx, g1, *w1_b, first_tile=1, in_place=False, side_t=w_in[0].T)
    h1 = (h1_head, h1_tail)

    b_gate = jnp.pad(b_forget[0][None], ((0, 0), (0, LANES - n_heads)))
    pool, q, k, v, aug, km, vm, augm, w_out_b = _in_proj(
        h1, hm, mix_norm[0][None], w_in_b, b_gate, q_norm[0][None], k_norm[0][None],
        pool_w[0].astype(BF16), pool_scale[0][None], w_out[0],
        batch=batch, d_pool=d_pool, d_att=d_att)

    att, *w2_b = _attention(q, k, v, km, vm, aug, augm,
                            (ffn2_w_gate[0], ffn2_w_up[0], ffn2_w_down[0]), batch=batch)
    h2 = _out_proj(h1, pool, att, w_out_b)

    out = _ffn_tail(h2, ffn2_norm[0][None], *w2_b, first_tile=0, in_place=True)
    return out.reshape(batch, seq, d)
```

```python
import functools

import jax
import jax.numpy as jnp
from jax import lax
from jax.experimental import pallas as pl
from jax.experimental.pallas import tpu as pltpu

F32 = jnp.float32
BF16 = jnp.bfloat16

EPS = 1e-6
N_META = 16
POOL_WINDOWS = (2, 4, 8, 16)
HEAD_DIM = 128
LANES = 128
SUBLANES = 8
META_PAD = 128
NEG_INF = float("-inf")
BIAS_SCALE = HEAD_DIM ** 0.5
EXP2_SCALE = 1.4426950408889634 / BIAS_SCALE

VMEM_LIMIT = 60 * 1024 * 1024
ROW_CHUNK = 256
COL_CHUNK = 512


def _cparams(n_axes):
    return pltpu.CompilerParams(
        dimension_semantics=("arbitrary",) * n_axes,
        vmem_limit_bytes=VMEM_LIMIT,
    )


def _rmsnorm_rows(x, gain):
    ms = jnp.mean(x * x, axis=-1, keepdims=True)
    return x * lax.rsqrt(ms + EPS) * gain


def _dot(a, b):
    return jnp.dot(a, b, preferred_element_type=F32)


def _dot_nt(a, b):
    return lax.dot_general(a, b, (((1,), (1,)), ((), ())), preferred_element_type=F32)


def _swiglu_accumulate(xn_ref, wg, wu, wd_cols, o_ref, oe_ref, *, tm, col_chunk, init_ref=None):
    d = o_ref.shape[1]
    xn = xn_ref[...]
    gate = _dot(xn, wg)
    up = _dot(xn, wu)
    hmid = (gate / (1.0 + jnp.exp(-gate)) * up).astype(BF16)
    for c in range(0, d, col_chunk):
        cols = slice(c, c + col_chunk)
        part = _dot(hmid, wd_cols(c, c + col_chunk))
        base = o_ref if init_ref is None else init_ref
        o_ref[:, cols] = base[:, cols] + part[:tm]
        if oe_ref is not None:
            oe_ref[:, cols] += part[tm:]


def _ffn_head_kernel(*refs, tm, n_extra, row_chunk, col_chunk):
    if n_extra:
        (x_ref, e_ref, g_ref, wg_ref, wu_ref, wd_ref,
         o_ref, oe_ref, wgb_ref, wub_ref, wdb_ref, xn_ref) = refs
    else:
        x_ref, g_ref, wg_ref, wu_ref, wd_ref, o_ref, wgb_ref, wub_ref, wdb_ref, xn_ref = refs
        e_ref = oe_ref = None
    f = pl.program_id(0)

    @pl.when(f == 0)
    def _():
        gain = g_ref[...]
        for r in range(0, tm, row_chunk):
            xn_ref[r:r + row_chunk, :] = _rmsnorm_rows(x_ref[r:r + row_chunk, :], gain).astype(BF16)
            o_ref[r:r + row_chunk, :] = x_ref[r:r + row_chunk, :]
        if n_extra:
            xn_ref[tm:tm + n_extra, :] = _rmsnorm_rows(e_ref[...], gain).astype(BF16)
            oe_ref[...] = e_ref[...]

    wg = wg_ref[...].astype(BF16)
    wu = wu_ref[...].astype(BF16)
    wd = (0.5 * wd_ref[...]).astype(BF16)
    wgb_ref[...] = wg
    wub_ref[...] = wu
    wdb_ref[...] = wd
    _swiglu_accumulate(xn_ref, wg, wu, lambda c0, c1: wd[:, c0:c1], o_ref, oe_ref,
                       tm=tm, col_chunk=col_chunk)


def _ffn_tail_kernel(*refs, tm, side_rows, side_chunks, row_chunk, col_chunk):
    x_ref, g_ref, wg_ref, wu_ref, wd_ref = refs[:5]
    xn_ref = refs[-1]
    i, f, nf = pl.program_id(0), pl.program_id(1), pl.num_programs(1)
    if side_rows:
        side_ref, o_ref, sideb_ref = refs[5], refs[6], refs[7]
    else:
        o_ref = refs[5]

    @pl.when(f == 0)
    def _():
        gain = g_ref[...]
        for r in range(0, tm, row_chunk):
            xn_ref[r:r + row_chunk, :] = _rmsnorm_rows(x_ref[r:r + row_chunk, :], gain).astype(BF16)

    def step(first):
        if side_rows:
            chunk = jnp.minimum(i * nf + f, side_chunks - 1)
            row = (chunk * side_ref.shape[0]
                   + lax.broadcasted_iota(jnp.int32, (side_ref.shape[0], 1), 0))
            sideb_ref[...] = jnp.where(row < side_rows, side_ref[...], 0.0).T.astype(BF16)
        _swiglu_accumulate(xn_ref, wg_ref[...], wu_ref[...], lambda c0, c1: wd_ref[:, c0:c1],
                           o_ref, None, tm=tm, col_chunk=col_chunk,
                           init_ref=x_ref if first else None)

    pl.when(f == 0)(functools.partial(step, True))
    pl.when(f > 0)(functools.partial(step, False))


def _ffn_head(x, extra, gain, wg, wu, wd, *, tm=1024, tf=128):
    n, d = x.shape
    dff = wg.shape[1]
    n_extra = 0 if extra is None else extra.shape[0]
    assert n % tm == 0 and dff % tf == 0
    in_specs = [pl.BlockSpec((tm, d), lambda f: (0, 0), pipeline_mode=pl.Buffered(1))]
    args = [x]
    if n_extra:
        in_specs.append(pl.BlockSpec((n_extra, d), lambda f: (0, 0)))
        args.append(extra)
    in_specs += [
        pl.BlockSpec((1, d), lambda f: (0, 0)),
        pl.BlockSpec((d, tf), lambda f: (0, f)),
        pl.BlockSpec((d, tf), lambda f: (0, f)),
        pl.BlockSpec((tf, d), lambda f: (f, 0)),
    ]
    args += [gain, wg, wu, wd]
    out_shape = [jax.ShapeDtypeStruct((tm, d), F32)]
    out_specs = [pl.BlockSpec((tm, d), lambda f: (0, 0))]
    if n_extra:
        out_shape.append(jax.ShapeDtypeStruct((n_extra, d), F32))
        out_specs.append(pl.BlockSpec((n_extra, d), lambda f: (0, 0)))
    out_shape += [jax.ShapeDtypeStruct(wg.shape, BF16), jax.ShapeDtypeStruct(wu.shape, BF16),
                  jax.ShapeDtypeStruct(wd.shape, BF16)]
    out_specs += [pl.BlockSpec((d, tf), lambda f: (0, f)),
                  pl.BlockSpec((d, tf), lambda f: (0, f)),
                  pl.BlockSpec((tf, d), lambda f: (f, 0))]
    head = pl.pallas_call(
        functools.partial(_ffn_head_kernel, tm=tm, n_extra=n_extra, row_chunk=ROW_CHUNK,
                          col_chunk=COL_CHUNK),
        grid=(dff // tf,),
        in_specs=in_specs,
        out_specs=out_specs,
        out_shape=out_shape,
        scratch_shapes=[pltpu.VMEM((tm + n_extra, d), BF16)],
        compiler_params=_cparams(1),
        name="ffn_head",
    )(*args)
    return head[0], (head[1] if n_extra else None), tuple(head[-3:])


def _ffn_tail(x, gain, wgb, wub, wdb, *, first_tile, in_place, side_t=None, tm=1024, tf=512,
              side_chunk=LANES):
    n, d = x.shape
    dff = wgb.shape[1]
    assert n % tm == 0 and dff % tf == 0
    n_tiles = n // tm - first_tile
    nf = dff // tf
    in_specs = [
        pl.BlockSpec((tm, d), lambda i, f: (i + first_tile, 0)),
        pl.BlockSpec((1, d), lambda i, f: (0, 0)),
        pl.BlockSpec((d, tf), lambda i, f: (0, f)),
        pl.BlockSpec((d, tf), lambda i, f: (0, f)),
        pl.BlockSpec((tf, d), lambda i, f: (f, 0)),
    ]
    args = [x, gain, wgb, wub, wdb]
    out_shape = [jax.ShapeDtypeStruct((n if in_place else n_tiles * tm, d), F32)]
    out_specs = [pl.BlockSpec((tm, d), lambda i, f: (i + (first_tile if in_place else 0), 0))]
    n_chunks = 0
    if side_t is not None:
        cols, rows = side_t.shape
        n_chunks = pl.cdiv(cols, side_chunk)
        assert n_chunks <= n_tiles * nf
        chunk_of = lambda i, f: jnp.minimum(i * nf + f, n_chunks - 1)
        in_specs.append(pl.BlockSpec((side_chunk, rows), lambda i, f: (chunk_of(i, f), 0)))
        args.append(side_t)
        out_shape.append(jax.ShapeDtypeStruct((rows, n_chunks * side_chunk), BF16))
        out_specs.append(pl.BlockSpec((rows, side_chunk), lambda i, f: (0, chunk_of(i, f))))
    tail = pl.pallas_call(
        functools.partial(_ffn_tail_kernel, tm=tm,
                          side_rows=0 if side_t is None else side_t.shape[0],
                          side_chunks=n_chunks, row_chunk=ROW_CHUNK, col_chunk=COL_CHUNK),
        grid=(n_tiles, nf),
        in_specs=in_specs,
        out_specs=out_specs,
        out_shape=out_shape,
        scratch_shapes=[pltpu.VMEM((tm, d), BF16)],
        input_output_aliases={0: 0} if in_place else {},
        compiler_params=_cparams(2),
        name="ffn_tail",
    )(*args)
    return tail[0] if side_t is None else (tail[0], tail[1])


def _log_sigmoid(x):
    return jnp.minimum(x, 0.0) - jnp.log1p(jnp.exp(-jnp.abs(x)))


def _split3(x):
    hi = x.astype(BF16)
    r1 = x - hi.astype(F32)
    mid = r1.astype(BF16)
    lo = (r1 - mid.astype(F32)).astype(BF16)
    return hi, mid, lo


def _cumsum_rows(x, tri, head_lane):
    hi, mid, lo = (p.astype(F32) for p in _split3(x))
    packed = hi + pltpu.roll(mid, SUBLANES, axis=1) + pltpu.roll(lo, 2 * SUBLANES, axis=1)
    c = _dot(tri, packed.astype(BF16))
    c = c + pltpu.roll(c, LANES - SUBLANES, axis=1) + pltpu.roll(c, LANES - 2 * SUBLANES, axis=1)
    return jnp.where(head_lane, c, 0.0)


def _bias_columns(cum):
    hi, mid, lo = (p.astype(F32) for p in _split3(cum * BIAS_SCALE))
    roll = lambda x, g: pltpu.roll(x, g * SUBLANES, axis=1)
    cols = hi + roll(mid, 1) + roll(lo, 2) - roll(hi, 3) - roll(mid, 4) - roll(lo, 5)
    return cols.astype(BF16)


def _head_rmsnorm(z, gain, n_heads):
    outs = []
    for h in range(n_heads):
        outs.append(_rmsnorm_rows(z[:, h * HEAD_DIM:(h + 1) * HEAD_DIM], gain))
    return jnp.concatenate(outs, axis=-1)


def _window_means(e, w):
    assert w & (w - 1) == 0 and w <= N_META
    s, span = e, 1
    while span < w:
        s = s + pltpu.roll(s, span, axis=0)
        span *= 2
    return s[N_META:, :] / float(w)


def _inproj_kernel(ha_ref, hb_ref, hm_ref, gn_ref, w_ref, b_ref, qn_ref, kn_ref, pw_ref, ps_ref,
                   side_ref,
                   pool_ref, q_ref, k_ref, v_ref, aug_ref, km_ref, vm_ref, augm_ref, sideb_ref,
                   u_ref, pe_ref, pmeta_ref, ccol_ref, mcol_ref,
                   *, tm, tiles_a, tiles_per_batch, n_heads, d_pool, d_att, row_chunk):
    i = pl.program_id(0)
    gain = gn_ref[...]
    sideb_ref[...] = side_ref[...].astype(BF16)
    head_lane = lax.broadcasted_iota(jnp.int32, (1, LANES), 1) < n_heads
    w_pool = lambda: w_ref[:, 0:d_pool]
    w_q = lambda: w_ref[:, d_pool:d_pool + d_att]
    w_k = lambda: w_ref[:, d_pool + d_att:d_pool + 2 * d_att]
    w_vf = lambda: w_ref[:, d_pool + 2 * d_att:]

    def log_forget(zvf):
        lf = _log_sigmoid(zvf[:, d_att:] + b_ref[...])
        return jnp.where(head_lane, lf, 0.0)

    def tri(t):
        r_i = lax.broadcasted_iota(jnp.int32, (t, t), 0)
        c_i = lax.broadcasted_iota(jnp.int32, (t, t), 1)
        return (c_i <= r_i).astype(BF16)

    @pl.when(i == 0)
    def _():
        u_ref[0:META_PAD, :] = jnp.zeros((META_PAD, u_ref.shape[1]), BF16)
        u_ref[0:N_META, :] = _rmsnorm_rows(hm_ref[...], gain).astype(BF16)
        um = u_ref[0:META_PAD, :]
        row_valid = lax.broadcasted_iota(jnp.int32, (META_PAD, 1), 0) < N_META
        pmeta_ref[...] = _dot(um, w_pool())[0:N_META, :]
        km = _head_rmsnorm(_dot(um, w_k()), kn_ref[...], n_heads)
        km_ref[...] = jnp.where(row_valid, km, 0.0).astype(BF16)
        zvf = _dot(um, w_vf())
        vm_ref[...] = zvf[:, 0:d_att].astype(BF16)
        cum_m = _cumsum_rows(jnp.where(row_valid, log_forget(zvf), 0.0), tri(META_PAD), head_lane)
        augm_ref[...] = _bias_columns(cum_m)
        mcol_ref[...] = cum_m[META_PAD - 1:META_PAD, :]

    @pl.when(i % tiles_per_batch == 0)
    def _():
        ccol_ref[...] = mcol_ref[...]
        pe_ref[0:N_META, :] = pmeta_ref[...]

    for r in range(0, tm, row_chunk):
        h = jnp.where(i < tiles_a, ha_ref[r:r + row_chunk, :], hb_ref[r:r + row_chunk, :])
        u_ref[r:r + row_chunk, :] = _rmsnorm_rows(h, gain).astype(BF16)
    u = u_ref[0:tm, :]

    pe_ref[N_META:N_META + tm, :] = _dot(u, w_pool())
    q_ref[...] = _head_rmsnorm(_dot(u, w_q()), qn_ref[...], n_heads).astype(BF16)
    k_ref[...] = _head_rmsnorm(_dot(u, w_k()), kn_ref[...], n_heads).astype(BF16)
    zvf = _dot(u, w_vf())
    v_ref[...] = zvf[:, 0:d_att].astype(BF16)

    gw = d_pool // len(POOL_WINDOWS)
    for g, w in enumerate(POOL_WINDOWS):
        lanes = slice(g * gw, (g + 1) * gw)
        e = pe_ref[:, lanes]
        pooled = (_window_means(e, w) - e[N_META:, :]).astype(BF16)
        mixed = _dot(pooled, pw_ref[g]) * ps_ref[:, lanes]
        pool_ref[:, lanes] = mixed.astype(BF16)
    pe_ref[0:N_META, :] = pe_ref[tm:tm + N_META, :]

    cum_col = _cumsum_rows(log_forget(zvf), tri(tm), head_lane) + ccol_ref[...]
    aug_ref[...] = _bias_columns(cum_col)
    ccol_ref[...] = cum_col[tm - 1:tm, :]


def _const_spec(shape, index=None):
    nd = len(shape)
    index = (0,) * nd if index is None else index
    return pl.BlockSpec(shape, lambda i: index, pipeline_mode=pl.Buffered(1))


def _in_proj(h1, h1m, gn, w_in, b_gate, qn, kn, pw, ps, side, *, batch, d_pool, d_att, tm=512):
    h1a, h1b = h1
    d = h1a.shape[1]
    n = h1a.shape[0] + h1b.shape[0]
    assert h1a.shape[0] % tm == 0 and h1b.shape[0] % tm == 0
    tiles_a = h1a.shape[0] // tm
    seq = n // batch
    n_heads = d_att // HEAD_DIM
    assert seq % tm == 0 and tm >= META_PAD and d_pool == d_att and n_heads <= SUBLANES
    assert w_in.shape == (d, d_pool + 3 * d_att + LANES)
    tiles_per_batch = seq // tm
    side_rows = side.shape[0] // (n // tm)
    assert side_rows * (n // tm) == side.shape[0] and side_rows % SUBLANES == 0
    side_spec = pl.BlockSpec((side_rows, side.shape[1]), lambda i: (i, 0))
    rows = lambda w: pl.BlockSpec((tm, w), lambda i: (i, 0))
    in_specs = [
        pl.BlockSpec((tm, d), lambda i: (jnp.minimum(i, tiles_a - 1), 0)),
        pl.BlockSpec((tm, d), lambda i: (jnp.maximum(i - tiles_a, 0), 0)),
        _const_spec((N_META, d)),
        _const_spec((1, d)),
        _const_spec(w_in.shape),
        _const_spec((1, LANES)),
        _const_spec((1, HEAD_DIM)),
        _const_spec((1, HEAD_DIM)),
        _const_spec(pw.shape),
        _const_spec((1, d_pool)),
        side_spec,
    ]
    out_shape = [
        jax.ShapeDtypeStruct((n, d_pool), BF16),
        jax.ShapeDtypeStruct((n, d_att), BF16),
        jax.ShapeDtypeStruct((n, d_att), BF16),
        jax.ShapeDtypeStruct((n, d_att), BF16),
        jax.ShapeDtypeStruct((n, LANES), BF16),
        jax.ShapeDtypeStruct((META_PAD, d_att), BF16),
        jax.ShapeDtypeStruct((META_PAD, d_att), BF16),
        jax.ShapeDtypeStruct((META_PAD, LANES), BF16),
        jax.ShapeDtypeStruct(side.shape, BF16),
    ]
    out_specs = [
        rows(d_pool), rows(d_att), rows(d_att), rows(d_att), rows(LANES),
        pl.BlockSpec((META_PAD, d_att), lambda i: (0, 0)),
        pl.BlockSpec((META_PAD, d_att), lambda i: (0, 0)),
        pl.BlockSpec((META_PAD, LANES), lambda i: (0, 0)),
        side_spec,
    ]
    scratch = [
        pltpu.VMEM((tm, d), BF16),
        pltpu.VMEM((tm + N_META, d_pool), F32),
        pltpu.VMEM((N_META, d_pool), F32),
        pltpu.VMEM((1, LANES), F32),
        pltpu.VMEM((1, LANES), F32),
    ]
    kern = functools.partial(_inproj_kernel, tm=tm, tiles_a=tiles_a, tiles_per_batch=tiles_per_batch,
                             n_heads=n_heads, d_pool=d_pool, d_att=d_att, row_chunk=ROW_CHUNK)
    return pl.pallas_call(
        kern,
        grid=(n // tm,),
        in_specs=in_specs,
        out_specs=out_specs,
        out_shape=out_shape,
        scratch_shapes=scratch,
        compiler_params=_cparams(1),
        name="in_proj",
    )(h1a, h1b, h1m, gn, w_in, b_gate, qn, kn, pw, ps, side)


def _attn_kernel(q_ref, k_ref, v_ref, km_ref, vm_ref, aug_ref, augm_ref, wg_ref, wu_ref, wd_ref,
                 o_ref, wgb_ref, wub_ref, wdb_ref, qa_ref, ka_ref, va_ref, *, tq):
    h = pl.program_id(1)
    seq = q_ref.shape[0]
    lane = lax.broadcasted_iota(jnp.int32, (1, LANES), 1)
    group = lane >> (SUBLANES.bit_length() - 1)
    mine = (lane & (SUBLANES - 1)) == h
    one_hot = lambda lo: jnp.where(mine & (group >= lo) & (group < lo + 3), 1.0, 0.0).astype(BF16)
    keep = lambda lo: jnp.where((group >= lo) & (group < lo + 3), 0.0, 1.0).astype(BF16)
    qa_ref[:, 0:HEAD_DIM] = q_ref[...]
    qa_ref[:, HEAD_DIM:] = aug_ref[...] * keep(3) + one_hot(3)
    ka_ref[:, 0:HEAD_DIM] = k_ref[...]
    ka_ref[:, HEAD_DIM:] = aug_ref[...] * keep(0) + one_hot(0)
    kma = jnp.concatenate([km_ref[...], augm_ref[...] * keep(0) + one_hot(0)], axis=1)
    va_ref[:, 0:HEAD_DIM] = v_ref[...]
    va_ref[:, HEAD_DIM:] = jnp.ones((seq, HEAD_DIM), BF16)
    vma = jnp.concatenate([vm_ref[...], jnp.ones((META_PAD, HEAD_DIM), BF16)], axis=1)

    hq = tq // 2
    rows_of = lambda w: lax.broadcasted_iota(jnp.int32, (hq, w), 0)
    cols_of = lambda w: lax.broadcasted_iota(jnp.int32, (hq, w), 1)
    causal_top = cols_of(hq) <= rows_of(hq)
    causal_bot = cols_of(tq) <= rows_of(tq) + hq
    tiles = list(reversed(range(seq // tq)))
    scores = {}
    for qi in tiles:
        rows = slice(qi * tq, (qi + 1) * tq)
        q = qa_ref[rows, :]
        blocks = [slice(j * tq, (j + 1) * tq) for j in range(qi)]
        raw_m = jnp.where(lane < N_META, _dot_nt(q, kma), NEG_INF)
        raws = [_dot_nt(q, ka_ref[cols, :]) for cols in blocks]
        raw_t = jnp.where(causal_top, _dot_nt(q[0:hq], ka_ref[qi * tq:qi * tq + hq, :]), NEG_INF)
        raw_b = jnp.where(causal_bot, _dot_nt(q[hq:], ka_ref[rows, :]), NEG_INF)
        scores[qi] = (rows, blocks, raw_m, raws, raw_t, raw_b)
    probs = {}
    for qi in tiles:
        rows, blocks, raw_m, raws, raw_t, raw_b = scores[qi]
        halves = []
        for r0, raw_d in ((0, raw_t), (hq, raw_b)):
            part = slice(r0, r0 + hq)
            full = [raw_m[part]] + [r[part] for r in raws]
            m = jnp.maximum(jnp.max(full[0], axis=-1, keepdims=True),
                            jnp.max(raw_d, axis=-1, keepdims=True))
            if raws:
                m = jnp.maximum(m, jnp.max(functools.reduce(jnp.maximum, full[1:]),
                                           axis=-1, keepdims=True))
            halves.append([jnp.exp2(EXP2_SCALE * (r - m)) for r in full + [raw_d]])
        top, bot = halves
        p_m = jnp.concatenate([top[0], bot[0]], axis=0)
        ps = [jnp.concatenate([t, b], axis=0) for t, b in zip(top[1:-1], bot[1:-1])]
        probs[qi] = (rows, blocks, p_m, ps, top[-1], bot[-1])
    wgb_ref[...] = wg_ref[...].astype(BF16)
    wub_ref[...] = wu_ref[...].astype(BF16)
    wdb_ref[...] = (0.5 * wd_ref[...]).astype(BF16)
    for qi in tiles:
        rows, blocks, p_m, ps, p_t, p_b = probs[qi]
        acc = _dot(p_m.astype(BF16), vma)
        for p, cols in zip(ps, blocks):
            acc = acc + _dot(p.astype(BF16), va_ref[cols, :])
        acc = acc + jnp.concatenate(
            [_dot(p_t.astype(BF16), va_ref[qi * tq:qi * tq + hq, :]),
             _dot(p_b.astype(BF16), va_ref[rows, :])], axis=0)
        o_ref[rows, :] = (acc[:, 0:HEAD_DIM] / acc[:, HEAD_DIM:]).astype(BF16)


def _attention(q, k, v, km, vm, aug, augm, next_ffn, *, batch, tq=512):
    n, d_att = q.shape
    seq = n // batch
    n_heads = d_att // HEAD_DIM
    steps = batch * n_heads
    tok = pl.BlockSpec((seq, HEAD_DIM), lambda b, h: (b, h))
    meta = pl.BlockSpec((META_PAD, HEAD_DIM), lambda b, h: (0, h))

    def chunk(w):
        assert w.shape[0] % (steps * 2 * SUBLANES) == 0
        return pl.BlockSpec((w.shape[0] // steps, w.shape[1]), lambda b, h: (b * n_heads + h, 0))

    side_specs = [chunk(w) for w in next_ffn]
    return pl.pallas_call(
        functools.partial(_attn_kernel, tq=tq),
        grid=(batch, n_heads),
        in_specs=[
            tok, tok, tok, meta, meta,
            pl.BlockSpec((seq, LANES), lambda b, h: (b, 0)),
            pl.BlockSpec((META_PAD, LANES), lambda b, h: (0, 0)),
        ] + side_specs,
        out_specs=[tok] + side_specs,
        out_shape=[jax.ShapeDtypeStruct((n, d_att), BF16)]
        + [jax.ShapeDtypeStruct(w.shape, BF16) for w in next_ffn],
        scratch_shapes=[pltpu.VMEM((seq, 2 * HEAD_DIM), BF16),
                        pltpu.VMEM((seq, 2 * HEAD_DIM), BF16),
                        pltpu.VMEM((seq, 2 * HEAD_DIM), BF16)],
        compiler_params=_cparams(2),
        name="fox_attn",
    )(q, k, v, km, vm, aug, augm, *next_ffn)


def _outproj_kernel(ha_ref, hb_ref, pool_ref, att_ref, wop_ref, woa_ref, o_ref, *, tiles_a,
                    col_chunk):
    i = pl.program_id(0)
    d = o_ref.shape[1]
    for c in range(0, d, col_chunk):
        cols = slice(c, c + col_chunk)
        h = jnp.where(i < tiles_a, ha_ref[:, cols], hb_ref[:, cols])
        o_ref[:, cols] = (h + _dot(pool_ref[...], wop_ref[:, cols])
                          + _dot(att_ref[...], woa_ref[:, cols]))


def _out_proj(h1, pool, att, w_out, *, tm=512):
    h1a, h1b = h1
    d = h1a.shape[1]
    n = h1a.shape[0] + h1b.shape[0]
    assert h1a.shape[0] % tm == 0 and h1b.shape[0] % tm == 0
    tiles_a = h1a.shape[0] // tm
    d_pool, d_att = pool.shape[1], att.shape[1]
    assert d_pool == d_att
    rows = lambda w: pl.BlockSpec((tm, w), lambda i: (i, 0))
    return pl.pallas_call(
        functools.partial(_outproj_kernel, tiles_a=tiles_a, col_chunk=COL_CHUNK),
        grid=(n // tm,),
        in_specs=[pl.BlockSpec((tm, d), lambda i: (jnp.minimum(i, tiles_a - 1), 0)),
                  pl.BlockSpec((tm, d), lambda i: (jnp.maximum(i - tiles_a, 0), 0)),
                  rows(d_pool), rows(d_att),
                  _const_spec((d_pool, d), (0, 0)), _const_spec((d_att, d), (1, 0))],
        out_specs=rows(d),
        out_shape=jax.ShapeDtypeStruct((n, d), F32),
        compiler_params=_cparams(1),
        name="out_proj",
    )(h1a, h1b, pool, att, w_out, w_out)


def kernel(x, meta_tokens, ffn1_norm, ffn1_w_gate, ffn1_w_up, ffn1_w_down, mix_norm, w_in,
           b_forget, q_norm, k_norm, pool_w, pool_scale, w_out, ffn2_norm, ffn2_w_gate,
           ffn2_w_up, ffn2_w_down):
    batch, seq, d = x.shape
    n_heads = b_forget.shape[1]
    d_att = n_heads * HEAD_DIM
    d_pool = pool_scale.shape[1]
    assert meta_tokens.shape[0] == N_META
    assert ffn1_norm.shape[0] == 1, "one layer: the meta rows past the mixers never reach the output"

    hx = x.reshape(batch * seq, d)
    hm = meta_tokens.astype(x.dtype)
    g1 = ffn1_norm[0][None]
    h1_head, hm, w1_b = _ffn_head(hx, hm, g1, ffn1_w_gate[0], ffn1_w_up[0], ffn1_w_down[0])
    h1_tail, w_in_b = _ffn_tail(hx, g1, *w1_b, first_tile=1, in_place=False, side_t=w_in[0].T)
    h1 = (h1_head, h1_tail)

    b_gate = jnp.pad(b_forget[0][None], ((0, 0), (0, LANES - n_heads)))
    pool, q, k, v, aug, km, vm, augm, w_out_b = _in_proj(
        h1, hm, mix_norm[0][None], w_in_b, b_gate, q_norm[0][None], k_norm[0][None],
        pool_w[0].astype(BF16), pool_scale[0][None], w_out[0],
        batch=batch, d_pool=d_pool, d_att=d_att)

    att, *w2_b = _attention(q, k, v, km, vm, aug, augm,
                            (ffn2_w_gate[0], ffn2_w_up[0], ffn2_w_down[0]), batch=batch)
    h2 = _out_proj(h1, pool, att, w_out_b)

    out = _ffn_tail(h2, ffn2_norm[0][None], *w2_b, first_tile=0, in_place=True)
    return out.reshape(batch, seq, d)
```

```python
import functools

import jax
import jax.numpy as jnp
from jax import lax
from jax.experimental import pallas as pl
from jax.experimental.pallas import tpu as pltpu

F32 = jnp.float32
BF16 = jnp.bfloat16

EPS = 1e-6
N_META = 16
POOL_WINDOWS = (2, 4, 8, 16)
HEAD_DIM = 128
LANES = 128
SUBLANES = 8
META_PAD = 128
NEG_INF = float("-inf")
BIAS_SCALE = HEAD_DIM ** 0.5
EXP2_SCALE = 1.4426950408889634 / BIAS_SCALE

VMEM_LIMIT = 60 * 1024 * 1024
ROW_CHUNK = 256
COL_CHUNK = 512


def _cparams(n_axes):
    return pltpu.CompilerParams(
        dimension_semantics=("arbitrary",) * n_axes,
        vmem_limit_bytes=VMEM_LIMIT,
    )


def _rmsnorm_rows(x, gain):
    ms = jnp.mean(x * x, axis=-1, keepdims=True)
    return x * lax.rsqrt(ms + EPS) * gain


def _dot(a, b):
    return jnp.dot(a, b, preferred_element_type=F32)


def _dot_nt(a, b):
    return lax.dot_general(a, b, (((1,), (1,)), ((), ())), preferred_element_type=F32)


def _swiglu_accumulate(xn_ref, wg, wu, wd_cols, o_ref, oe_ref, *, tm, col_chunk, init_ref=None):
    d = o_ref.shape[1]
    xn = xn_ref[...]
    gate = _dot(xn, wg)
    up = _dot(xn, wu)
    hmid = (gate / (1.0 + jnp.exp(-gate)) * up).astype(BF16)
    for c in range(0, d, col_chunk):
        cols = slice(c, c + col_chunk)
        part = _dot(hmid, wd_cols(c, c + col_chunk))
        base = o_ref if init_ref is None else init_ref
        o_ref[:, cols] = base[:, cols] + part[:tm]
        if oe_ref is not None:
            oe_ref[:, cols] += part[tm:]


def _ffn_head_kernel(*refs, tm, n_extra, row_chunk, col_chunk):
    if n_extra:
        (x_ref, e_ref, g_ref, wg_ref, wu_ref, wd_ref,
         o_ref, oe_ref, wgb_ref, wub_ref, wdb_ref, xn_ref) = refs
    else:
        x_ref, g_ref, wg_ref, wu_ref, wd_ref, o_ref, wgb_ref, wub_ref, wdb_ref, xn_ref = refs
        e_ref = oe_ref = None
    f = pl.program_id(0)

    @pl.when(f == 0)
    def _():
        gain = g_ref[...]
        for r in range(0, tm, row_chunk):
            xn_ref[r:r + row_chunk, :] = _rmsnorm_rows(x_ref[r:r + row_chunk, :], gain).astype(BF16)
            o_ref[r:r + row_chunk, :] = x_ref[r:r + row_chunk, :]
        if n_extra:
            xn_ref[tm:tm + n_extra, :] = _rmsnorm_rows(e_ref[...], gain).astype(BF16)
            oe_ref[...] = e_ref[...]

    wg = wg_ref[...].astype(BF16)
    wu = wu_ref[...].astype(BF16)
    wd = (0.5 * wd_ref[...]).astype(BF16)
    tf = wg.shape[1]
    for half in range(2):
        @pl.when(f % 2 == half)
        def _():
            wgb_ref[:, half * tf:(half + 1) * tf] = wg
            wub_ref[:, half * tf:(half + 1) * tf] = wu
    wdb_ref[...] = wd
    _swiglu_accumulate(xn_ref, wg, wu, lambda c0, c1: wd[:, c0:c1], o_ref, oe_ref,
                       tm=tm, col_chunk=col_chunk)


def _ffn_tail_kernel(*refs, tm, side_rows, side_chunks, row_chunk, col_chunk):
    x_ref, g_ref, wg_ref, wu_ref, wd_ref = refs[:5]
    xn_ref = refs[-1]
    i, f, nf = pl.program_id(0), pl.program_id(1), pl.num_programs(1)
    if side_rows:
        side_ref, o_ref, sideb_ref = refs[5], refs[6], refs[7]
    else:
        o_ref = refs[5]

    @pl.when(f == 0)
    def _():
        gain = g_ref[...]
        for r in range(0, tm, row_chunk):
            xn_ref[r:r + row_chunk, :] = _rmsnorm_rows(x_ref[r:r + row_chunk, :], gain).astype(BF16)

    def step(first):
        if side_rows:
            chunk = jnp.minimum(i * nf + f, side_chunks - 1)
            row = (chunk * side_ref.shape[0]
                   + lax.broadcasted_iota(jnp.int32, (side_ref.shape[0], 1), 0))
            sideb_ref[...] = jnp.where(row < side_rows, side_ref[...], 0.0).T.astype(BF16)
        _swiglu_accumulate(xn_ref, wg_ref[...], wu_ref[...], lambda c0, c1: wd_ref[:, c0:c1],
                           o_ref, None, tm=tm, col_chunk=col_chunk,
                           init_ref=x_ref if first else None)

    pl.when(f == 0)(functools.partial(step, True))
    pl.when(f > 0)(functools.partial(step, False))


def _ffn_head(x, extra, gain, wg, wu, wd, *, tm=1024, tf=256):
    n, d = x.shape
    dff = wg.shape[1]
    n_extra = 0 if extra is None else extra.shape[0]
    assert n % tm == 0 and dff % tf == 0
    in_specs = [pl.BlockSpec((tm, d), lambda f: (0, 0), pipeline_mode=pl.Buffered(1))]
    args = [x]
    if n_extra:
        in_specs.append(pl.BlockSpec((n_extra, d), lambda f: (0, 0)))
        args.append(extra)
    in_specs += [
        pl.BlockSpec((1, d), lambda f: (0, 0)),
        pl.BlockSpec((d, tf), lambda f: (0, f)),
        pl.BlockSpec((d, tf), lambda f: (0, f)),
        pl.BlockSpec((tf, d), lambda f: (f, 0)),
    ]
    args += [gain, wg, wu, wd]
    out_shape = [jax.ShapeDtypeStruct((tm, d), F32)]
    out_specs = [pl.BlockSpec((tm, d), lambda f: (0, 0))]
    if n_extra:
        out_shape.append(jax.ShapeDtypeStruct((n_extra, d), F32))
        out_specs.append(pl.BlockSpec((n_extra, d), lambda f: (0, 0)))
    out_shape += [jax.ShapeDtypeStruct(wg.shape, BF16), jax.ShapeDtypeStruct(wu.shape, BF16),
                  jax.ShapeDtypeStruct(wd.shape, BF16)]
    assert (dff // tf) % 2 == 0
    out_specs += [pl.BlockSpec((d, 2 * tf), lambda f: (0, f // 2)),
                  pl.BlockSpec((d, 2 * tf), lambda f: (0, f // 2)),
                  pl.BlockSpec((tf, d), lambda f: (f, 0))]
    head = pl.pallas_call(
        functools.partial(_ffn_head_kernel, tm=tm, n_extra=n_extra, row_chunk=ROW_CHUNK,
                          col_chunk=COL_CHUNK),
        grid=(dff // tf,),
        in_specs=in_specs,
        out_specs=out_specs,
        out_shape=out_shape,
        scratch_shapes=[pltpu.VMEM((tm + n_extra, d), BF16)],
        compiler_params=_cparams(1),
        name="ffn_head",
    )(*args)
    return head[0], (head[1] if n_extra else None), tuple(head[-3:])


def _ffn_tail(x, gain, wgb, wub, wdb, *, first_tile, in_place, side_t=None, tm=1024, tf=512,
              side_chunk=LANES):
    n, d = x.shape
    dff = wgb.shape[1]
    assert n % tm == 0 and dff % tf == 0
    n_tiles = n // tm - first_tile
    nf = dff // tf
    in_specs = [
        pl.BlockSpec((tm, d), lambda i, f: (i + first_tile, 0)),
        pl.BlockSpec((1, d), lambda i, f: (0, 0)),
        pl.BlockSpec((d, tf), lambda i, f: (0, f)),
        pl.BlockSpec((d, tf), lambda i, f: (0, f)),
        pl.BlockSpec((tf, d), lambda i, f: (f, 0)),
    ]
    args = [x, gain, wgb, wub, wdb]
    out_shape = [jax.ShapeDtypeStruct((n if in_place else n_tiles * tm, d), F32)]
    out_specs = [pl.BlockSpec((tm, d), lambda i, f: (i + (first_tile if in_place else 0), 0))]
    n_chunks = 0
    if side_t is not None:
        cols, rows = side_t.shape
        n_chunks = pl.cdiv(cols, side_chunk)
        assert n_chunks <= n_tiles * nf
        chunk_of = lambda i, f: jnp.minimum(i * nf + f, n_chunks - 1)
        in_specs.append(pl.BlockSpec((side_chunk, rows), lambda i, f: (chunk_of(i, f), 0)))
        args.append(side_t)
        out_shape.append(jax.ShapeDtypeStruct((rows, n_chunks * side_chunk), BF16))
        out_specs.append(pl.BlockSpec((rows, side_chunk), lambda i, f: (0, chunk_of(i, f))))
    tail = pl.pallas_call(
        functools.partial(_ffn_tail_kernel, tm=tm,
                          side_rows=0 if side_t is None else side_t.shape[0],
                          side_chunks=n_chunks, row_chunk=ROW_CHUNK, col_chunk=COL_CHUNK),
        grid=(n_tiles, nf),
        in_specs=in_specs,
        out_specs=out_specs,
        out_shape=out_shape,
        scratch_shapes=[pltpu.VMEM((tm, d), BF16)],
        input_output_aliases={0: 0} if in_place else {},
        compiler_params=_cparams(2),
        name="ffn_tail",
    )(*args)
    return tail[0] if side_t is None else (tail[0], tail[1])


def _log_sigmoid(x):
    return jnp.minimum(x, 0.0) - jnp.log1p(jnp.exp(-jnp.abs(x)))


def _split3(x):
    hi = x.astype(BF16)
    r1 = x - hi.astype(F32)
    mid = r1.astype(BF16)
    lo = (r1 - mid.astype(F32)).astype(BF16)
    return hi, mid, lo


def _cumsum_rows(x, tri, head_lane):
    hi, mid, lo = (p.astype(F32) for p in _split3(x))
    packed = hi + pltpu.roll(mid, SUBLANES, axis=1) + pltpu.roll(lo, 2 * SUBLANES, axis=1)
    c = _dot(tri, packed.astype(BF16))
    c = c + pltpu.roll(c, LANES - SUBLANES, axis=1) + pltpu.roll(c, LANES - 2 * SUBLANES, axis=1)
    return jnp.where(head_lane, c, 0.0)


def _bias_columns(cum):
    hi, mid, lo = (p.astype(F32) for p in _split3(cum * BIAS_SCALE))
    roll = lambda x, g: pltpu.roll(x, g * SUBLANES, axis=1)
    cols = hi + roll(mid, 1) + roll(lo, 2) - roll(hi, 3) - roll(mid, 4) - roll(lo, 5)
    return cols.astype(BF16)


def _head_rmsnorm(z, gain, n_heads):
    outs = []
    for h in range(n_heads):
        outs.append(_rmsnorm_rows(z[:, h * HEAD_DIM:(h + 1) * HEAD_DIM], gain))
    return jnp.concatenate(outs, axis=-1)


def _window_means(e, w):
    assert w & (w - 1) == 0 and w <= N_META
    s, span = e, 1
    while span < w:
        s = s + pltpu.roll(s, span, axis=0)
        span *= 2
    return s[N_META:, :] / float(w)


def _inproj_kernel(ha_ref, hb_ref, hm_ref, gn_ref, w_ref, b_ref, qn_ref, kn_ref, pw_ref, ps_ref,
                   side_ref,
                   pool_ref, q_ref, k_ref, v_ref, aug_ref, km_ref, vm_ref, augm_ref, sideb_ref,
                   u_ref, pe_ref, pmeta_ref, ccol_ref, mcol_ref,
                   *, tm, tiles_a, tiles_per_batch, n_heads, d_pool, d_att, row_chunk):
    i = pl.program_id(0)
    gain = gn_ref[...]
    sideb_ref[...] = side_ref[...].astype(BF16)
    head_lane = lax.broadcasted_iota(jnp.int32, (1, LANES), 1) < n_heads
    w_pool = lambda: w_ref[:, 0:d_pool]
    w_q = lambda: w_ref[:, d_pool:d_pool + d_att]
    w_k = lambda: w_ref[:, d_pool + d_att:d_pool + 2 * d_att]
    w_vf = lambda: w_ref[:, d_pool + 2 * d_att:]

    def log_forget(zvf):
        lf = _log_sigmoid(zvf[:, d_att:] + b_ref[...])
        return jnp.where(head_lane, lf, 0.0)

    def tri(t):
        r_i = lax.broadcasted_iota(jnp.int32, (t, t), 0)
        c_i = lax.broadcasted_iota(jnp.int32, (t, t), 1)
        return (c_i <= r_i).astype(BF16)

    @pl.when(i == 0)
    def _():
        u_ref[0:META_PAD, :] = jnp.zeros((META_PAD, u_ref.shape[1]), BF16)
        u_ref[0:N_META, :] = _rmsnorm_rows(hm_ref[...], gain).astype(BF16)
        um = u_ref[0:META_PAD, :]
        row_valid = lax.broadcasted_iota(jnp.int32, (META_PAD, 1), 0) < N_META
        pmeta_ref[...] = _dot(um, w_pool())[0:N_META, :]
        km = _head_rmsnorm(_dot(um, w_k()), kn_ref[...], n_heads)
        km_ref[...] = jnp.where(row_valid, km, 0.0).astype(BF16)
        zvf = _dot(um, w_vf())
        vm_ref[...] = zvf[:, 0:d_att].astype(BF16)
        cum_m = _cumsum_rows(jnp.where(row_valid, log_forget(zvf), 0.0), tri(META_PAD), head_lane)
        augm_ref[...] = _bias_columns(cum_m)
        mcol_ref[...] = cum_m[META_PAD - 1:META_PAD, :]

    @pl.when(i % tiles_per_batch == 0)
    def _():
        ccol_ref[...] = mcol_ref[...]
        pe_ref[0:N_META, :] = pmeta_ref[...]

    for r in range(0, tm, row_chunk):
        h = jnp.where(i < tiles_a, ha_ref[r:r + row_chunk, :], hb_ref[r:r + row_chunk, :])
        u_ref[r:r + row_chunk, :] = _rmsnorm_rows(h, gain).astype(BF16)
    u = u_ref[0:tm, :]

    pe_ref[N_META:N_META + tm, :] = _dot(u, w_pool())
    q_ref[...] = _head_rmsnorm(_dot(u, w_q()), qn_ref[...], n_heads).astype(BF16)
    k_ref[...] = _head_rmsnorm(_dot(u, w_k()), kn_ref[...], n_heads).astype(BF16)
    zvf = _dot(u, w_vf())
    v_ref[...] = zvf[:, 0:d_att].astype(BF16)

    gw = d_pool // len(POOL_WINDOWS)
    for g, w in enumerate(POOL_WINDOWS):
        lanes = slice(g * gw, (g + 1) * gw)
        e = pe_ref[:, lanes]
        pooled = (_window_means(e, w) - e[N_META:, :]).astype(BF16)
        mixed = _dot(pooled, pw_ref[g]) * ps_ref[:, lanes]
        pool_ref[:, lanes] = mixed.astype(BF16)
    pe_ref[0:N_META, :] = pe_ref[tm:tm + N_META, :]

    cum_col = _cumsum_rows(log_forget(zvf), tri(tm), head_lane) + ccol_ref[...]
    aug_ref[...] = _bias_columns(cum_col)
    ccol_ref[...] = cum_col[tm - 1:tm, :]


def _const_spec(shape, index=None):
    nd = len(shape)
    index = (0,) * nd if index is None else index
    return pl.BlockSpec(shape, lambda i: index, pipeline_mode=pl.Buffered(1))


def _in_proj(h1, h1m, gn, w_in, b_gate, qn, kn, pw, ps, side, *, batch, d_pool, d_att, tm=512):
    h1a, h1b = h1
    d = h1a.shape[1]
    n = h1a.shape[0] + h1b.shape[0]
    assert h1a.shape[0] % tm == 0 and h1b.shape[0] % tm == 0
    tiles_a = h1a.shape[0] // tm
    seq = n // batch
    n_heads = d_att // HEAD_DIM
    assert seq % tm == 0 and tm >= META_PAD and d_pool == d_att and n_heads <= SUBLANES
    assert w_in.shape == (d, d_pool + 3 * d_att + LANES)
    tiles_per_batch = seq // tm
    side_rows = side.shape[0] // (n // tm)
    assert side_rows * (n // tm) == side.shape[0] and side_rows % SUBLANES == 0
    side_spec = pl.BlockSpec((side_rows, side.shape[1]), lambda i: (i, 0))
    rows = lambda w: pl.BlockSpec((tm, w), lambda i: (i, 0))
    in_specs = [
        pl.BlockSpec((tm, d), lambda i: (jnp.minimum(i, tiles_a - 1), 0)),
        pl.BlockSpec((tm, d), lambda i: (jnp.maximum(i - tiles_a, 0), 0)),
        _const_spec((N_META, d)),
        _const_spec((1, d)),
        _const_spec(w_in.shape),
        _const_spec((1, LANES)),
        _const_spec((1, HEAD_DIM)),
        _const_spec((1, HEAD_DIM)),
        _const_spec(pw.shape),
        _const_spec((1, d_pool)),
        side_spec,
    ]
    out_shape = [
        jax.ShapeDtypeStruct((n, d_pool), BF16),
        jax.ShapeDtypeStruct((n, d_att), BF16),
        jax.ShapeDtypeStruct((n, d_att), BF16),
        jax.ShapeDtypeStruct((n, d_att), BF16),
        jax.ShapeDtypeStruct((n, LANES), BF16),
        jax.ShapeDtypeStruct((META_PAD, d_att), BF16),
        jax.ShapeDtypeStruct((META_PAD, d_att), BF16),
        jax.ShapeDtypeStruct((META_PAD, LANES), BF16),
        jax.ShapeDtypeStruct(side.shape, BF16),
    ]
    out_specs = [
        rows(d_pool), rows(d_att), rows(d_att), rows(d_att), rows(LANES),
        pl.BlockSpec((META_PAD, d_att), lambda i: (0, 0)),
        pl.BlockSpec((META_PAD, d_att), lambda i: (0, 0)),
        pl.BlockSpec((META_PAD, LANES), lambda i: (0, 0)),
        side_spec,
    ]
    scratch = [
        pltpu.VMEM((tm, d), BF16),
        pltpu.VMEM((tm + N_META, d_pool), F32),
        pltpu.VMEM((N_META, d_pool), F32),
        pltpu.VMEM((1, LANES), F32),
        pltpu.VMEM((1, LANES), F32),
    ]
    kern = functools.partial(_inproj_kernel, tm=tm, tiles_a=tiles_a, tiles_per_batch=tiles_per_batch,
                             n_heads=n_heads, d_pool=d_pool, d_att=d_att, row_chunk=ROW_CHUNK)
    return pl.pallas_call(
        kern,
        grid=(n // tm,),
        in_specs=in_specs,
        out_specs=out_specs,
        out_shape=out_shape,
        scratch_shapes=scratch,
        compiler_params=_cparams(1),
        name="in_proj",
    )(h1a, h1b, h1m, gn, w_in, b_gate, qn, kn, pw, ps, side)


def _attn_kernel(q_ref, k_ref, v_ref, km_ref, vm_ref, aug_ref, augm_ref, wg_ref, wu_ref, wd_ref,
                 o_ref, wgb_ref, wub_ref, wdb_ref, qa_ref, ka_ref, va_ref, *, tq):
    h = pl.program_id(1)
    seq = q_ref.shape[0]
    lane = lax.broadcasted_iota(jnp.int32, (1, LANES), 1)
    group = lane >> (SUBLANES.bit_length() - 1)
    mine = (lane & (SUBLANES - 1)) == h
    one_hot = lambda lo: jnp.where(mine & (group >= lo) & (group < lo + 3), 1.0, 0.0).astype(BF16)
    keep = lambda lo: jnp.where((group >= lo) & (group < lo + 3), 0.0, 1.0).astype(BF16)
    qa_ref[:, 0:HEAD_DIM] = q_ref[...]
    qa_ref[:, HEAD_DIM:] = aug_ref[...] * keep(3) + one_hot(3)
    ka_ref[:, 0:HEAD_DIM] = k_ref[...]
    ka_ref[:, HEAD_DIM:] = aug_ref[...] * keep(0) + one_hot(0)
    kma = jnp.concatenate([km_ref[...], augm_ref[...] * keep(0) + one_hot(0)], axis=1)
    va_ref[:, 0:HEAD_DIM] = v_ref[...]
    va_ref[:, HEAD_DIM:] = jnp.ones((seq, HEAD_DIM), BF16)
    vma = jnp.concatenate([vm_ref[...], jnp.ones((META_PAD, HEAD_DIM), BF16)], axis=1)

    hq = tq // 2
    rows_of = lambda w: lax.broadcasted_iota(jnp.int32, (hq, w), 0)
    cols_of = lambda w: lax.broadcasted_iota(jnp.int32, (hq, w), 1)
    causal_top = cols_of(hq) <= rows_of(hq)
    causal_bot = cols_of(tq) <= rows_of(tq) + hq
    tiles = list(reversed(range(seq // tq)))
    scores = {}
    for qi in tiles:
        rows = slice(qi * tq, (qi + 1) * tq)
        q = qa_ref[rows, :]
        blocks = [slice(j * tq, (j + 1) * tq) for j in range(qi)]
        raw_m = jnp.where(lane < N_META, _dot_nt(q, kma), NEG_INF)
        raws = [_dot_nt(q, ka_ref[cols, :]) for cols in blocks]
        raw_t = jnp.where(causal_top, _dot_nt(q[0:hq], ka_ref[qi * tq:qi * tq + hq, :]), NEG_INF)
        raw_b = jnp.where(causal_bot, _dot_nt(q[hq:], ka_ref[rows, :]), NEG_INF)
        scores[qi] = (rows, blocks, raw_m, raws, raw_t, raw_b)
    probs = {}
    for qi in tiles:
        rows, blocks, raw_m, raws, raw_t, raw_b = scores[qi]
        halves = []
        for r0, raw_d in ((0, raw_t), (hq, raw_b)):
            part = slice(r0, r0 + hq)
            full = [raw_m[part]] + [r[part] for r in raws]
            m = jnp.maximum(jnp.max(full[0], axis=-1, keepdims=True),
                            jnp.max(raw_d, axis=-1, keepdims=True))
            if raws:
                m = jnp.maximum(m, jnp.max(functools.reduce(jnp.maximum, full[1:]),
                                           axis=-1, keepdims=True))
            halves.append([jnp.exp2(EXP2_SCALE * (r - m)) for r in full + [raw_d]])
        top, bot = halves
        p_m = jnp.concatenate([top[0], bot[0]], axis=0)
        ps = [jnp.concatenate([t, b], axis=0) for t, b in zip(top[1:-1], bot[1:-1])]
        probs[qi] = (rows, blocks, p_m, ps, top[-1], bot[-1])
    wgb_ref[...] = wg_ref[...].astype(BF16)
    wub_ref[...] = wu_ref[...].astype(BF16)
    wdb_ref[...] = (0.5 * wd_ref[...]).astype(BF16)
    for qi in tiles:
        rows, blocks, p_m, ps, p_t, p_b = probs[qi]
        acc = _dot(p_m.astype(BF16), vma)
        for p, cols in zip(ps, blocks):
            acc = acc + _dot(p.astype(BF16), va_ref[cols, :])
        acc = acc + jnp.concatenate(
            [_dot(p_t.astype(BF16), va_ref[qi * tq:qi * tq + hq, :]),
             _dot(p_b.astype(BF16), va_ref[rows, :])], axis=0)
        o_ref[rows, :] = (acc[:, 0:HEAD_DIM] / acc[:, HEAD_DIM:]).astype(BF16)


def _attention(q, k, v, km, vm, aug, augm, next_ffn, *, batch, tq=512):
    n, d_att = q.shape
    seq = n // batch
    n_heads = d_att // HEAD_DIM
    steps = batch * n_heads
    tok = pl.BlockSpec((seq, HEAD_DIM), lambda b, h: (b, h))
    meta = pl.BlockSpec((META_PAD, HEAD_DIM), lambda b, h: (0, h))

    def chunk(w):
        assert w.shape[0] % (steps * 2 * SUBLANES) == 0
        return pl.BlockSpec((w.shape[0] // steps, w.shape[1]), lambda b, h: (b * n_heads + h, 0))

    side_specs = [chunk(w) for w in next_ffn]
    return pl.pallas_call(
        functools.partial(_attn_kernel, tq=tq),
        grid=(batch, n_heads),
        in_specs=[
            tok, tok, tok, meta, meta,
            pl.BlockSpec((seq, LANES), lambda b, h: (b, 0)),
            pl.BlockSpec((META_PAD, LANES), lambda b, h: (0, 0)),
        ] + side_specs,
        out_specs=[tok] + side_specs,
        out_shape=[jax.ShapeDtypeStruct((n, d_att), BF16)]
        + [jax.ShapeDtypeStruct(w.shape, BF16) for w in next_ffn],
        scratch_shapes=[pltpu.VMEM((seq, 2 * HEAD_DIM), BF16),
                        pltpu.VMEM((seq, 2 * HEAD_DIM), BF16),
                        pltpu.VMEM((seq, 2 * HEAD_DIM), BF16)],
        compiler_params=_cparams(2),
        name="fox_attn",
    )(q, k, v, km, vm, aug, augm, *next_ffn)


def _outproj_kernel(ha_ref, hb_ref, pool_ref, att_ref, wop_ref, woa_ref, o_ref, *, tiles_a,
                    col_chunk):
    i = pl.program_id(0)
    d = o_ref.shape[1]
    for c in range(0, d, col_chunk):
        cols = slice(c, c + col_chunk)
        h = jnp.where(i < tiles_a, ha_ref[:, cols], hb_ref[:, cols])
        o_ref[:, cols] = (h + _dot(pool_ref[...], wop_ref[:, cols])
                          + _dot(att_ref[...], woa_ref[:, cols]))


def _out_proj(h1, pool, att, w_out, *, tm=512):
    h1a, h1b = h1
    d = h1a.shape[1]
    n = h1a.shape[0] + h1b.shape[0]
    assert h1a.shape[0] % tm == 0 and h1b.shape[0] % tm == 0
    tiles_a = h1a.shape[0] // tm
    d_pool, d_att = pool.shape[1], att.shape[1]
    assert d_pool == d_att
    rows = lambda w: pl.BlockSpec((tm, w), lambda i: (i, 0))
    return pl.pallas_call(
        functools.partial(_outproj_kernel, tiles_a=tiles_a, col_chunk=COL_CHUNK),
        grid=(n // tm,),
        in_specs=[pl.BlockSpec((tm, d), lambda i: (jnp.minimum(i, tiles_a - 1), 0)),
                  pl.BlockSpec((tm, d), lambda i: (jnp.maximum(i - tiles_a, 0), 0)),
                  rows(d_pool), rows(d_att),
                  _const_spec((d_pool, d), (0, 0)), _const_spec((d_att, d), (1, 0))],
        out_specs=rows(d),
        out_shape=jax.ShapeDtypeStruct((n, d), F32),
        compiler_params=_cparams(1),
        name="out_proj",
    )(h1a, h1b, pool, att, w_out, w_out)


def kernel(x, meta_tokens, ffn1_norm, ffn1_w_gate, ffn1_w_up, ffn1_w_down, mix_norm, w_in,
           b_forget, q_norm, k_norm, pool_w, pool_scale, w_out, ffn2_norm, ffn2_w_gate,
           ffn2_w_up, ffn2_w_down):
    batch, seq, d = x.shape
    n_heads = b_forget.shape[1]
    d_att = n_heads * HEAD_DIM
    d_pool = pool_scale.shape[1]
    assert meta_tokens.shape[0] == N_META
    assert ffn1_norm.shape[0] == 1, "one layer: the meta rows past the mixers never reach the output"

    hx = x.reshape(batch * seq, d)
    hm = meta_tokens.astype(x.dtype)
    g1 = ffn1_norm[0][None]
    h1_head, hm, w1_b = _ffn_head(hx, hm, g1, ffn1_w_gate[0], ffn1_w_up[0], ffn1_w_down[0])
    h1_tail, w_in_b = _ffn_tail(hx, g1, *w1_b, first_tile=1, in_place=False, side_t=w_in[0].T)
    h1 = (h1_head, h1_tail)

    b_gate = jnp.pad(b_forget[0][None], ((0, 0), (0, LANES - n_heads)))
    pool, q, k, v, aug, km, vm, augm, w_out_b = _in_proj(
        h1, hm, mix_norm[0][None], w_in_b, b_gate, q_norm[0][None], k_norm[0][None],
        pool_w[0].astype(BF16), pool_scale[0][None], w_out[0],
        batch=batch, d_pool=d_pool, d_att=d_att)

    att, *w2_b = _attention(q, k, v, km, vm, aug, augm,
                            (ffn2_w_gate[0], ffn2_w_up[0], ffn2_w_down[0]), batch=batch)
    h2 = _out_proj(h1, pool, att, w_out_b)

    out = _ffn_tail(h2, ffn2_norm[0][None], *w2_b, first_tile=0, in_place=True)
    return out.reshape(batch, seq, d)
```

```python
import functools

import jax
import jax.numpy as jnp
from jax import lax
from jax.experimental import pallas as pl
from jax.experimental.pallas import tpu as pltpu

F32 = jnp.float32
BF16 = jnp.bfloat16

EPS = 1e-6
N_META = 16
POOL_WINDOWS = (2, 4, 8, 16)
HEAD_DIM = 128
LANES = 128
SUBLANES = 8
META_PAD = 128
NEG_INF = float("-inf")
BIAS_SCALE = HEAD_DIM ** 0.5
EXP2_SCALE = 1.4426950408889634 / BIAS_SCALE

VMEM_LIMIT = 60 * 1024 * 1024
ROW_CHUNK = 256
COL_CHUNK = 512


def _cparams(n_axes):
    return pltpu.CompilerParams(
        dimension_semantics=("arbitrary",) * n_axes,
        vmem_limit_bytes=VMEM_LIMIT,
    )


def _rmsnorm_rows(x, gain):
    ms = jnp.mean(x * x, axis=-1, keepdims=True)
    return x * lax.rsqrt(ms + EPS) * gain


def _dot(a, b):
    return jnp.dot(a, b, preferred_element_type=F32)


def _dot_nt(a, b):
    return lax.dot_general(a, b, (((1,), (1,)), ((), ())), preferred_element_type=F32)


def _swiglu_accumulate(xn_ref, wg, wu, wd_cols, o_ref, oe_ref, *, tm, col_chunk, init_ref=None):
    d = o_ref.shape[1]
    xn = xn_ref[...]
    gate = _dot(xn, wg)
    up = _dot(xn, wu)
    hmid = (gate / (1.0 + jnp.exp(-gate)) * up).astype(BF16)
    for c in range(0, d, col_chunk):
        cols = slice(c, c + col_chunk)
        part = _dot(hmid, wd_cols(c, c + col_chunk))
        base = o_ref if init_ref is None else init_ref
        o_ref[:, cols] = base[:, cols] + part[:tm]
        if oe_ref is not None:
            oe_ref[:, cols] += part[tm:]


def _ffn_head_kernel(*refs, tm, n_extra, row_chunk, col_chunk):
    if n_extra:
        (x_ref, e_ref, g_ref, wg_ref, wu_ref, wd_ref,
         o_ref, oe_ref, wgb_ref, wub_ref, wdb_ref, xn_ref, gate_ref, up_ref) = refs
    else:
        (x_ref, g_ref, wg_ref, wu_ref, wd_ref,
         o_ref, wgb_ref, wub_ref, wdb_ref, xn_ref, gate_ref, up_ref) = refs
        e_ref = oe_ref = None
    s = pl.program_id(0)
    d = x_ref.shape[1]

    @pl.when(s == 0)
    def _():
        gain = g_ref[...]
        for r in range(0, tm, row_chunk):
            xn_ref[r:r + row_chunk, :] = _rmsnorm_rows(x_ref[r:r + row_chunk, :], gain).astype(BF16)
            o_ref[r:r + row_chunk, :] = x_ref[r:r + row_chunk, :]
        if n_extra:
            xn_ref[tm:tm + n_extra, :] = _rmsnorm_rows(e_ref[...], gain).astype(BF16)
            oe_ref[...] = e_ref[...]

    def half(k):
        wg = wg_ref[...].astype(BF16)
        wu = wu_ref[...].astype(BF16)
        wgb_ref[...] = wg
        wub_ref[...] = wu
        xn = xn_ref[:, k * (d // 2):(k + 1) * (d // 2)]
        return _dot(xn, wg), _dot(xn, wu)

    @pl.when(s % 2 == 0)
    def _():
        gate_ref[...], up_ref[...] = half(0)

    @pl.when(s % 2 == 1)
    def _():
        gate, up = half(1)
        gate = gate_ref[...] + gate
        up = up_ref[...] + up
        hmid = (gate / (1.0 + jnp.exp(-gate)) * up).astype(BF16)
        wd = (0.5 * wd_ref[...]).astype(BF16)
        wdb_ref[...] = wd
        for c in range(0, d, col_chunk):
            cols = slice(c, c + col_chunk)
            part = _dot(hmid, wd[:, cols])
            o_ref[:, cols] += part[:tm]
            if n_extra:
                oe_ref[:, cols] += part[tm:]


def _ffn_tail_kernel(*refs, tm, side_rows, side_chunks, row_chunk, col_chunk):
    x_ref, g_ref, wg_ref, wu_ref, wd_ref = refs[:5]
    xn_ref = refs[-1]
    i, f, nf = pl.program_id(0), pl.program_id(1), pl.num_programs(1)
    if side_rows:
        side_ref, o_ref, sideb_ref = refs[5], refs[6], refs[7]
    else:
        o_ref = refs[5]

    @pl.when(f == 0)
    def _():
        gain = g_ref[...]
        for r in range(0, tm, row_chunk):
            xn_ref[r:r + row_chunk, :] = _rmsnorm_rows(x_ref[r:r + row_chunk, :], gain).astype(BF16)

    def step(first):
        if side_rows:
            chunk = jnp.minimum(i * nf + f, side_chunks - 1)
            row = (chunk * side_ref.shape[0]
                   + lax.broadcasted_iota(jnp.int32, (side_ref.shape[0], 1), 0))
            sideb_ref[...] = jnp.where(row < side_rows, side_ref[...], 0.0).T.astype(BF16)
        _swiglu_accumulate(xn_ref, wg_ref[...], wu_ref[...], lambda c0, c1: wd_ref[:, c0:c1],
                           o_ref, None, tm=tm, col_chunk=col_chunk,
                           init_ref=x_ref if first else None)

    pl.when(f == 0)(functools.partial(step, True))
    pl.when(f > 0)(functools.partial(step, False))


def _ffn_head(x, extra, gain, wg, wu, wd, *, tm=1024, tf=512):
    n, d = x.shape
    dff = wg.shape[1]
    n_extra = 0 if extra is None else extra.shape[0]
    assert n % tm == 0 and dff % tf == 0 and d % 2 == 0
    half_rows = pl.BlockSpec((d // 2, tf), lambda s: (s % 2, s // 2))
    chunk_rows = pl.BlockSpec((tf, d), lambda s: (s // 2, 0))
    in_specs = [pl.BlockSpec((tm, d), lambda s: (0, 0), pipeline_mode=pl.Buffered(1))]
    args = [x]
    if n_extra:
        in_specs.append(pl.BlockSpec((n_extra, d), lambda s: (0, 0)))
        args.append(extra)
    in_specs += [pl.BlockSpec((1, d), lambda s: (0, 0)), half_rows, half_rows, chunk_rows]
    args += [gain, wg, wu, wd]
    out_shape = [jax.ShapeDtypeStruct((tm, d), F32)]
    out_specs = [pl.BlockSpec((tm, d), lambda s: (0, 0))]
    if n_extra:
        out_shape.append(jax.ShapeDtypeStruct((n_extra, d), F32))
        out_specs.append(pl.BlockSpec((n_extra, d), lambda s: (0, 0)))
    out_shape += [jax.ShapeDtypeStruct(wg.shape, BF16), jax.ShapeDtypeStruct(wu.shape, BF16),
                  jax.ShapeDtypeStruct(wd.shape, BF16)]
    out_specs += [half_rows, half_rows, chunk_rows]
    head = pl.pallas_call(
        functools.partial(_ffn_head_kernel, tm=tm, n_extra=n_extra, row_chunk=ROW_CHUNK,
                          col_chunk=COL_CHUNK),
        grid=(2 * (dff // tf),),
        in_specs=in_specs,
        out_specs=out_specs,
        out_shape=out_shape,
        scratch_shapes=[pltpu.VMEM((tm + n_extra, d), BF16),
                        pltpu.VMEM((tm + n_extra, tf), F32),
                        pltpu.VMEM((tm + n_extra, tf), F32)],
        compiler_params=_cparams(1),
        name="ffn_head",
    )(*args)
    return head[0], (head[1] if n_extra else None), tuple(head[-3:])


def _ffn_tail(x, gain, wgb, wub, wdb, *, first_tile, in_place, side_t=None, tm=1024, tf=512,
              side_chunk=LANES):
    n, d = x.shape
    dff = wgb.shape[1]
    assert n % tm == 0 and dff % tf == 0
    n_tiles = n // tm - first_tile
    nf = dff // tf
    in_specs = [
        pl.BlockSpec((tm, d), lambda i, f: (i + first_tile, 0)),
        pl.BlockSpec((1, d), lambda i, f: (0, 0)),
        pl.BlockSpec((d, tf), lambda i, f: (0, f)),
        pl.BlockSpec((d, tf), lambda i, f: (0, f)),
        pl.BlockSpec((tf, d), lambda i, f: (f, 0)),
    ]
    args = [x, gain, wgb, wub, wdb]
    out_shape = [jax.ShapeDtypeStruct((n if in_place else n_tiles * tm, d), F32)]
    out_specs = [pl.BlockSpec((tm, d), lambda i, f: (i + (first_tile if in_place else 0), 0))]
    n_chunks = 0
    if side_t is not None:
        cols, rows = side_t.shape
        n_chunks = pl.cdiv(cols, side_chunk)
        assert n_chunks <= n_tiles * nf
        chunk_of = lambda i, f: jnp.minimum(i * nf + f, n_chunks - 1)
        in_specs.append(pl.BlockSpec((side_chunk, rows), lambda i, f: (chunk_of(i, f), 0)))
        args.append(side_t)
        out_shape.append(jax.ShapeDtypeStruct((rows, n_chunks * side_chunk), BF16))
        out_specs.append(pl.BlockSpec((rows, side_chunk), lambda i, f: (0, chunk_of(i, f))))
    tail = pl.pallas_call(
        functools.partial(_ffn_tail_kernel, tm=tm,
                          side_rows=0 if side_t is None else side_t.shape[0],
                          side_chunks=n_chunks, row_chunk=ROW_CHUNK, col_chunk=COL_CHUNK),
        grid=(n_tiles, nf),
        in_specs=in_specs,
        out_specs=out_specs,
        out_shape=out_shape,
        scratch_shapes=[pltpu.VMEM((tm, d), BF16)],
        input_output_aliases={0: 0} if in_place else {},
        compiler_params=_cparams(2),
        name="ffn_tail",
    )(*args)
    return tail[0] if side_t is None else (tail[0], tail[1])


def _log_sigmoid(x):
    return jnp.minimum(x, 0.0) - jnp.log1p(jnp.exp(-jnp.abs(x)))


def _split3(x):
    hi = x.astype(BF16)
    r1 = x - hi.astype(F32)
    mid = r1.astype(BF16)
    lo = (r1 - mid.astype(F32)).astype(BF16)
    return hi, mid, lo


def _cumsum_rows(x, tri, head_lane):
    hi, mid, lo = (p.astype(F32) for p in _split3(x))
    packed = hi + pltpu.roll(mid, SUBLANES, axis=1) + pltpu.roll(lo, 2 * SUBLANES, axis=1)
    c = _dot(tri, packed.astype(BF16))
    c = c + pltpu.roll(c, LANES - SUBLANES, axis=1) + pltpu.roll(c, LANES - 2 * SUBLANES, axis=1)
    return jnp.where(head_lane, c, 0.0)


def _bias_columns(cum):
    hi, mid, lo = (p.astype(F32) for p in _split3(cum * BIAS_SCALE))
    roll = lambda x, g: pltpu.roll(x, g * SUBLANES, axis=1)
    cols = hi + roll(mid, 1) + roll(lo, 2) - roll(hi, 3) - roll(mid, 4) - roll(lo, 5)
    return cols.astype(BF16)


def _head_rmsnorm(z, gain, n_heads):
    outs = []
    for h in range(n_heads):
        outs.append(_rmsnorm_rows(z[:, h * HEAD_DIM:(h + 1) * HEAD_DIM], gain))
    return jnp.concatenate(outs, axis=-1)


def _window_means(e, w):
    assert w & (w - 1) == 0 and w <= N_META
    s, span = e, 1
    while span < w:
        s = s + pltpu.roll(s, span, axis=0)
        span *= 2
    return s[N_META:, :] / float(w)


def _inproj_kernel(ha_ref, hb_ref, hm_ref, gn_ref, w_ref, b_ref, qn_ref, kn_ref, pw_ref, ps_ref,
                   side_ref,
                   pool_ref, q_ref, k_ref, v_ref, aug_ref, km_ref, vm_ref, augm_ref, sideb_ref,
                   u_ref, pe_ref, pmeta_ref, ccol_ref, mcol_ref,
                   *, tm, tiles_a, tiles_per_batch, n_heads, d_pool, d_att, row_chunk):
    i = pl.program_id(0)
    gain = gn_ref[...]
    sideb_ref[...] = side_ref[...].astype(BF16)
    head_lane = lax.broadcasted_iota(jnp.int32, (1, LANES), 1) < n_heads
    w_pool = lambda: w_ref[:, 0:d_pool]
    w_q = lambda: w_ref[:, d_pool:d_pool + d_att]
    w_k = lambda: w_ref[:, d_pool + d_att:d_pool + 2 * d_att]
    w_vf = lambda: w_ref[:, d_pool + 2 * d_att:]

    def log_forget(zvf):
        lf = _log_sigmoid(zvf[:, d_att:] + b_ref[...])
        return jnp.where(head_lane, lf, 0.0)

    def tri(t):
        r_i = lax.broadcasted_iota(jnp.int32, (t, t), 0)
        c_i = lax.broadcasted_iota(jnp.int32, (t, t), 1)
        return (c_i <= r_i).astype(BF16)

    @pl.when(i == 0)
    def _():
        u_ref[0:META_PAD, :] = jnp.zeros((META_PAD, u_ref.shape[1]), BF16)
        u_ref[0:N_META, :] = _rmsnorm_rows(hm_ref[...], gain).astype(BF16)
        um = u_ref[0:META_PAD, :]
        row_valid = lax.broadcasted_iota(jnp.int32, (META_PAD, 1), 0) < N_META
        pmeta_ref[...] = _dot(um, w_pool())[0:N_META, :]
        km = _head_rmsnorm(_dot(um, w_k()), kn_ref[...], n_heads)
        km_ref[...] = jnp.where(row_valid, km, 0.0).astype(BF16)
        zvf = _dot(um, w_vf())
        vm_ref[...] = zvf[:, 0:d_att].astype(BF16)
        cum_m = _cumsum_rows(jnp.where(row_valid, log_forget(zvf), 0.0), tri(META_PAD), head_lane)
        augm_ref[...] = _bias_columns(cum_m)
        mcol_ref[...] = cum_m[META_PAD - 1:META_PAD, :]

    @pl.when(i % tiles_per_batch == 0)
    def _():
        ccol_ref[...] = mcol_ref[...]
        pe_ref[0:N_META, :] = pmeta_ref[...]

    for r in range(0, tm, row_chunk):
        h = jnp.where(i < tiles_a, ha_ref[r:r + row_chunk, :], hb_ref[r:r + row_chunk, :])
        u_ref[r:r + row_chunk, :] = _rmsnorm_rows(h, gain).astype(BF16)
    u = u_ref[0:tm, :]

    pe_ref[N_META:N_META + tm, :] = _dot(u, w_pool())
    q_ref[...] = _head_rmsnorm(_dot(u, w_q()), qn_ref[...], n_heads).astype(BF16)
    k_ref[...] = _head_rmsnorm(_dot(u, w_k()), kn_ref[...], n_heads).astype(BF16)
    zvf = _dot(u, w_vf())
    v_ref[...] = zvf[:, 0:d_att].astype(BF16)

    gw = d_pool // len(POOL_WINDOWS)
    for g, w in enumerate(POOL_WINDOWS):
        lanes = slice(g * gw, (g + 1) * gw)
        e = pe_ref[:, lanes]
        pooled = (_window_means(e, w) - e[N_META:, :]).astype(BF16)
        mixed = _dot(pooled, pw_ref[g]) * ps_ref[:, lanes]
        pool_ref[:, lanes] = mixed.astype(BF16)
    pe_ref[0:N_META, :] = pe_ref[tm:tm + N_META, :]

    cum_col = _cumsum_rows(log_forget(zvf), tri(tm), head_lane) + ccol_ref[...]
    aug_ref[...] = _bias_columns(cum_col)
    ccol_ref[...] = cum_col[tm - 1:tm, :]


def _const_spec(shape, index=None):
    nd = len(shape)
    index = (0,) * nd if index is None else index
    return pl.BlockSpec(shape, lambda i: index, pipeline_mode=pl.Buffered(1))


def _in_proj(h1, h1m, gn, w_in, b_gate, qn, kn, pw, ps, side, *, batch, d_pool, d_att, tm=512):
    h1a, h1b = h1
    d = h1a.shape[1]
    n = h1a.shape[0] + h1b.shape[0]
    assert h1a.shape[0] % tm == 0 and h1b.shape[0] % tm == 0
    tiles_a = h1a.shape[0] // tm
    seq = n // batch
    n_heads = d_att // HEAD_DIM
    assert seq % tm == 0 and tm >= META_PAD and d_pool == d_att and n_heads <= SUBLANES
    assert w_in.shape == (d, d_pool + 3 * d_att + LANES)
    tiles_per_batch = seq // tm
    side_rows = side.shape[0] // (n // tm)
    assert side_rows * (n // tm) == side.shape[0] and side_rows % SUBLANES == 0
    side_spec = pl.BlockSpec((side_rows, side.shape[1]), lambda i: (i, 0))
    rows = lambda w: pl.BlockSpec((tm, w), lambda i: (i, 0))
    in_specs = [
        pl.BlockSpec((tm, d), lambda i: (jnp.minimum(i, tiles_a - 1), 0)),
        pl.BlockSpec((tm, d), lambda i: (jnp.maximum(i - tiles_a, 0), 0)),
        _const_spec((N_META, d)),
        _const_spec((1, d)),
        _const_spec(w_in.shape),
        _const_spec((1, LANES)),
        _const_spec((1, HEAD_DIM)),
        _const_spec((1, HEAD_DIM)),
        _const_spec(pw.shape),
        _const_spec((1, d_pool)),
        side_spec,
    ]
    out_shape = [
        jax.ShapeDtypeStruct((n, d_pool), BF16),
        jax.ShapeDtypeStruct((n, d_att), BF16),
        jax.ShapeDtypeStruct((n, d_att), BF16),
        jax.ShapeDtypeStruct((n, d_att), BF16),
        jax.ShapeDtypeStruct((n, LANES), BF16),
        jax.ShapeDtypeStruct((META_PAD, d_att), BF16),
        jax.ShapeDtypeStruct((META_PAD, d_att), BF16),
        jax.ShapeDtypeStruct((META_PAD, LANES), BF16),
        jax.ShapeDtypeStruct(side.shape, BF16),
    ]
    out_specs = [
        rows(d_pool), rows(d_att), rows(d_att), rows(d_att), rows(LANES),
        pl.BlockSpec((META_PAD, d_att), lambda i: (0, 0)),
        pl.BlockSpec((META_PAD, d_att), lambda i: (0, 0)),
        pl.BlockSpec((META_PAD, LANES), lambda i: (0, 0)),
        side_spec,
    ]
    scratch = [
        pltpu.VMEM((tm, d), BF16),
        pltpu.VMEM((tm + N_META, d_pool), F32),
        pltpu.VMEM((N_META, d_pool), F32),
        pltpu.VMEM((1, LANES), F32),
        pltpu.VMEM((1, LANES), F32),
    ]
    kern = functools.partial(_inproj_kernel, tm=tm, tiles_a=tiles_a, tiles_per_batch=tiles_per_batch,
                             n_heads=n_heads, d_pool=d_pool, d_att=d_att, row_chunk=ROW_CHUNK)
    return pl.pallas_call(
        kern,
        grid=(n // tm,),
        in_specs=in_specs,
        out_specs=out_specs,
        out_shape=out_shape,
        scratch_shapes=scratch,
        compiler_params=_cparams(1),
        name="in_proj",
    )(h1a, h1b, h1m, gn, w_in, b_gate, qn, kn, pw, ps, side)


def _attn_kernel(q_ref, k_ref, v_ref, km_ref, vm_ref, aug_ref, augm_ref, wg_ref, wu_ref, wd_ref,
                 o_ref, wgb_ref, wub_ref, wdb_ref, qa_ref, ka_ref, va_ref, *, tq):
    h = pl.program_id(1)
    seq = q_ref.shape[0]
    lane = lax.broadcasted_iota(jnp.int32, (1, LANES), 1)
    group = lane >> (SUBLANES.bit_length() - 1)
    mine = (lane & (SUBLANES - 1)) == h
    one_hot = lambda lo: jnp.where(mine & (group >= lo) & (group < lo + 3), 1.0, 0.0).astype(BF16)
    keep = lambda lo: jnp.where((group >= lo) & (group < lo + 3), 0.0, 1.0).astype(BF16)
    qa_ref[:, 0:HEAD_DIM] = q_ref[...]
    qa_ref[:, HEAD_DIM:] = aug_ref[...] * keep(3) + one_hot(3)
    ka_ref[:, 0:HEAD_DIM] = k_ref[...]
    ka_ref[:, HEAD_DIM:] = aug_ref[...] * keep(0) + one_hot(0)
    kma = jnp.concatenate([km_ref[...], augm_ref[...] * keep(0) + one_hot(0)], axis=1)
    va_ref[:, 0:HEAD_DIM] = v_ref[...]
    va_ref[:, HEAD_DIM:] = jnp.ones((seq, HEAD_DIM), BF16)
    vma = jnp.concatenate([vm_ref[...], jnp.ones((META_PAD, HEAD_DIM), BF16)], axis=1)

    hq = tq // 2
    rows_of = lambda w: lax.broadcasted_iota(jnp.int32, (hq, w), 0)
    cols_of = lambda w: lax.broadcasted_iota(jnp.int32, (hq, w), 1)
    causal_top = cols_of(hq) <= rows_of(hq)
    causal_bot = cols_of(tq) <= rows_of(tq) + hq
    tiles = list(reversed(range(seq // tq)))
    scores = {}
    for qi in tiles:
        rows = slice(qi * tq, (qi + 1) * tq)
        q = qa_ref[rows, :]
        blocks = [slice(j * tq, (j + 1) * tq) for j in range(qi)]
        raw_m = jnp.where(lane < N_META, _dot_nt(q, kma), NEG_INF)
        raws = [_dot_nt(q, ka_ref[cols, :]) for cols in blocks]
        raw_t = jnp.where(causal_top, _dot_nt(q[0:hq], ka_ref[qi * tq:qi * tq + hq, :]), NEG_INF)
        raw_b = jnp.where(causal_bot, _dot_nt(q[hq:], ka_ref[rows, :]), NEG_INF)
        scores[qi] = (rows, blocks, raw_m, raws, raw_t, raw_b)
    probs = {}
    for qi in tiles:
        rows, blocks, raw_m, raws, raw_t, raw_b = scores[qi]
        halves = []
        for r0, raw_d in ((0, raw_t), (hq, raw_b)):
            part = slice(r0, r0 + hq)
            full = [raw_m[part]] + [r[part] for r in raws]
            m = jnp.maximum(jnp.max(full[0], axis=-1, keepdims=True),
                            jnp.max(raw_d, axis=-1, keepdims=True))
            if raws:
                m = jnp.maximum(m, jnp.max(functools.reduce(jnp.maximum, full[1:]),
                                           axis=-1, keepdims=True))
            halves.append([jnp.exp2(EXP2_SCALE * (r - m)) for r in full + [raw_d]])
        top, bot = halves
        p_m = jnp.concatenate([top[0], bot[0]], axis=0)
        ps = [jnp.concatenate([t, b], axis=0) for t, b in zip(top[1:-1], bot[1:-1])]
        probs[qi] = (rows, blocks, p_m, ps, top[-1], bot[-1])
    wgb_ref[...] = wg_ref[...].astype(BF16)
    wub_ref[...] = wu_ref[...].astype(BF16)
    wdb_ref[...] = (0.5 * wd_ref[...]).astype(BF16)
    for qi in tiles:
        rows, blocks, p_m, ps, p_t, p_b = probs[qi]
        acc = _dot(p_m.astype(BF16), vma)
        for p, cols in zip(ps, blocks):
            acc = acc + _dot(p.astype(BF16), va_ref[cols, :])
        acc = acc + jnp.concatenate(
            [_dot(p_t.astype(BF16), va_ref[qi * tq:qi * tq + hq, :]),
             _dot(p_b.astype(BF16), va_ref[rows, :])], axis=0)
        o_ref[rows, :] = (acc[:, 0:HEAD_DIM] / acc[:, HEAD_DIM:]).astype(BF16)


def _attention(q, k, v, km, vm, aug, augm, next_ffn, *, batch, tq=512):
    n, d_att = q.shape
    seq = n // batch
    n_heads = d_att // HEAD_DIM
    steps = batch * n_heads
    tok = pl.BlockSpec((seq, HEAD_DIM), lambda b, h: (b, h))
    meta = pl.BlockSpec((META_PAD, HEAD_DIM), lambda b, h: (0, h))

    def chunk(w):
        assert w.shape[0] % (steps * 2 * SUBLANES) == 0
        return pl.BlockSpec((w.shape[0] // steps, w.shape[1]), lambda b, h: (b * n_heads + h, 0))

    side_specs = [chunk(w) for w in next_ffn]
    return pl.pallas_call(
        functools.partial(_attn_kernel, tq=tq),
        grid=(batch, n_heads),
        in_specs=[
            tok, tok, tok, meta, meta,
            pl.BlockSpec((seq, LANES), lambda b, h: (b, 0)),
            pl.BlockSpec((META_PAD, LANES), lambda b, h: (0, 0)),
        ] + side_specs,
        out_specs=[tok] + side_specs,
        out_shape=[jax.ShapeDtypeStruct((n, d_att), BF16)]
        + [jax.ShapeDtypeStruct(w.shape, BF16) for w in next_ffn],
        scratch_shapes=[pltpu.VMEM((seq, 2 * HEAD_DIM), BF16),
                        pltpu.VMEM((seq, 2 * HEAD_DIM), BF16),
                        pltpu.VMEM((seq, 2 * HEAD_DIM), BF16)],
        compiler_params=_cparams(2),
        name="fox_attn",
    )(q, k, v, km, vm, aug, augm, *next_ffn)


def _outproj_kernel(ha_ref, hb_ref, pool_ref, att_ref, wop_ref, woa_ref, o_ref, *, tiles_a,
                    col_chunk):
    i = pl.program_id(0)
    d = o_ref.shape[1]
    for c in range(0, d, col_chunk):
        cols = slice(c, c + col_chunk)
        h = jnp.where(i < tiles_a, ha_ref[:, cols], hb_ref[:, cols])
        o_ref[:, cols] = (h + _dot(pool_ref[...], wop_ref[:, cols])
                          + _dot(att_ref[...], woa_ref[:, cols]))


def _out_proj(h1, pool, att, w_out, *, tm=512):
    h1a, h1b = h1
    d = h1a.shape[1]
    n = h1a.shape[0] + h1b.shape[0]
    assert h1a.shape[0] % tm == 0 and h1b.shape[0] % tm == 0
    tiles_a = h1a.shape[0] // tm
    d_pool, d_att = pool.shape[1], att.shape[1]
    assert d_pool == d_att
    rows = lambda w: pl.BlockSpec((tm, w), lambda i: (i, 0))
    return pl.pallas_call(
        functools.partial(_outproj_kernel, tiles_a=tiles_a, col_chunk=COL_CHUNK),
        grid=(n // tm,),
        in_specs=[pl.BlockSpec((tm, d), lambda i: (jnp.minimum(i, tiles_a - 1), 0)),
                  pl.BlockSpec((tm, d), lambda i: (jnp.maximum(i - tiles_a, 0), 0)),
                  rows(d_pool), rows(d_att),
                  _const_spec((d_pool, d), (0, 0)), _const_spec((d_att, d), (1, 0))],
        out_specs=rows(d),
        out_shape=jax.ShapeDtypeStruct((n, d), F32),
        compiler_params=_cparams(1),
        name="out_proj",
    )(h1a, h1b, pool, att, w_out, w_out)


def kernel(x, meta_tokens, ffn1_norm, ffn1_w_gate, ffn1_w_up, ffn1_w_down, mix_norm, w_in,
           b_forget, q_norm, k_norm, pool_w, pool_scale, w_out, ffn2_norm, ffn2_w_gate,
           ffn2_w_up, ffn2_w_down):
    batch, seq, d = x.shape
    n_heads = b_forget.shape[1]
    d_att = n_heads * HEAD_DIM
    d_pool = pool_scale.shape[1]
    assert meta_tokens.shape[0] == N_META
    assert ffn1_norm.shape[0] == 1, "one layer: the meta rows past the mixers never reach the output"

    hx = x.reshape(batch * seq, d)
    hm = meta_tokens.astype(x.dtype)
    g1 = ffn1_norm[0][None]
    h1_head, hm, w1_b = _ffn_head(hx, hm, g1, ffn1_w_gate[0], ffn1_w_up[0], ffn1_w_down[0])
    h1_tail, w_in_b = _ffn_tail(hx, g1, *w1_b, first_tile=1, in_place=False, side_t=w_in[0].T)
    h1 = (h1_head, h1_tail)

    b_gate = jnp.pad(b_forget[0][None], ((0, 0), (0, LANES - n_heads)))
    pool, q, k, v, aug, km, vm, augm, w_out_b = _in_proj(
        h1, hm, mix_norm[0][None], w_in_b, b_gate, q_norm[0][None], k_norm[0][None],
        pool_w[0].astype(BF16), pool_scale[0][None], w_out[0],
        batch=batch, d_pool=d_pool, d_att=d_att)

    att, *w2_b = _attention(q, k, v, km, vm, aug, augm,
                            (ffn2_w_gate[0], ffn2_w_up[0], ffn2_w_down[0]), batch=batch)
    h2 = _out_proj(h1, pool, att, w_out_b)

    out = _ffn_tail(h2, ffn2_norm[0][None], *w2_b, first_tile=0, in_place=True)
    return out.reshape(batch, seq, d)
```

```python
import functools

import jax
import jax.numpy as jnp
from jax import lax
from jax.experimental import pallas as pl
from jax.experimental.pallas import tpu as pltpu

F32 = jnp.float32
BF16 = jnp.bfloat16

EPS = 1e-6
N_META = 16
POOL_WINDOWS = (2, 4, 8, 16)
HEAD_DIM = 128
LANES = 128
SUBLANES = 8
META_PAD = 128
NEG_INF = float("-inf")
BIAS_SCALE = HEAD_DIM ** 0.5
EXP2_SCALE = 1.4426950408889634 / BIAS_SCALE

VMEM_LIMIT = 60 * 1024 * 1024
ROW_CHUNK = 256
COL_CHUNK = 512
PAIR = 256


def _cparams(n_axes):
    return pltpu.CompilerParams(
        dimension_semantics=("arbitrary",) * n_axes,
        vmem_limit_bytes=VMEM_LIMIT,
    )


def _rmsnorm_rows(x, gain):
    ms = jnp.mean(x * x, axis=-1, keepdims=True)
    return x * lax.rsqrt(ms + EPS) * gain


def _dot(a, b):
    return jnp.dot(a, b, preferred_element_type=F32)


def _dot_nt(a, b):
    return lax.dot_general(a, b, (((1,), (1,)), ((), ())), preferred_element_type=F32)


def _silu_mul(gate, up):
    return (gate / (1.0 + jnp.exp(-gate)) * up).astype(BF16)


def _accumulate_down(hmid, wd_cols, o_ref, oe_ref, *, tm, col_chunk, init_ref=None):
    d = o_ref.shape[1]
    for c in range(0, d, col_chunk):
        cols = slice(c, c + col_chunk)
        part = _dot(hmid, wd_cols(c, c + col_chunk))
        base = o_ref if init_ref is None else init_ref
        o_ref[:, cols] = base[:, cols] + part[:tm]
        if oe_ref is not None:
            oe_ref[:, cols] += part[tm:]


def _ffn_head_kernel(*refs, tm, n_extra, row_chunk, col_chunk):
    if n_extra:
        (x_ref, e_ref, g_ref, wg_ref, wu_ref, wd_ref,
         o_ref, oe_ref, wgub_ref, wdb_ref, xn_ref) = refs
    else:
        x_ref, g_ref, wg_ref, wu_ref, wd_ref, o_ref, wgub_ref, wdb_ref, xn_ref = refs
        e_ref = oe_ref = None
    f = pl.program_id(0)

    @pl.when(f == 0)
    def _():
        gain = g_ref[...]
        for r in range(0, tm, row_chunk):
            xn_ref[r:r + row_chunk, :] = _rmsnorm_rows(x_ref[r:r + row_chunk, :], gain).astype(BF16)
            o_ref[r:r + row_chunk, :] = x_ref[r:r + row_chunk, :]
        if n_extra:
            xn_ref[tm:tm + n_extra, :] = _rmsnorm_rows(e_ref[...], gain).astype(BF16)
            oe_ref[...] = e_ref[...]

    wg = wg_ref[...].astype(BF16)
    wu = wu_ref[...].astype(BF16)
    wd = (0.5 * wd_ref[...]).astype(BF16)
    wgub_ref[:, 0:PAIR] = wg
    wgub_ref[:, PAIR:] = wu
    wdb_ref[...] = wd
    xn = xn_ref[...]
    _accumulate_down(_silu_mul(_dot(xn, wg), _dot(xn, wu)), lambda c0, c1: wd[:, c0:c1],
                     o_ref, oe_ref, tm=tm, col_chunk=col_chunk)


def _ffn_tail_kernel(*refs, tm, side_rows, side_chunks, row_chunk, col_chunk):
    x_ref, g_ref, wgu_ref, wd_ref = refs[:4]
    xn_ref = refs[-1]
    i, f, nf = pl.program_id(0), pl.program_id(1), pl.num_programs(1)
    if side_rows:
        side_ref, o_ref, sideb_ref = refs[4], refs[5], refs[6]
    else:
        o_ref = refs[4]

    @pl.when(f == 0)
    def _():
        gain = g_ref[...]
        for r in range(0, tm, row_chunk):
            xn_ref[r:r + row_chunk, :] = _rmsnorm_rows(x_ref[r:r + row_chunk, :], gain).astype(BF16)

    def step(first):
        if side_rows:
            chunk = jnp.minimum(i * nf + f, side_chunks - 1)
            row = (chunk * side_ref.shape[0]
                   + lax.broadcasted_iota(jnp.int32, (side_ref.shape[0], 1), 0))
            sideb_ref[...] = jnp.where(row < side_rows, side_ref[...], 0.0).T.astype(BF16)
        z = _dot(xn_ref[...], wgu_ref[...])
        hmid = jnp.concatenate(
            [_silu_mul(z[:, c:c + PAIR], z[:, c + PAIR:c + 2 * PAIR])
             for c in range(0, z.shape[1], 2 * PAIR)], axis=1)
        _accumulate_down(hmid, lambda c0, c1: wd_ref[:, c0:c1], o_ref, None, tm=tm,
                         col_chunk=col_chunk, init_ref=x_ref if first else None)

    pl.when(f == 0)(functools.partial(step, True))
    pl.when(f > 0)(functools.partial(step, False))


def _ffn_head(x, extra, gain, wg, wu, wd, *, tm=1024, tf=PAIR):
    assert tf == PAIR
    n, d = x.shape
    dff = wg.shape[1]
    n_extra = 0 if extra is None else extra.shape[0]
    assert n % tm == 0 and dff % tf == 0
    in_specs = [pl.BlockSpec((tm, d), lambda f: (0, 0), pipeline_mode=pl.Buffered(1))]
    args = [x]
    if n_extra:
        in_specs.append(pl.BlockSpec((n_extra, d), lambda f: (0, 0)))
        args.append(extra)
    in_specs += [
        pl.BlockSpec((1, d), lambda f: (0, 0)),
        pl.BlockSpec((d, tf), lambda f: (0, f)),
        pl.BlockSpec((d, tf), lambda f: (0, f)),
        pl.BlockSpec((tf, d), lambda f: (f, 0)),
    ]
    args += [gain, wg, wu, wd]
    out_shape = [jax.ShapeDtypeStruct((tm, d), F32)]
    out_specs = [pl.BlockSpec((tm, d), lambda f: (0, 0))]
    if n_extra:
        out_shape.append(jax.ShapeDtypeStruct((n_extra, d), F32))
        out_specs.append(pl.BlockSpec((n_extra, d), lambda f: (0, 0)))
    out_shape += [jax.ShapeDtypeStruct((d, 2 * dff), BF16), jax.ShapeDtypeStruct(wd.shape, BF16)]
    out_specs += [pl.BlockSpec((d, 2 * tf), lambda f: (0, f)),
                  pl.BlockSpec((tf, d), lambda f: (f, 0))]
    head = pl.pallas_call(
        functools.partial(_ffn_head_kernel, tm=tm, n_extra=n_extra, row_chunk=ROW_CHUNK,
                          col_chunk=COL_CHUNK),
        grid=(dff // tf,),
        in_specs=in_specs,
        out_specs=out_specs,
        out_shape=out_shape,
        scratch_shapes=[pltpu.VMEM((tm + n_extra, d), BF16)],
        compiler_params=_cparams(1),
        name="ffn_head",
    )(*args)
    return head[0], (head[1] if n_extra else None), tuple(head[-2:])


def _ffn_tail(x, gain, wgub, wdb, *, first_tile, in_place, side_t=None, tm=1024, tf=512,
              side_chunk=LANES):
    n, d = x.shape
    dff = wdb.shape[0]
    assert n % tm == 0 and dff % tf == 0 and tf % PAIR == 0 and wgub.shape == (d, 2 * dff)
    n_tiles = n // tm - first_tile
    nf = dff // tf
    in_specs = [
        pl.BlockSpec((tm, d), lambda i, f: (i + first_tile, 0)),
        pl.BlockSpec((1, d), lambda i, f: (0, 0)),
        pl.BlockSpec((d, 2 * tf), lambda i, f: (0, f)),
        pl.BlockSpec((tf, d), lambda i, f: (f, 0)),
    ]
    args = [x, gain, wgub, wdb]
    out_shape = [jax.ShapeDtypeStruct((n if in_place else n_tiles * tm, d), F32)]
    out_specs = [pl.BlockSpec((tm, d), lambda i, f: (i + (first_tile if in_place else 0), 0))]
    n_chunks = 0
    if side_t is not None:
        cols, rows = side_t.shape
        n_chunks = pl.cdiv(cols, side_chunk)
        assert n_chunks <= n_tiles * nf
        chunk_of = lambda i, f: jnp.minimum(i * nf + f, n_chunks - 1)
        in_specs.append(pl.BlockSpec((side_chunk, rows), lambda i, f: (chunk_of(i, f), 0)))
        args.append(side_t)
        out_shape.append(jax.ShapeDtypeStruct((rows, n_chunks * side_chunk), BF16))
        out_specs.append(pl.BlockSpec((rows, side_chunk), lambda i, f: (0, chunk_of(i, f))))
    tail = pl.pallas_call(
        functools.partial(_ffn_tail_kernel, tm=tm,
                          side_rows=0 if side_t is None else side_t.shape[0],
                          side_chunks=n_chunks, row_chunk=ROW_CHUNK, col_chunk=COL_CHUNK),
        grid=(n_tiles, nf),
        in_specs=in_specs,
        out_specs=out_specs,
        out_shape=out_shape,
        scratch_shapes=[pltpu.VMEM((tm, d), BF16)],
        input_output_aliases={0: 0} if in_place else {},
        compiler_params=_cparams(2),
        name="ffn_tail",
    )(*args)
    return tail[0] if side_t is None else (tail[0], tail[1])


def _log_sigmoid(x):
    return jnp.minimum(x, 0.0) - jnp.log1p(jnp.exp(-jnp.abs(x)))


def _split3(x):
    hi = x.astype(BF16)
    r1 = x - hi.astype(F32)
    mid = r1.astype(BF16)
    lo = (r1 - mid.astype(F32)).astype(BF16)
    return hi, mid, lo


def _cumsum_rows(x, tri, head_lane):
    hi, mid, lo = (p.astype(F32) for p in _split3(x))
    packed = hi + pltpu.roll(mid, SUBLANES, axis=1) + pltpu.roll(lo, 2 * SUBLANES, axis=1)
    c = _dot(tri, packed.astype(BF16))
    c = c + pltpu.roll(c, LANES - SUBLANES, axis=1) + pltpu.roll(c, LANES - 2 * SUBLANES, axis=1)
    return jnp.where(head_lane, c, 0.0)


def _bias_columns(cum):
    hi, mid, lo = (p.astype(F32) for p in _split3(cum * BIAS_SCALE))
    roll = lambda x, g: pltpu.roll(x, g * SUBLANES, axis=1)
    cols = hi + roll(mid, 1) + roll(lo, 2) - roll(hi, 3) - roll(mid, 4) - roll(lo, 5)
    return cols.astype(BF16)


def _head_rmsnorm(z, gain, n_heads):
    outs = []
    for h in range(n_heads):
        outs.append(_rmsnorm_rows(z[:, h * HEAD_DIM:(h + 1) * HEAD_DIM], gain))
    return jnp.concatenate(outs, axis=-1)


def _window_means(e, w):
    assert w & (w - 1) == 0 and w <= N_META
    s, span = e, 1
    while span < w:
        s = s + pltpu.roll(s, span, axis=0)
        span *= 2
    return s[N_META:, :] / float(w)


def _inproj_kernel(ha_ref, hb_ref, hm_ref, gn_ref, w_ref, b_ref, qn_ref, kn_ref, pw_ref, ps_ref,
                   side_ref,
                   pool_ref, q_ref, k_ref, v_ref, aug_ref, km_ref, vm_ref, augm_ref, sideb_ref,
                   u_ref, pe_ref, pmeta_ref, ccol_ref, mcol_ref,
                   *, tm, tiles_a, tiles_per_batch, n_heads, d_pool, d_att, row_chunk):
    i = pl.program_id(0)
    gain = gn_ref[...]
    sideb_ref[...] = side_ref[...].astype(BF16)
    head_lane = lax.broadcasted_iota(jnp.int32, (1, LANES), 1) < n_heads
    w_pool = lambda: w_ref[:, 0:d_pool]
    w_q = lambda: w_ref[:, d_pool:d_pool + d_att]
    w_k = lambda: w_ref[:, d_pool + d_att:d_pool + 2 * d_att]
    w_vf = lambda: w_ref[:, d_pool + 2 * d_att:]

    def log_forget(zvf):
        lf = _log_sigmoid(zvf[:, d_att:] + b_ref[...])
        return jnp.where(head_lane, lf, 0.0)

    def tri(t):
        r_i = lax.broadcasted_iota(jnp.int32, (t, t), 0)
        c_i = lax.broadcasted_iota(jnp.int32, (t, t), 1)
        return (c_i <= r_i).astype(BF16)

    @pl.when(i == 0)
    def _():
        u_ref[0:META_PAD, :] = jnp.zeros((META_PAD, u_ref.shape[1]), BF16)
        u_ref[0:N_META, :] = _rmsnorm_rows(hm_ref[...], gain).astype(BF16)
        um = u_ref[0:META_PAD, :]
        row_valid = lax.broadcasted_iota(jnp.int32, (META_PAD, 1), 0) < N_META
        pmeta_ref[...] = _dot(um, w_pool())[0:N_META, :]
        km = _head_rmsnorm(_dot(um, w_k()), kn_ref[...], n_heads)
        km_ref[...] = jnp.where(row_valid, km, 0.0).astype(BF16)
        zvf = _dot(um, w_vf())
        vm_ref[...] = zvf[:, 0:d_att].astype(BF16)
        cum_m = _cumsum_rows(jnp.where(row_valid, log_forget(zvf), 0.0), tri(META_PAD), head_lane)
        augm_ref[...] = _bias_columns(cum_m)
        mcol_ref[...] = cum_m[META_PAD - 1:META_PAD, :]

    @pl.when(i % tiles_per_batch == 0)
    def _():
        ccol_ref[...] = mcol_ref[...]
        pe_ref[0:N_META, :] = pmeta_ref[...]

    for r in range(0, tm, row_chunk):
        h = jnp.where(i < tiles_a, ha_ref[r:r + row_chunk, :], hb_ref[r:r + row_chunk, :])
        u_ref[r:r + row_chunk, :] = _rmsnorm_rows(h, gain).astype(BF16)
    u = u_ref[0:tm, :]

    pe_ref[N_META:N_META + tm, :] = _dot(u, w_pool())
    q_ref[...] = _head_rmsnorm(_dot(u, w_q()), qn_ref[...], n_heads).astype(BF16)
    k_ref[...] = _head_rmsnorm(_dot(u, w_k()), kn_ref[...], n_heads).astype(BF16)
    zvf = _dot(u, w_vf())
    v_ref[...] = zvf[:, 0:d_att].astype(BF16)

    gw = d_pool // len(POOL_WINDOWS)
    for g, w in enumerate(POOL_WINDOWS):
        lanes = slice(g * gw, (g + 1) * gw)
        e = pe_ref[:, lanes]
        pooled = (_window_means(e, w) - e[N_META:, :]).astype(BF16)
        mixed = _dot(pooled, pw_ref[g]) * ps_ref[:, lanes]
        pool_ref[:, lanes] = mixed.astype(BF16)
    pe_ref[0:N_META, :] = pe_ref[tm:tm + N_META, :]

    cum_col = _cumsum_rows(log_forget(zvf), tri(tm), head_lane) + ccol_ref[...]
    aug_ref[...] = _bias_columns(cum_col)
    ccol_ref[...] = cum_col[tm - 1:tm, :]


def _const_spec(shape, index=None):
    nd = len(shape)
    index = (0,) * nd if index is None else index
    return pl.BlockSpec(shape, lambda i: index, pipeline_mode=pl.Buffered(1))


def _in_proj(h1, h1m, gn, w_in, b_gate, qn, kn, pw, ps, side, *, batch, d_pool, d_att, tm=512):
    h1a, h1b = h1
    d = h1a.shape[1]
    n = h1a.shape[0] + h1b.shape[0]
    assert h1a.shape[0] % tm == 0 and h1b.shape[0] % tm == 0
    tiles_a = h1a.shape[0] // tm
    seq = n // batch
    n_heads = d_att // HEAD_DIM
    assert seq % tm == 0 and tm >= META_PAD and d_pool == d_att and n_heads <= SUBLANES
    assert w_in.shape == (d, d_pool + 3 * d_att + LANES)
    tiles_per_batch = seq // tm
    side_rows = side.shape[0] // (n // tm)
    assert side_rows * (n // tm) == side.shape[0] and side_rows % SUBLANES == 0
    side_spec = pl.BlockSpec((side_rows, side.shape[1]), lambda i: (i, 0))
    rows = lambda w: pl.BlockSpec((tm, w), lambda i: (i, 0))
    in_specs = [
        pl.BlockSpec((tm, d), lambda i: (jnp.minimum(i, tiles_a - 1), 0)),
        pl.BlockSpec((tm, d), lambda i: (jnp.maximum(i - tiles_a, 0), 0)),
        _const_spec((N_META, d)),
        _const_spec((1, d)),
        _const_spec(w_in.shape),
        _const_spec((1, LANES)),
        _const_spec((1, HEAD_DIM)),
        _const_spec((1, HEAD_DIM)),
        _const_spec(pw.shape),
        _const_spec((1, d_pool)),
        side_spec,
    ]
    out_shape = [
        jax.ShapeDtypeStruct((n, d_pool), BF16),
        jax.ShapeDtypeStruct((n, d_att), BF16),
        jax.ShapeDtypeStruct((n, d_att), BF16),
        jax.ShapeDtypeStruct((n, d_att), BF16),
        jax.ShapeDtypeStruct((n, LANES), BF16),
        jax.ShapeDtypeStruct((META_PAD, d_att), BF16),
        jax.ShapeDtypeStruct((META_PAD, d_att), BF16),
        jax.ShapeDtypeStruct((META_PAD, LANES), BF16),
        jax.ShapeDtypeStruct(side.shape, BF16),
    ]
    out_specs = [
        rows(d_pool), rows(d_att), rows(d_att), rows(d_att), rows(LANES),
        pl.BlockSpec((META_PAD, d_att), lambda i: (0, 0)),
        pl.BlockSpec((META_PAD, d_att), lambda i: (0, 0)),
        pl.BlockSpec((META_PAD, LANES), lambda i: (0, 0)),
        side_spec,
    ]
    scratch = [
        pltpu.VMEM((tm, d), BF16),
        pltpu.VMEM((tm + N_META, d_pool), F32),
        pltpu.VMEM((N_META, d_pool), F32),
        pltpu.VMEM((1, LANES), F32),
        pltpu.VMEM((1, LANES), F32),
    ]
    kern = functools.partial(_inproj_kernel, tm=tm, tiles_a=tiles_a, tiles_per_batch=tiles_per_batch,
                             n_heads=n_heads, d_pool=d_pool, d_att=d_att, row_chunk=ROW_CHUNK)
    return pl.pallas_call(
        kern,
        grid=(n // tm,),
        in_specs=in_specs,
        out_specs=out_specs,
        out_shape=out_shape,
        scratch_shapes=scratch,
        compiler_params=_cparams(1),
        name="in_proj",
    )(h1a, h1b, h1m, gn, w_in, b_gate, qn, kn, pw, ps, side)


def _attn_kernel(q_ref, k_ref, v_ref, km_ref, vm_ref, aug_ref, augm_ref, wg_ref, wu_ref, wd_ref,
                 o_ref, wgub_ref, wdb_ref, qa_ref, ka_ref, va_ref, *, tq):
    h = pl.program_id(1)
    seq = q_ref.shape[0]
    lane = lax.broadcasted_iota(jnp.int32, (1, LANES), 1)
    group = lane >> (SUBLANES.bit_length() - 1)
    mine = (lane & (SUBLANES - 1)) == h
    one_hot = lambda lo: jnp.where(mine & (group >= lo) & (group < lo + 3), 1.0, 0.0).astype(BF16)
    keep = lambda lo: jnp.where((group >= lo) & (group < lo + 3), 0.0, 1.0).astype(BF16)
    qa_ref[:, 0:HEAD_DIM] = q_ref[...]
    qa_ref[:, HEAD_DIM:] = aug_ref[...] * keep(3) + one_hot(3)
    ka_ref[:, 0:HEAD_DIM] = k_ref[...]
    ka_ref[:, HEAD_DIM:] = aug_ref[...] * keep(0) + one_hot(0)
    kma = jnp.concatenate([km_ref[...], augm_ref[...] * keep(0) + one_hot(0)], axis=1)
    va_ref[:, 0:HEAD_DIM] = v_ref[...]
    va_ref[:, HEAD_DIM:] = jnp.ones((seq, HEAD_DIM), BF16)
    vma = jnp.concatenate([vm_ref[...], jnp.ones((META_PAD, HEAD_DIM), BF16)], axis=1)

    hq = tq // 2
    rows_of = lambda w: lax.broadcasted_iota(jnp.int32, (hq, w), 0)
    cols_of = lambda w: lax.broadcasted_iota(jnp.int32, (hq, w), 1)
    causal_top = cols_of(hq) <= rows_of(hq)
    causal_bot = cols_of(tq) <= rows_of(tq) + hq
    tiles = list(reversed(range(seq // tq)))
    scores = {}
    for qi in tiles:
        rows = slice(qi * tq, (qi + 1) * tq)
        q = qa_ref[rows, :]
        blocks = [slice(j * tq, (j + 1) * tq) for j in range(qi)]
        raw_m = jnp.where(lane < N_META, _dot_nt(q, kma), NEG_INF)
        raws = [_dot_nt(q, ka_ref[cols, :]) for cols in blocks]
        raw_t = jnp.where(causal_top, _dot_nt(q[0:hq], ka_ref[qi * tq:qi * tq + hq, :]), NEG_INF)
        raw_b = jnp.where(causal_bot, _dot_nt(q[hq:], ka_ref[rows, :]), NEG_INF)
        scores[qi] = (rows, blocks, raw_m, raws, raw_t, raw_b)
    probs = {}
    for qi in tiles:
        rows, blocks, raw_m, raws, raw_t, raw_b = scores[qi]
        halves = []
        for r0, raw_d in ((0, raw_t), (hq, raw_b)):
            part = slice(r0, r0 + hq)
            full = [raw_m[part]] + [r[part] for r in raws]
            m = jnp.maximum(jnp.max(full[0], axis=-1, keepdims=True),
                            jnp.max(raw_d, axis=-1, keepdims=True))
            if raws:
                m = jnp.maximum(m, jnp.max(functools.reduce(jnp.maximum, full[1:]),
                                           axis=-1, keepdims=True))
            halves.append([jnp.exp2(EXP2_SCALE * (r - m)) for r in full + [raw_d]])
        top, bot = halves
        p_m = jnp.concatenate([top[0], bot[0]], axis=0)
        ps = [jnp.concatenate([t, b], axis=0) for t, b in zip(top[1:-1], bot[1:-1])]
        probs[qi] = (rows, blocks, p_m, ps, top[-1], bot[-1])
    for c in range(0, wg_ref.shape[1], PAIR):
        wgub_ref[:, 2 * c:2 * c + PAIR] = wg_ref[:, c:c + PAIR].astype(BF16)
        wgub_ref[:, 2 * c + PAIR:2 * c + 2 * PAIR] = wu_ref[:, c:c + PAIR].astype(BF16)
    wdb_ref[...] = (0.5 * wd_ref[...]).astype(BF16)
    for qi in tiles:
        rows, blocks, p_m, ps, p_t, p_b = probs[qi]
        acc = _dot(p_m.astype(BF16), vma)
        for p, cols in zip(ps, blocks):
            acc = acc + _dot(p.astype(BF16), va_ref[cols, :])
        acc = acc + jnp.concatenate(
            [_dot(p_t.astype(BF16), va_ref[qi * tq:qi * tq + hq, :]),
             _dot(p_b.astype(BF16), va_ref[rows, :])], axis=0)
        o_ref[rows, :] = (acc[:, 0:HEAD_DIM] / acc[:, HEAD_DIM:]).astype(BF16)


def _attention(q, k, v, km, vm, aug, augm, next_ffn, *, batch, tq=512):
    n, d_att = q.shape
    seq = n // batch
    n_heads = d_att // HEAD_DIM
    steps = batch * n_heads
    tok = pl.BlockSpec((seq, HEAD_DIM), lambda b, h: (b, h))
    meta = pl.BlockSpec((META_PAD, HEAD_DIM), lambda b, h: (0, h))

    def chunk(w):
        assert w.shape[0] % (steps * 2 * SUBLANES) == 0
        return pl.BlockSpec((w.shape[0] // steps, w.shape[1]), lambda b, h: (b * n_heads + h, 0))

    wg, wu, wd = next_ffn
    side_in = [chunk(w) for w in next_ffn]
    side_out = [pl.BlockSpec((wg.shape[0] // steps, 2 * wg.shape[1]),
                             lambda b, h: (b * n_heads + h, 0)), chunk(wd)]
    return pl.pallas_call(
        functools.partial(_attn_kernel, tq=tq),
        grid=(batch, n_heads),
        in_specs=[
            tok, tok, tok, meta, meta,
            pl.BlockSpec((seq, LANES), lambda b, h: (b, 0)),
            pl.BlockSpec((META_PAD, LANES), lambda b, h: (0, 0)),
        ] + side_in,
        out_specs=[tok] + side_out,
        out_shape=[jax.ShapeDtypeStruct((n, d_att), BF16),
                   jax.ShapeDtypeStruct((wg.shape[0], 2 * wg.shape[1]), BF16),
                   jax.ShapeDtypeStruct(wd.shape, BF16)],
        scratch_shapes=[pltpu.VMEM((seq, 2 * HEAD_DIM), BF16),
                        pltpu.VMEM((seq, 2 * HEAD_DIM), BF16),
                        pltpu.VMEM((seq, 2 * HEAD_DIM), BF16)],
        compiler_params=_cparams(2),
        name="fox_attn",
    )(q, k, v, km, vm, aug, augm, *next_ffn)


def _outproj_kernel(ha_ref, hb_ref, pool_ref, att_ref, wop_ref, woa_ref, o_ref, *, tiles_a,
                    col_chunk):
    i = pl.program_id(0)
    d = o_ref.shape[1]
    for c in range(0, d, col_chunk):
        cols = slice(c, c + col_chunk)
        h = jnp.where(i < tiles_a, ha_ref[:, cols], hb_ref[:, cols])
        o_ref[:, cols] = (h + _dot(pool_ref[...], wop_ref[:, cols])
                          + _dot(att_ref[...], woa_ref[:, cols]))


def _out_proj(h1, pool, att, w_out, *, tm=512):
    h1a, h1b = h1
    d = h1a.shape[1]
    n = h1a.shape[0] + h1b.shape[0]
    assert h1a.shape[0] % tm == 0 and h1b.shape[0] % tm == 0
    tiles_a = h1a.shape[0] // tm
    d_pool, d_att = pool.shape[1], att.shape[1]
    assert d_pool == d_att
    rows = lambda w: pl.BlockSpec((tm, w), lambda i: (i, 0))
    return pl.pallas_call(
        functools.partial(_outproj_kernel, tiles_a=tiles_a, col_chunk=COL_CHUNK),
        grid=(n // tm,),
        in_specs=[pl.BlockSpec((tm, d), lambda i: (jnp.minimum(i, tiles_a - 1), 0)),
                  pl.BlockSpec((tm, d), lambda i: (jnp.maximum(i - tiles_a, 0), 0)),
                  rows(d_pool), rows(d_att),
                  _const_spec((d_pool, d), (0, 0)), _const_spec((d_att, d), (1, 0))],
        out_specs=rows(d),
        out_shape=jax.ShapeDtypeStruct((n, d), F32),
        compiler_params=_cparams(1),
        name="out_proj",
    )(h1a, h1b, pool, att, w_out, w_out)


def kernel(x, meta_tokens, ffn1_norm, ffn1_w_gate, ffn1_w_up, ffn1_w_down, mix_norm, w_in,
           b_forget, q_norm, k_norm, pool_w, pool_scale, w_out, ffn2_norm, ffn2_w_gate,
           ffn2_w_up, ffn2_w_down):
    batch, seq, d = x.shape
    n_heads = b_forget.shape[1]
    d_att = n_heads * HEAD_DIM
    d_pool = pool_scale.shape[1]
    assert meta_tokens.shape[0] == N_META
    assert ffn1_norm.shape[0] == 1, "one layer: the meta rows past the mixers never reach the output"

    hx = x.reshape(batch * seq, d)
    hm = meta_tokens.astype(x.dtype)
    g1 = ffn1_norm[0][None]
    h1_head, hm, w1_b = _ffn_head(hx, hm, g1, ffn1_w_gate[0], ffn1_w_up[0], ffn1_w_down[0])
    h1_tail, w_in_b = _ffn_tail(hx, g1, *w1_b, first_tile=1, in_place=False, side_t=w_in[0].T)
    h1 = (h1_head, h1_tail)

    b_gate = jnp.pad(b_forget[0][None], ((0, 0), (0, LANES - n_heads)))
    pool, q, k, v, aug, km, vm, augm, w_out_b = _in_proj(
        h1, hm, mix_norm[0][None], w_in_b, b_gate, q_norm[0][None], k_norm[0][None],
        pool_w[0].astype(BF16), pool_scale[0][None], w_out[0],
        batch=batch, d_pool=d_pool, d_att=d_att)

    att, *w2_b = _attention(q, k, v, km, vm, aug, augm,
                            (ffn2_w_gate[0], ffn2_w_up[0], ffn2_w_down[0]), batch=batch)
    h2 = _out_proj(h1, pool, att, w_out_b)

    out = _ffn_tail(h2, ffn2_norm[0][None], *w2_b, first_tile=0, in_place=True)
    return out.reshape(batch, seq, d)
```

```python
import functools

import jax
import jax.numpy as jnp
from jax import lax
from jax.experimental import pallas as pl
from jax.experimental.pallas import tpu as pltpu

F32 = jnp.float32
BF16 = jnp.bfloat16

EPS = 1e-6
N_META = 16
POOL_WINDOWS = (2, 4, 8, 16)
HEAD_DIM = 128
LANES = 128
SUBLANES = 8
META_PAD = 128
NEG_INF = float("-inf")
BIAS_SCALE = HEAD_DIM ** 0.5
EXP2_SCALE = 1.4426950408889634 / BIAS_SCALE

VMEM_LIMIT = 60 * 1024 * 1024
ROW_CHUNK = 256
COL_CHUNK = 512
PAIR = 256


def _cparams(n_axes):
    return pltpu.CompilerParams(
        dimension_semantics=("arbitrary",) * n_axes,
        vmem_limit_bytes=VMEM_LIMIT,
    )


def _rmsnorm_rows(x, gain):
    ms = jnp.mean(x * x, axis=-1, keepdims=True)
    return x * lax.rsqrt(ms + EPS) * gain


def _dot(a, b):
    return jnp.dot(a, b, preferred_element_type=F32)


def _dot_nt(a, b):
    return lax.dot_general(a, b, (((1,), (1,)), ((), ())), preferred_element_type=F32)


def _silu_mul(gate, up):
    return (gate / (1.0 + jnp.exp(-gate)) * up).astype(BF16)


def _accumulate_down(hmid, wd_cols, o_ref, oe_ref, *, tm, col_chunk, init_ref=None):
    d = o_ref.shape[1]
    for c in range(0, d, col_chunk):
        cols = slice(c, c + col_chunk)
        part = _dot(hmid, wd_cols(c, c + col_chunk))
        base = o_ref if init_ref is None else init_ref
        o_ref[:, cols] = base[:, cols] + part[:tm]
        if oe_ref is not None:
            oe_ref[:, cols] += part[tm:]


def _ffn_head_kernel(*refs, tm, n_extra, row_chunk, col_chunk):
    if n_extra:
        (x_ref, e_ref, g_ref, wg_ref, wu_ref, wd_ref,
         o_ref, oe_ref, wgub_ref, wdb_ref, xn_ref) = refs
    else:
        x_ref, g_ref, wg_ref, wu_ref, wd_ref, o_ref, wgub_ref, wdb_ref, xn_ref = refs
        e_ref = oe_ref = None
    f = pl.program_id(0)

    @pl.when(f == 0)
    def _():
        gain = g_ref[...]
        for r in range(0, tm, row_chunk):
            xn_ref[r:r + row_chunk, :] = _rmsnorm_rows(x_ref[r:r + row_chunk, :], gain).astype(BF16)
            o_ref[r:r + row_chunk, :] = x_ref[r:r + row_chunk, :]
        if n_extra:
            xn_ref[tm:tm + n_extra, :] = _rmsnorm_rows(e_ref[...], gain).astype(BF16)
            oe_ref[...] = e_ref[...]

    wg = wg_ref[...].astype(BF16)
    wu = wu_ref[...].astype(BF16)
    wd = (0.5 * wd_ref[...]).astype(BF16)
    wgub_ref[:, 0:PAIR] = wg
    wgub_ref[:, PAIR:] = wu
    wdb_ref[...] = wd
    xn = xn_ref[...]
    _accumulate_down(_silu_mul(_dot(xn, wg), _dot(xn, wu)), lambda c0, c1: wd[:, c0:c1],
                     o_ref, oe_ref, tm=tm, col_chunk=col_chunk)


def _ffn_tail_kernel(*refs, tm, side_rows, side_chunks, row_chunk, col_chunk):
    x_ref, g_ref, wgu_ref, wd_ref = refs[:4]
    xn_ref = refs[-1]
    i, f, nf = pl.program_id(0), pl.program_id(1), pl.num_programs(1)
    if side_rows:
        side_ref, o_ref, sideb_ref = refs[4], refs[5], refs[6]
    else:
        o_ref = refs[4]

    @pl.when(f == 0)
    def _():
        gain = g_ref[...]
        for r in range(0, tm, row_chunk):
            xn_ref[r:r + row_chunk, :] = _rmsnorm_rows(x_ref[r:r + row_chunk, :], gain).astype(BF16)

    def step(first):
        if side_rows:
            chunk = jnp.minimum(i * nf + f, side_chunks - 1)
            row = (chunk * side_ref.shape[0]
                   + lax.broadcasted_iota(jnp.int32, (side_ref.shape[0], 1), 0))
            sideb_ref[...] = jnp.where(row < side_rows, side_ref[...], 0.0).T.astype(BF16)
        z = _dot(xn_ref[...], wgu_ref[...])
        hmid = jnp.concatenate(
            [_silu_mul(z[:, c:c + PAIR], z[:, c + PAIR:c + 2 * PAIR])
             for c in range(0, z.shape[1], 2 * PAIR)], axis=1)
        _accumulate_down(hmid, lambda c0, c1: wd_ref[:, c0:c1], o_ref, None, tm=tm,
                         col_chunk=col_chunk, init_ref=x_ref if first else None)

    pl.when(f == 0)(functools.partial(step, True))
    pl.when(f > 0)(functools.partial(step, False))


def _ffn_head(x, extra, gain, wg, wu, wd, *, tm=1024, tf=PAIR):
    assert tf == PAIR
    n, d = x.shape
    dff = wg.shape[1]
    n_extra = 0 if extra is None else extra.shape[0]
    assert n % tm == 0 and dff % tf == 0
    in_specs = [pl.BlockSpec((tm, d), lambda f: (0, 0), pipeline_mode=pl.Buffered(1))]
    args = [x]
    if n_extra:
        in_specs.append(pl.BlockSpec((n_extra, d), lambda f: (0, 0)))
        args.append(extra)
    in_specs += [
        pl.BlockSpec((1, d), lambda f: (0, 0)),
        pl.BlockSpec((d, tf), lambda f: (0, f)),
        pl.BlockSpec((d, tf), lambda f: (0, f)),
        pl.BlockSpec((tf, d), lambda f: (f, 0)),
    ]
    args += [gain, wg, wu, wd]
    out_shape = [jax.ShapeDtypeStruct((tm, d), F32)]
    out_specs = [pl.BlockSpec((tm, d), lambda f: (0, 0))]
    if n_extra:
        out_shape.append(jax.ShapeDtypeStruct((n_extra, d), F32))
        out_specs.append(pl.BlockSpec((n_extra, d), lambda f: (0, 0)))
    out_shape += [jax.ShapeDtypeStruct((d, 2 * dff), BF16), jax.ShapeDtypeStruct(wd.shape, BF16)]
    out_specs += [pl.BlockSpec((d, 2 * tf), lambda f: (0, f)),
                  pl.BlockSpec((tf, d), lambda f: (f, 0))]
    head = pl.pallas_call(
        functools.partial(_ffn_head_kernel, tm=tm, n_extra=n_extra, row_chunk=ROW_CHUNK,
                          col_chunk=COL_CHUNK),
        grid=(dff // tf,),
        in_specs=in_specs,
        out_specs=out_specs,
        out_shape=out_shape,
        scratch_shapes=[pltpu.VMEM((tm + n_extra, d), BF16)],
        compiler_params=_cparams(1),
        name="ffn_head",
    )(*args)
    return head[0], (head[1] if n_extra else None), tuple(head[-2:])


def _ffn_tail(x, gain, wgub, wdb, *, first_tile, in_place, side_t=None, nested=False, tm=1024,
              tf=512, side_chunk=LANES):
    n, d = x.shape
    dff = wdb.shape[0]
    assert n % tm == 0 and dff % tf == 0 and tf % PAIR == 0 and wgub.shape == (d, 2 * dff)
    n_tiles = n // tm - first_tile
    nf = dff // tf
    in_specs = [
        pl.BlockSpec((tm, d), lambda i, f: (i + first_tile, 0)),
        pl.BlockSpec((1, d), lambda i, f: (0, 0)),
        pl.BlockSpec((d, 2 * tf), lambda i, f: (0, f)),
        pl.BlockSpec((tf, d), lambda i, f: (f, 0)),
    ]
    args = [x, gain, wgub, wdb]
    out_shape = [jax.ShapeDtypeStruct((n if in_place else n_tiles * tm, d), F32)]
    out_specs = [pl.BlockSpec((tm, d), lambda i, f: (i + (first_tile if in_place else 0), 0))]
    n_chunks = 0
    if side_t is not None:
        cols, rows = side_t.shape
        n_chunks = pl.cdiv(cols, side_chunk)
        assert n_chunks <= n_tiles * nf
        chunk_of = lambda i, f: jnp.minimum(i * nf + f, n_chunks - 1)
        in_specs.append(pl.BlockSpec((side_chunk, rows), lambda i, f: (chunk_of(i, f), 0)))
        args.append(side_t)
        out_shape.append(jax.ShapeDtypeStruct((rows, n_chunks * side_chunk), BF16))
        out_specs.append(pl.BlockSpec((rows, side_chunk), lambda i, f: (0, chunk_of(i, f))))
    step = functools.partial(_ffn_tail_kernel, tm=tm,
                             side_rows=0 if side_t is None else side_t.shape[0],
                             side_chunks=n_chunks, row_chunk=ROW_CHUNK, col_chunk=COL_CHUNK)
    if nested:
        def whole(*refs):
            pltpu.emit_pipeline(step, grid=(n_tiles, nf), in_specs=in_specs, out_specs=out_specs)(
                *refs[:-1], scratches=(refs[-1],))

        anywhere = pl.BlockSpec(memory_space=pl.ANY)
        tail = pl.pallas_call(
            whole,
            in_specs=[anywhere] * len(in_specs),
            out_specs=[anywhere] * len(out_specs),
            out_shape=out_shape,
            scratch_shapes=[pltpu.VMEM((tm, d), BF16)],
            input_output_aliases={0: 0} if in_place else {},
            compiler_params=pltpu.CompilerParams(vmem_limit_bytes=VMEM_LIMIT),
            name="ffn_tail",
        )(*args)
        return tail[0] if side_t is None else (tail[0], tail[1])
    tail = pl.pallas_call(
        step,
        grid=(n_tiles, nf),
        in_specs=in_specs,
        out_specs=out_specs,
        out_shape=out_shape,
        scratch_shapes=[pltpu.VMEM((tm, d), BF16)],
        input_output_aliases={0: 0} if in_place else {},
        compiler_params=_cparams(2),
        name="ffn_tail",
    )(*args)
    return tail[0] if side_t is None else (tail[0], tail[1])


def _log_sigmoid(x):
    return jnp.minimum(x, 0.0) - jnp.log1p(jnp.exp(-jnp.abs(x)))


def _split3(x):
    hi = x.astype(BF16)
    r1 = x - hi.astype(F32)
    mid = r1.astype(BF16)
    lo = (r1 - mid.astype(F32)).astype(BF16)
    return hi, mid, lo


def _cumsum_rows(x, tri, head_lane):
    hi, mid, lo = (p.astype(F32) for p in _split3(x))
    packed = hi + pltpu.roll(mid, SUBLANES, axis=1) + pltpu.roll(lo, 2 * SUBLANES, axis=1)
    c = _dot(tri, packed.astype(BF16))
    c = c + pltpu.roll(c, LANES - SUBLANES, axis=1) + pltpu.roll(c, LANES - 2 * SUBLANES, axis=1)
    return jnp.where(head_lane, c, 0.0)


def _bias_columns(cum):
    hi, mid, lo = (p.astype(F32) for p in _split3(cum * BIAS_SCALE))
    roll = lambda x, g: pltpu.roll(x, g * SUBLANES, axis=1)
    cols = hi + roll(mid, 1) + roll(lo, 2) - roll(hi, 3) - roll(mid, 4) - roll(lo, 5)
    return cols.astype(BF16)


def _head_rmsnorm(z, gain, n_heads):
    outs = []
    for h in range(n_heads):
        outs.append(_rmsnorm_rows(z[:, h * HEAD_DIM:(h + 1) * HEAD_DIM], gain))
    return jnp.concatenate(outs, axis=-1)


def _window_means(e, w):
    assert w & (w - 1) == 0 and w <= N_META
    s, span = e, 1
    while span < w:
        s = s + pltpu.roll(s, span, axis=0)
        span *= 2
    return s[N_META:, :] / float(w)


def _inproj_kernel(ha_ref, hb_ref, hm_ref, gn_ref, w_ref, b_ref, qn_ref, kn_ref, pw_ref, ps_ref,
                   side_ref,
                   pool_ref, q_ref, k_ref, v_ref, aug_ref, km_ref, vm_ref, augm_ref, sideb_ref,
                   u_ref, pe_ref, pmeta_ref, ccol_ref, mcol_ref,
                   *, tm, tiles_a, tiles_per_batch, n_heads, d_pool, d_att, row_chunk):
    i = pl.program_id(0)
    gain = gn_ref[...]
    sideb_ref[...] = side_ref[...].astype(BF16)
    head_lane = lax.broadcasted_iota(jnp.int32, (1, LANES), 1) < n_heads
    w_pool = lambda: w_ref[:, 0:d_pool]
    w_q = lambda: w_ref[:, d_pool:d_pool + d_att]
    w_k = lambda: w_ref[:, d_pool + d_att:d_pool + 2 * d_att]
    w_vf = lambda: w_ref[:, d_pool + 2 * d_att:]

    def log_forget(zvf):
        lf = _log_sigmoid(zvf[:, d_att:] + b_ref[...])
        return jnp.where(head_lane, lf, 0.0)

    def tri(t):
        r_i = lax.broadcasted_iota(jnp.int32, (t, t), 0)
        c_i = lax.broadcasted_iota(jnp.int32, (t, t), 1)
        return (c_i <= r_i).astype(BF16)

    @pl.when(i == 0)
    def _():
        u_ref[0:META_PAD, :] = jnp.zeros((META_PAD, u_ref.shape[1]), BF16)
        u_ref[0:N_META, :] = _rmsnorm_rows(hm_ref[...], gain).astype(BF16)
        um = u_ref[0:META_PAD, :]
        row_valid = lax.broadcasted_iota(jnp.int32, (META_PAD, 1), 0) < N_META
        pmeta_ref[...] = _dot(um, w_pool())[0:N_META, :]
        km = _head_rmsnorm(_dot(um, w_k()), kn_ref[...], n_heads)
        km_ref[...] = jnp.where(row_valid, km, 0.0).astype(BF16)
        zvf = _dot(um, w_vf())
        vm_ref[...] = zvf[:, 0:d_att].astype(BF16)
        cum_m = _cumsum_rows(jnp.where(row_valid, log_forget(zvf), 0.0), tri(META_PAD), head_lane)
        augm_ref[...] = _bias_columns(cum_m)
        mcol_ref[...] = cum_m[META_PAD - 1:META_PAD, :]

    @pl.when(i % tiles_per_batch == 0)
    def _():
        ccol_ref[...] = mcol_ref[...]
        pe_ref[0:N_META, :] = pmeta_ref[...]

    for r in range(0, tm, row_chunk):
        h = jnp.where(i < tiles_a, ha_ref[r:r + row_chunk, :], hb_ref[r:r + row_chunk, :])
        u_ref[r:r + row_chunk, :] = _rmsnorm_rows(h, gain).astype(BF16)
    u = u_ref[0:tm, :]

    pe_ref[N_META:N_META + tm, :] = _dot(u, w_pool())
    q_ref[...] = _head_rmsnorm(_dot(u, w_q()), qn_ref[...], n_heads).astype(BF16)
    k_ref[...] = _head_rmsnorm(_dot(u, w_k()), kn_ref[...], n_heads).astype(BF16)
    zvf = _dot(u, w_vf())
    v_ref[...] = zvf[:, 0:d_att].astype(BF16)

    gw = d_pool // len(POOL_WINDOWS)
    for g, w in enumerate(POOL_WINDOWS):
        lanes = slice(g * gw, (g + 1) * gw)
        e = pe_ref[:, lanes]
        pooled = (_window_means(e, w) - e[N_META:, :]).astype(BF16)
        mixed = _dot(pooled, pw_ref[g]) * ps_ref[:, lanes]
        pool_ref[:, lanes] = mixed.astype(BF16)
    pe_ref[0:N_META, :] = pe_ref[tm:tm + N_META, :]

    cum_col = _cumsum_rows(log_forget(zvf), tri(tm), head_lane) + ccol_ref[...]
    aug_ref[...] = _bias_columns(cum_col)
    ccol_ref[...] = cum_col[tm - 1:tm, :]


def _const_spec(shape, index=None):
    nd = len(shape)
    index = (0,) * nd if index is None else index
    return pl.BlockSpec(shape, lambda i: index, pipeline_mode=pl.Buffered(1))


def _in_proj(h1, h1m, gn, w_in, b_gate, qn, kn, pw, ps, side, *, batch, d_pool, d_att, tm=512):
    h1a, h1b = h1
    d = h1a.shape[1]
    n = h1a.shape[0] + h1b.shape[0]
    assert h1a.shape[0] % tm == 0 and h1b.shape[0] % tm == 0
    tiles_a = h1a.shape[0] // tm
    seq = n // batch
    n_heads = d_att // HEAD_DIM
    assert seq % tm == 0 and tm >= META_PAD and d_pool == d_att and n_heads <= SUBLANES
    assert w_in.shape == (d, d_pool + 3 * d_att + LANES)
    tiles_per_batch = seq // tm
    side_rows = side.shape[0] // (n // tm)
    assert side_rows * (n // tm) == side.shape[0] and side_rows % SUBLANES == 0
    side_spec = pl.BlockSpec((side_rows, side.shape[1]), lambda i: (i, 0))
    rows = lambda w: pl.BlockSpec((tm, w), lambda i: (i, 0))
    in_specs = [
        pl.BlockSpec((tm, d), lambda i: (jnp.minimum(i, tiles_a - 1), 0)),
        pl.BlockSpec((tm, d), lambda i: (jnp.maximum(i - tiles_a, 0), 0)),
        _const_spec((N_META, d)),
        _const_spec((1, d)),
        _const_spec(w_in.shape),
        _const_spec((1, LANES)),
        _const_spec((1, HEAD_DIM)),
        _const_spec((1, HEAD_DIM)),
        _const_spec(pw.shape),
        _const_spec((1, d_pool)),
        side_spec,
    ]
    out_shape = [
        jax.ShapeDtypeStruct((n, d_pool), BF16),
        jax.ShapeDtypeStruct((n, d_att), BF16),
        jax.ShapeDtypeStruct((n, d_att), BF16),
        jax.ShapeDtypeStruct((n, d_att), BF16),
        jax.ShapeDtypeStruct((n, LANES), BF16),
        jax.ShapeDtypeStruct((META_PAD, d_att), BF16),
        jax.ShapeDtypeStruct((META_PAD, d_att), BF16),
        jax.ShapeDtypeStruct((META_PAD, LANES), BF16),
        jax.ShapeDtypeStruct(side.shape, BF16),
    ]
    out_specs = [
        rows(d_pool), rows(d_att), rows(d_att), rows(d_att), rows(LANES),
        pl.BlockSpec((META_PAD, d_att), lambda i: (0, 0)),
        pl.BlockSpec((META_PAD, d_att), lambda i: (0, 0)),
        pl.BlockSpec((META_PAD, LANES), lambda i: (0, 0)),
        side_spec,
    ]
    scratch = [
        pltpu.VMEM((tm, d), BF16),
        pltpu.VMEM((tm + N_META, d_pool), F32),
        pltpu.VMEM((N_META, d_pool), F32),
        pltpu.VMEM((1, LANES), F32),
        pltpu.VMEM((1, LANES), F32),
    ]
    kern = functools.partial(_inproj_kernel, tm=tm, tiles_a=tiles_a, tiles_per_batch=tiles_per_batch,
                             n_heads=n_heads, d_pool=d_pool, d_att=d_att, row_chunk=ROW_CHUNK)
    return pl.pallas_call(
        kern,
        grid=(n // tm,),
        in_specs=in_specs,
        out_specs=out_specs,
        out_shape=out_shape,
        scratch_shapes=scratch,
        compiler_params=_cparams(1),
        name="in_proj",
    )(h1a, h1b, h1m, gn, w_in, b_gate, qn, kn, pw, ps, side)


def _attn_kernel(q_ref, k_ref, v_ref, km_ref, vm_ref, aug_ref, augm_ref, wg_ref, wu_ref, wd_ref,
                 o_ref, wgub_ref, wdb_ref, qa_ref, ka_ref, va_ref, *, tq):
    h = pl.program_id(1)
    seq = q_ref.shape[0]
    lane = lax.broadcasted_iota(jnp.int32, (1, LANES), 1)
    group = lane >> (SUBLANES.bit_length() - 1)
    mine = (lane & (SUBLANES - 1)) == h
    one_hot = lambda lo: jnp.where(mine & (group >= lo) & (group < lo + 3), 1.0, 0.0).astype(BF16)
    keep = lambda lo: jnp.where((group >= lo) & (group < lo + 3), 0.0, 1.0).astype(BF16)
    qa_ref[:, 0:HEAD_DIM] = q_ref[...]
    qa_ref[:, HEAD_DIM:] = aug_ref[...] * keep(3) + one_hot(3)
    ka_ref[:, 0:HEAD_DIM] = k_ref[...]
    ka_ref[:, HEAD_DIM:] = aug_ref[...] * keep(0) + one_hot(0)
    kma = jnp.concatenate([km_ref[...], augm_ref[...] * keep(0) + one_hot(0)], axis=1)
    va_ref[:, 0:HEAD_DIM] = v_ref[...]
    va_ref[:, HEAD_DIM:] = jnp.ones((seq, HEAD_DIM), BF16)
    vma = jnp.concatenate([vm_ref[...], jnp.ones((META_PAD, HEAD_DIM), BF16)], axis=1)

    hq = tq // 2
    rows_of = lambda w: lax.broadcasted_iota(jnp.int32, (hq, w), 0)
    cols_of = lambda w: lax.broadcasted_iota(jnp.int32, (hq, w), 1)
    causal_top = cols_of(hq) <= rows_of(hq)
    causal_bot = cols_of(tq) <= rows_of(tq) + hq
    tiles = list(reversed(range(seq // tq)))
    scores = {}
    for qi in tiles:
        rows = slice(qi * tq, (qi + 1) * tq)
        q = qa_ref[rows, :]
        blocks = [slice(j * tq, (j + 1) * tq) for j in range(qi)]
        raw_m = jnp.where(lane < N_META, _dot_nt(q, kma), NEG_INF)
        raws = [_dot_nt(q, ka_ref[cols, :]) for cols in blocks]
        raw_t = jnp.where(causal_top, _dot_nt(q[0:hq], ka_ref[qi * tq:qi * tq + hq, :]), NEG_INF)
        raw_b = jnp.where(causal_bot, _dot_nt(q[hq:], ka_ref[rows, :]), NEG_INF)
        scores[qi] = (rows, blocks, raw_m, raws, raw_t, raw_b)
    probs = {}
    for qi in tiles:
        rows, blocks, raw_m, raws, raw_t, raw_b = scores[qi]
        halves = []
        for r0, raw_d in ((0, raw_t), (hq, raw_b)):
            part = slice(r0, r0 + hq)
            full = [raw_m[part]] + [r[part] for r in raws]
            m = jnp.maximum(jnp.max(full[0], axis=-1, keepdims=True),
                            jnp.max(raw_d, axis=-1, keepdims=True))
            if raws:
                m = jnp.maximum(m, jnp.max(functools.reduce(jnp.maximum, full[1:]),
                                           axis=-1, keepdims=True))
            halves.append([jnp.exp2(EXP2_SCALE * (r - m)) for r in full + [raw_d]])
        top, bot = halves
        p_m = jnp.concatenate([top[0], bot[0]], axis=0)
        ps = [jnp.concatenate([t, b], axis=0) for t, b in zip(top[1:-1], bot[1:-1])]
        probs[qi] = (rows, blocks, p_m, ps, top[-1], bot[-1])
    for c in range(0, wg_ref.shape[1], PAIR):
        wgub_ref[:, 2 * c:2 * c + PAIR] = wg_ref[:, c:c + PAIR].astype(BF16)
        wgub_ref[:, 2 * c + PAIR:2 * c + 2 * PAIR] = wu_ref[:, c:c + PAIR].astype(BF16)
    wdb_ref[...] = (0.5 * wd_ref[...]).astype(BF16)
    for qi in tiles:
        rows, blocks, p_m, ps, p_t, p_b = probs[qi]
        acc = _dot(p_m.astype(BF16), vma)
        for p, cols in zip(ps, blocks):
            acc = acc + _dot(p.astype(BF16), va_ref[cols, :])
        acc = acc + jnp.concatenate(
            [_dot(p_t.astype(BF16), va_ref[qi * tq:qi * tq + hq, :]),
             _dot(p_b.astype(BF16), va_ref[rows, :])], axis=0)
        o_ref[rows, :] = (acc[:, 0:HEAD_DIM] / acc[:, HEAD_DIM:]).astype(BF16)


def _attention(q, k, v, km, vm, aug, augm, next_ffn, *, batch, tq=512):
    n, d_att = q.shape
    seq = n // batch
    n_heads = d_att // HEAD_DIM
    steps = batch * n_heads
    tok = pl.BlockSpec((seq, HEAD_DIM), lambda b, h: (b, h))
    meta = pl.BlockSpec((META_PAD, HEAD_DIM), lambda b, h: (0, h))

    def chunk(w):
        assert w.shape[0] % (steps * 2 * SUBLANES) == 0
        return pl.BlockSpec((w.shape[0] // steps, w.shape[1]), lambda b, h: (b * n_heads + h, 0))

    wg, wu, wd = next_ffn
    side_in = [chunk(w) for w in next_ffn]
    side_out = [pl.BlockSpec((wg.shape[0] // steps, 2 * wg.shape[1]),
                             lambda b, h: (b * n_heads + h, 0)), chunk(wd)]
    return pl.pallas_call(
        functools.partial(_attn_kernel, tq=tq),
        grid=(batch, n_heads),
        in_specs=[
            tok, tok, tok, meta, meta,
            pl.BlockSpec((seq, LANES), lambda b, h: (b, 0)),
            pl.BlockSpec((META_PAD, LANES), lambda b, h: (0, 0)),
        ] + side_in,
        out_specs=[tok] + side_out,
        out_shape=[jax.ShapeDtypeStruct((n, d_att), BF16),
                   jax.ShapeDtypeStruct((wg.shape[0], 2 * wg.shape[1]), BF16),
                   jax.ShapeDtypeStruct(wd.shape, BF16)],
        scratch_shapes=[pltpu.VMEM((seq, 2 * HEAD_DIM), BF16),
                        pltpu.VMEM((seq, 2 * HEAD_DIM), BF16),
                        pltpu.VMEM((seq, 2 * HEAD_DIM), BF16)],
        compiler_params=_cparams(2),
        name="fox_attn",
    )(q, k, v, km, vm, aug, augm, *next_ffn)


def _outproj_kernel(ha_ref, hb_ref, pool_ref, att_ref, wop_ref, woa_ref, o_ref, *, tiles_a,
                    col_chunk):
    i = pl.program_id(0)
    d = o_ref.shape[1]
    for c in range(0, d, col_chunk):
        cols = slice(c, c + col_chunk)
        h = jnp.where(i < tiles_a, ha_ref[:, cols], hb_ref[:, cols])
        o_ref[:, cols] = (h + _dot(pool_ref[...], wop_ref[:, cols])
                          + _dot(att_ref[...], woa_ref[:, cols]))


def _out_proj(h1, pool, att, w_out, *, tm=512):
    h1a, h1b = h1
    d = h1a.shape[1]
    n = h1a.shape[0] + h1b.shape[0]
    assert h1a.shape[0] % tm == 0 and h1b.shape[0] % tm == 0
    tiles_a = h1a.shape[0] // tm
    d_pool, d_att = pool.shape[1], att.shape[1]
    assert d_pool == d_att
    rows = lambda w: pl.BlockSpec((tm, w), lambda i: (i, 0))
    return pl.pallas_call(
        functools.partial(_outproj_kernel, tiles_a=tiles_a, col_chunk=COL_CHUNK),
        grid=(n // tm,),
        in_specs=[pl.BlockSpec((tm, d), lambda i: (jnp.minimum(i, tiles_a - 1), 0)),
                  pl.BlockSpec((tm, d), lambda i: (jnp.maximum(i - tiles_a, 0), 0)),
                  rows(d_pool), rows(d_att),
                  _const_spec((d_pool, d), (0, 0)), _const_spec((d_att, d), (1, 0))],
        out_specs=rows(d),
        out_shape=jax.ShapeDtypeStruct((n, d), F32),
        compiler_params=_cparams(1),
        name="out_proj",
    )(h1a, h1b, pool, att, w_out, w_out)


def kernel(x, meta_tokens, ffn1_norm, ffn1_w_gate, ffn1_w_up, ffn1_w_down, mix_norm, w_in,
           b_forget, q_norm, k_norm, pool_w, pool_scale, w_out, ffn2_norm, ffn2_w_gate,
           ffn2_w_up, ffn2_w_down):
    batch, seq, d = x.shape
    n_heads = b_forget.shape[1]
    d_att = n_heads * HEAD_DIM
    d_pool = pool_scale.shape[1]
    assert meta_tokens.shape[0] == N_META
    assert ffn1_norm.shape[0] == 1, "one layer: the meta rows past the mixers never reach the output"

    hx = x.reshape(batch * seq, d)
    hm = meta_tokens.astype(x.dtype)
    g1 = ffn1_norm[0][None]
    h1_head, hm, w1_b = _ffn_head(hx, hm, g1, ffn1_w_gate[0], ffn1_w_up[0], ffn1_w_down[0])
    h1_tail, w_in_b = _ffn_tail(hx, g1, *w1_b, first_tile=1, in_place=False, side_t=w_in[0].T)
    h1 = (h1_head, h1_tail)

    b_gate = jnp.pad(b_forget[0][None], ((0, 0), (0, LANES - n_heads)))
    pool, q, k, v, aug, km, vm, augm, w_out_b = _in_proj(
        h1, hm, mix_norm[0][None], w_in_b, b_gate, q_norm[0][None], k_norm[0][None],
        pool_w[0].astype(BF16), pool_scale[0][None], w_out[0],
        batch=batch, d_pool=d_pool, d_att=d_att)

    att, *w2_b = _attention(q, k, v, km, vm, aug, augm,
                            (ffn2_w_gate[0], ffn2_w_up[0], ffn2_w_down[0]), batch=batch)
    h2 = _out_proj(h1, pool, att, w_out_b)

    out = _ffn_tail(h2, ffn2_norm[0][None], *w2_b, first_tile=0, in_place=True, nested=True)
    return out.reshape(batch, seq, d)
```

```python
import functools

import jax
import jax.numpy as jnp
from jax import lax
from jax.experimental import pallas as pl
from jax.experimental.pallas import tpu as pltpu

F32 = jnp.float32
BF16 = jnp.bfloat16

EPS = 1e-6
N_META = 16
POOL_WINDOWS = (2, 4, 8, 16)
HEAD_DIM = 128
LANES = 128
SUBLANES = 8
META_PAD = 128
NEG_INF = float("-inf")
BIAS_SCALE = HEAD_DIM ** 0.5
EXP2_SCALE = 1.4426950408889634 / BIAS_SCALE

VMEM_LIMIT = 60 * 1024 * 1024
ROW_CHUNK = 256
COL_CHUNK = 512
PAIR = 256


def _cparams(n_axes):
    return pltpu.CompilerParams(
        dimension_semantics=("arbitrary",) * n_axes,
        vmem_limit_bytes=VMEM_LIMIT,
    )


def _rmsnorm_rows(x, gain):
    ms = jnp.mean(x * x, axis=-1, keepdims=True)
    return x * lax.rsqrt(ms + EPS) * gain


def _dot(a, b):
    return jnp.dot(a, b, preferred_element_type=F32)


def _dot_nt(a, b):
    return lax.dot_general(a, b, (((1,), (1,)), ((), ())), preferred_element_type=F32)


def _silu_mul(gate, up):
    return (gate / (1.0 + jnp.exp(-gate)) * up).astype(BF16)


def _accumulate_down(hmid, wd_cols, o_ref, oe_ref, *, tm, col_chunk, init_ref=None):
    d = o_ref.shape[1]
    for c in range(0, d, col_chunk):
        cols = slice(c, c + col_chunk)
        part = _dot(hmid, wd_cols(c, c + col_chunk))
        base = o_ref if init_ref is None else init_ref
        o_ref[:, cols] = base[:, cols] + part[:tm]
        if oe_ref is not None:
            oe_ref[:, cols] += part[tm:]


def _ffn_head_kernel(*refs, tm, n_extra, row_chunk, col_chunk):
    if n_extra:
        (x_ref, e_ref, g_ref, wg_ref, wu_ref, wd_ref,
         o_ref, oe_ref, wgub_ref, wdb_ref, xn_ref) = refs
    else:
        x_ref, g_ref, wg_ref, wu_ref, wd_ref, o_ref, wgub_ref, wdb_ref, xn_ref = refs
        e_ref = oe_ref = None
    f = pl.program_id(0)

    @pl.when(f == 0)
    def _():
        gain = g_ref[...]
        for r in range(0, tm, row_chunk):
            xn_ref[r:r + row_chunk, :] = _rmsnorm_rows(x_ref[r:r + row_chunk, :], gain).astype(BF16)
            o_ref[r:r + row_chunk, :] = x_ref[r:r + row_chunk, :]
        if n_extra:
            xn_ref[tm:tm + n_extra, :] = _rmsnorm_rows(e_ref[...], gain).astype(BF16)
            oe_ref[...] = e_ref[...]

    wg = wg_ref[...].astype(BF16)
    wu = wu_ref[...].astype(BF16)
    wd = (0.5 * wd_ref[...]).astype(BF16)
    wgub_ref[:, 0:PAIR] = wg
    wgub_ref[:, PAIR:] = wu
    wdb_ref[...] = wd
    xn = xn_ref[...]
    _accumulate_down(_silu_mul(_dot(xn, wg), _dot(xn, wu)), lambda c0, c1: wd[:, c0:c1],
                     o_ref, oe_ref, tm=tm, col_chunk=col_chunk)


def _ffn_tail_kernel(*refs, tm, side_rows, side_chunks, row_chunk, col_chunk):
    x_ref, g_ref, wgu_ref, wd_ref = refs[:4]
    xn_ref = refs[-1]
    i, f, nf = pl.program_id(0), pl.program_id(1), pl.num_programs(1)
    if side_rows:
        side_ref, o_ref, sideb_ref = refs[4], refs[5], refs[6]
    else:
        o_ref = refs[4]

    @pl.when(f == 0)
    def _():
        gain = g_ref[...]
        for r in range(0, tm, row_chunk):
            xn_ref[r:r + row_chunk, :] = _rmsnorm_rows(x_ref[r:r + row_chunk, :], gain).astype(BF16)

    def step(first):
        if side_rows:
            chunk = jnp.minimum(i * nf + f, side_chunks - 1)
            row = (chunk * side_ref.shape[0]
                   + lax.broadcasted_iota(jnp.int32, (side_ref.shape[0], 1), 0))
            sideb_ref[...] = jnp.where(row < side_rows, side_ref[...], 0.0).T.astype(BF16)
        z = _dot(xn_ref[...], wgu_ref[...])
        hmid = jnp.concatenate(
            [_silu_mul(z[:, c:c + PAIR], z[:, c + PAIR:c + 2 * PAIR])
             for c in range(0, z.shape[1], 2 * PAIR)], axis=1)
        _accumulate_down(hmid, lambda c0, c1: wd_ref[:, c0:c1], o_ref, None, tm=tm,
                         col_chunk=col_chunk, init_ref=x_ref if first else None)

    pl.when(f == 0)(functools.partial(step, True))
    pl.when(f > 0)(functools.partial(step, False))


def _ffn_head(x, extra, gain, wg, wu, wd, *, tm=1024, tf=PAIR):
    assert tf == PAIR
    n, d = x.shape
    dff = wg.shape[1]
    n_extra = 0 if extra is None else extra.shape[0]
    assert n % tm == 0 and dff % tf == 0
    in_specs = [pl.BlockSpec((tm, d), lambda f: (0, 0), pipeline_mode=pl.Buffered(1))]
    args = [x]
    if n_extra:
        in_specs.append(pl.BlockSpec((n_extra, d), lambda f: (0, 0)))
        args.append(extra)
    in_specs += [
        pl.BlockSpec((1, d), lambda f: (0, 0)),
        pl.BlockSpec((d, tf), lambda f: (0, f)),
        pl.BlockSpec((d, tf), lambda f: (0, f)),
        pl.BlockSpec((tf, d), lambda f: (f, 0)),
    ]
    args += [gain, wg, wu, wd]
    out_shape = [jax.ShapeDtypeStruct((tm, d), F32)]
    out_specs = [pl.BlockSpec((tm, d), lambda f: (0, 0))]
    if n_extra:
        out_shape.append(jax.ShapeDtypeStruct((n_extra, d), F32))
        out_specs.append(pl.BlockSpec((n_extra, d), lambda f: (0, 0)))
    out_shape += [jax.ShapeDtypeStruct((d, 2 * dff), BF16), jax.ShapeDtypeStruct(wd.shape, BF16)]
    out_specs += [pl.BlockSpec((d, 2 * tf), lambda f: (0, f)),
                  pl.BlockSpec((tf, d), lambda f: (f, 0))]
    head = pl.pallas_call(
        functools.partial(_ffn_head_kernel, tm=tm, n_extra=n_extra, row_chunk=ROW_CHUNK,
                          col_chunk=COL_CHUNK),
        grid=(dff // tf,),
        in_specs=in_specs,
        out_specs=out_specs,
        out_shape=out_shape,
        scratch_shapes=[pltpu.VMEM((tm + n_extra, d), BF16)],
        compiler_params=_cparams(1),
        name="ffn_head",
    )(*args)
    return head[0], (head[1] if n_extra else None), tuple(head[-2:])


def _ffn_tail(x, gain, wgub, wdb, *, first_tile, in_place, side_t=None, nested=False, tm=1024,
              tf=512, side_chunk=LANES):
    n, d = x.shape
    dff = wdb.shape[0]
    assert n % tm == 0 and dff % tf == 0 and tf % PAIR == 0 and wgub.shape == (d, 2 * dff)
    n_tiles = n // tm - first_tile
    nf = dff // tf
    in_specs = [
        pl.BlockSpec((tm, d), lambda i, f: (i + first_tile, 0)),
        pl.BlockSpec((1, d), lambda i, f: (0, 0)),
        pl.BlockSpec((d, 2 * tf), lambda i, f: (0, f)),
        pl.BlockSpec((tf, d), lambda i, f: (f, 0)),
    ]
    args = [x, gain, wgub, wdb]
    out_shape = [jax.ShapeDtypeStruct((n if in_place else n_tiles * tm, d), F32)]
    out_specs = [pl.BlockSpec((tm, d), lambda i, f: (i + (first_tile if in_place else 0), 0))]
    n_chunks = 0
    if side_t is not None:
        cols, rows = side_t.shape
        n_chunks = pl.cdiv(cols, side_chunk)
        assert n_chunks <= n_tiles * nf
        chunk_of = lambda i, f: jnp.minimum(i * nf + f, n_chunks - 1)
        in_specs.append(pl.BlockSpec((side_chunk, rows), lambda i, f: (chunk_of(i, f), 0)))
        args.append(side_t)
        out_shape.append(jax.ShapeDtypeStruct((rows, n_chunks * side_chunk), BF16))
        out_specs.append(pl.BlockSpec((rows, side_chunk), lambda i, f: (0, chunk_of(i, f))))
    step = functools.partial(_ffn_tail_kernel, tm=tm,
                             side_rows=0 if side_t is None else side_t.shape[0],
                             side_chunks=n_chunks, row_chunk=ROW_CHUNK, col_chunk=COL_CHUNK)
    if nested:
        def whole(*refs):
            pltpu.emit_pipeline(step, grid=(n_tiles, nf), in_specs=in_specs, out_specs=out_specs)(
                *refs[:-1], scratches=(refs[-1],))

        anywhere = pl.BlockSpec(memory_space=pl.ANY)
        tail = pl.pallas_call(
            whole,
            in_specs=[anywhere] * len(in_specs),
            out_specs=[anywhere] * len(out_specs),
            out_shape=out_shape,
            scratch_shapes=[pltpu.VMEM((tm, d), BF16)],
            input_output_aliases={0: 0} if in_place else {},
            compiler_params=pltpu.CompilerParams(vmem_limit_bytes=VMEM_LIMIT),
            name="ffn_tail",
        )(*args)
        return tail[0] if side_t is None else (tail[0], tail[1])
    tail = pl.pallas_call(
        step,
        grid=(n_tiles, nf),
        in_specs=in_specs,
        out_specs=out_specs,
        out_shape=out_shape,
        scratch_shapes=[pltpu.VMEM((tm, d), BF16)],
        input_output_aliases={0: 0} if in_place else {},
        compiler_params=_cparams(2),
        name="ffn_tail",
    )(*args)
    return tail[0] if side_t is None else (tail[0], tail[1])


def _log_sigmoid(x):
    return jnp.minimum(x, 0.0) - jnp.log1p(jnp.exp(-jnp.abs(x)))


def _split3(x):
    hi = x.astype(BF16)
    r1 = x - hi.astype(F32)
    mid = r1.astype(BF16)
    lo = (r1 - mid.astype(F32)).astype(BF16)
    return hi, mid, lo


def _cumsum_rows(x, tri, head_lane):
    hi, mid, lo = (p.astype(F32) for p in _split3(x))
    packed = hi + pltpu.roll(mid, SUBLANES, axis=1) + pltpu.roll(lo, 2 * SUBLANES, axis=1)
    c = _dot(tri, packed.astype(BF16))
    c = c + pltpu.roll(c, LANES - SUBLANES, axis=1) + pltpu.roll(c, LANES - 2 * SUBLANES, axis=1)
    return jnp.where(head_lane, c, 0.0)


def _bias_columns(cum):
    hi, mid, lo = (p.astype(F32) for p in _split3(cum * BIAS_SCALE))
    roll = lambda x, g: pltpu.roll(x, g * SUBLANES, axis=1)
    cols = hi + roll(mid, 1) + roll(lo, 2) - roll(hi, 3) - roll(mid, 4) - roll(lo, 5)
    return cols.astype(BF16)


def _head_rmsnorm(z, gain, n_heads):
    outs = []
    for h in range(n_heads):
        outs.append(_rmsnorm_rows(z[:, h * HEAD_DIM:(h + 1) * HEAD_DIM], gain))
    return jnp.concatenate(outs, axis=-1)


def _window_means(e, w):
    assert w & (w - 1) == 0 and w <= N_META
    s, span = e, 1
    while span < w:
        s = s + pltpu.roll(s, span, axis=0)
        span *= 2
    return s[N_META:, :] / float(w)


def _inproj_kernel(ha_ref, hb_ref, hm_ref, gn_ref, w_ref, b_ref, qn_ref, kn_ref, pw_ref, ps_ref,
                   side_ref,
                   pool_ref, q_ref, k_ref, v_ref, aug_ref, km_ref, vm_ref, augm_ref, sideb_ref,
                   u_ref, pe_ref, pmeta_ref, ccol_ref, mcol_ref,
                   *, tm, tiles_a, tiles_per_batch, n_heads, d_pool, d_att, row_chunk):
    i = pl.program_id(0)
    gain = gn_ref[...]
    sideb_ref[...] = side_ref[...].astype(BF16)
    head_lane = lax.broadcasted_iota(jnp.int32, (1, LANES), 1) < n_heads
    w_pool = lambda: w_ref[:, 0:d_pool]
    w_q = lambda: w_ref[:, d_pool:d_pool + d_att]
    w_k = lambda: w_ref[:, d_pool + d_att:d_pool + 2 * d_att]
    w_vf = lambda: w_ref[:, d_pool + 2 * d_att:]

    def log_forget(zvf):
        lf = _log_sigmoid(zvf[:, d_att:] + b_ref[...])
        return jnp.where(head_lane, lf, 0.0)

    def tri(t):
        r_i = lax.broadcasted_iota(jnp.int32, (t, t), 0)
        c_i = lax.broadcasted_iota(jnp.int32, (t, t), 1)
        return (c_i <= r_i).astype(BF16)

    @pl.when(i == 0)
    def _():
        u_ref[0:META_PAD, :] = jnp.zeros((META_PAD, u_ref.shape[1]), BF16)
        u_ref[0:N_META, :] = _rmsnorm_rows(hm_ref[...], gain).astype(BF16)
        um = u_ref[0:META_PAD, :]
        row_valid = lax.broadcasted_iota(jnp.int32, (META_PAD, 1), 0) < N_META
        pmeta_ref[...] = _dot(um, w_pool())[0:N_META, :]
        km = _head_rmsnorm(_dot(um, w_k()), kn_ref[...], n_heads)
        km_ref[...] = jnp.where(row_valid, km, 0.0).astype(BF16)
        zvf = _dot(um, w_vf())
        vm_ref[...] = zvf[:, 0:d_att].astype(BF16)
        cum_m = _cumsum_rows(jnp.where(row_valid, log_forget(zvf), 0.0), tri(META_PAD), head_lane)
        augm_ref[...] = _bias_columns(cum_m)
        mcol_ref[...] = cum_m[META_PAD - 1:META_PAD, :]

    @pl.when(i % tiles_per_batch == 0)
    def _():
        ccol_ref[...] = mcol_ref[...]
        pe_ref[0:N_META, :] = pmeta_ref[...]

    for r in range(0, tm, row_chunk):
        h = jnp.where(i < tiles_a, ha_ref[r:r + row_chunk, :], hb_ref[r:r + row_chunk, :])
        u_ref[r:r + row_chunk, :] = _rmsnorm_rows(h, gain).astype(BF16)
    u = u_ref[0:tm, :]

    pe_ref[N_META:N_META + tm, :] = _dot(u, w_pool())
    q_ref[...] = _head_rmsnorm(_dot(u, w_q()), qn_ref[...], n_heads).astype(BF16)
    k_ref[...] = _head_rmsnorm(_dot(u, w_k()), kn_ref[...], n_heads).astype(BF16)
    zvf = _dot(u, w_vf())
    v_ref[...] = zvf[:, 0:d_att].astype(BF16)

    gw = d_pool // len(POOL_WINDOWS)
    for g, w in enumerate(POOL_WINDOWS):
        lanes = slice(g * gw, (g + 1) * gw)
        e = pe_ref[:, lanes]
        pooled = (_window_means(e, w) - e[N_META:, :]).astype(BF16)
        mixed = _dot(pooled, pw_ref[g]) * ps_ref[:, lanes]
        pool_ref[:, lanes] = mixed.astype(BF16)
    pe_ref[0:N_META, :] = pe_ref[tm:tm + N_META, :]

    cum_col = _cumsum_rows(log_forget(zvf), tri(tm), head_lane) + ccol_ref[...]
    aug_ref[...] = _bias_columns(cum_col)
    ccol_ref[...] = cum_col[tm - 1:tm, :]


def _const_spec(shape, index=None):
    nd = len(shape)
    index = (0,) * nd if index is None else index
    return pl.BlockSpec(shape, lambda i: index, pipeline_mode=pl.Buffered(1))


def _in_proj(h1, h1m, gn, w_in, b_gate, qn, kn, pw, ps, side, *, batch, d_pool, d_att, tm=512):
    h1a, h1b = h1
    d = h1a.shape[1]
    n = h1a.shape[0] + h1b.shape[0]
    assert h1a.shape[0] % tm == 0 and h1b.shape[0] % tm == 0
    tiles_a = h1a.shape[0] // tm
    seq = n // batch
    n_heads = d_att // HEAD_DIM
    assert seq % tm == 0 and tm >= META_PAD and d_pool == d_att and n_heads <= SUBLANES
    assert w_in.shape == (d, d_pool + 3 * d_att + LANES)
    tiles_per_batch = seq // tm
    side_rows = side.shape[0] // (n // tm)
    assert side_rows * (n // tm) == side.shape[0] and side_rows % SUBLANES == 0
    side_spec = pl.BlockSpec((side_rows, side.shape[1]), lambda i: (i, 0))
    rows = lambda w: pl.BlockSpec((tm, w), lambda i: (i, 0))
    in_specs = [
        pl.BlockSpec((tm, d), lambda i: (jnp.minimum(i, tiles_a - 1), 0)),
        pl.BlockSpec((tm, d), lambda i: (jnp.maximum(i - tiles_a, 0), 0)),
        _const_spec((N_META, d)),
        _const_spec((1, d)),
        _const_spec(w_in.shape),
        _const_spec((1, LANES)),
        _const_spec((1, HEAD_DIM)),
        _const_spec((1, HEAD_DIM)),
        _const_spec(pw.shape),
        _const_spec((1, d_pool)),
        side_spec,
    ]
    out_shape = [
        jax.ShapeDtypeStruct((n, d_pool), BF16),
        jax.ShapeDtypeStruct((n, d_att), BF16),
        jax.ShapeDtypeStruct((n, d_att), BF16),
        jax.ShapeDtypeStruct((n, d_att), BF16),
        jax.ShapeDtypeStruct((n, LANES), BF16),
        jax.ShapeDtypeStruct((META_PAD, d_att), BF16),
        jax.ShapeDtypeStruct((META_PAD, d_att), BF16),
        jax.ShapeDtypeStruct((META_PAD, LANES), BF16),
        jax.ShapeDtypeStruct(side.shape, BF16),
    ]
    out_specs = [
        rows(d_pool), rows(d_att), rows(d_att), rows(d_att), rows(LANES),
        pl.BlockSpec((META_PAD, d_att), lambda i: (0, 0)),
        pl.BlockSpec((META_PAD, d_att), lambda i: (0, 0)),
        pl.BlockSpec((META_PAD, LANES), lambda i: (0, 0)),
        side_spec,
    ]
    scratch = [
        pltpu.VMEM((tm, d), BF16),
        pltpu.VMEM((tm + N_META, d_pool), F32),
        pltpu.VMEM((N_META, d_pool), F32),
        pltpu.VMEM((1, LANES), F32),
        pltpu.VMEM((1, LANES), F32),
    ]
    kern = functools.partial(_inproj_kernel, tm=tm, tiles_a=tiles_a, tiles_per_batch=tiles_per_batch,
                             n_heads=n_heads, d_pool=d_pool, d_att=d_att, row_chunk=ROW_CHUNK)
    return pl.pallas_call(
        kern,
        grid=(n // tm,),
        in_specs=in_specs,
        out_specs=out_specs,
        out_shape=out_shape,
        scratch_shapes=scratch,
        compiler_params=_cparams(1),
        name="in_proj",
    )(h1a, h1b, h1m, gn, w_in, b_gate, qn, kn, pw, ps, side)


def _attn_kernel(q_ref, k_ref, v_ref, km_ref, vm_ref, aug_ref, augm_ref, wg_ref, wu_ref, wd_ref,
                 o_ref, wgub_ref, wdb_ref, qa_ref, ka_ref, va_ref, *, tq):
    h = pl.program_id(1)
    seq = q_ref.shape[0]
    lane = lax.broadcasted_iota(jnp.int32, (1, LANES), 1)
    group = lane >> (SUBLANES.bit_length() - 1)
    mine = (lane & (SUBLANES - 1)) == h
    one_hot = lambda lo: jnp.where(mine & (group >= lo) & (group < lo + 3), 1.0, 0.0).astype(BF16)
    keep = lambda lo: jnp.where((group >= lo) & (group < lo + 3), 0.0, 1.0).astype(BF16)
    qa_ref[:, 0:HEAD_DIM] = q_ref[...]
    qa_ref[:, HEAD_DIM:] = aug_ref[...] * keep(3) + one_hot(3)
    ka_ref[:, 0:HEAD_DIM] = k_ref[...]
    ka_ref[:, HEAD_DIM:] = aug_ref[...] * keep(0) + one_hot(0)
    kma = jnp.concatenate([km_ref[...], augm_ref[...] * keep(0) + one_hot(0)], axis=1)
    va_ref[:, 0:HEAD_DIM] = v_ref[...]
    va_ref[:, HEAD_DIM:] = jnp.ones((seq, HEAD_DIM), BF16)
    vma = jnp.concatenate([vm_ref[...], jnp.ones((META_PAD, HEAD_DIM), BF16)], axis=1)

    hq = tq // 2
    rows_of = lambda w: lax.broadcasted_iota(jnp.int32, (hq, w), 0)
    cols_of = lambda w: lax.broadcasted_iota(jnp.int32, (hq, w), 1)
    causal_top = cols_of(hq) <= rows_of(hq)
    causal_bot = cols_of(tq) <= rows_of(tq) + hq
    tiles = list(reversed(range(seq // tq)))
    scores = {}
    for qi in tiles:
        rows = slice(qi * tq, (qi + 1) * tq)
        q = qa_ref[rows, :]
        blocks = [slice(j * tq, (j + 1) * tq) for j in range(qi)]
        raw_m = jnp.where(lane < N_META, _dot_nt(q, kma), NEG_INF)
        raws = [_dot_nt(q, ka_ref[cols, :]) for cols in blocks]
        raw_t = jnp.where(causal_top, _dot_nt(q[0:hq], ka_ref[qi * tq:qi * tq + hq, :]), NEG_INF)
        raw_b = jnp.where(causal_bot, _dot_nt(q[hq:], ka_ref[rows, :]), NEG_INF)
        scores[qi] = (rows, blocks, raw_m, raws, raw_t, raw_b)
    probs = {}
    for qi in tiles:
        rows, blocks, raw_m, raws, raw_t, raw_b = scores[qi]
        halves = []
        for r0, raw_d in ((0, raw_t), (hq, raw_b)):
            part = slice(r0, r0 + hq)
            full = [raw_m[part]] + [r[part] for r in raws]
            m = jnp.maximum(jnp.max(full[0], axis=-1, keepdims=True),
                            jnp.max(raw_d, axis=-1, keepdims=True))
            if raws:
                m = jnp.maximum(m, jnp.max(functools.reduce(jnp.maximum, full[1:]),
                                           axis=-1, keepdims=True))
            halves.append([jnp.exp2(EXP2_SCALE * (r - m)) for r in full + [raw_d]])
        top, bot = halves
        p_m = jnp.concatenate([top[0], bot[0]], axis=0)
        ps = [jnp.concatenate([t, b], axis=0) for t, b in zip(top[1:-1], bot[1:-1])]
        probs[qi] = (rows, blocks, p_m, ps, top[-1], bot[-1])
    for c in range(0, wg_ref.shape[1], PAIR):
        wgub_ref[:, 2 * c:2 * c + PAIR] = wg_ref[:, c:c + PAIR].astype(BF16)
        wgub_ref[:, 2 * c + PAIR:2 * c + 2 * PAIR] = wu_ref[:, c:c + PAIR].astype(BF16)
    wdb_ref[...] = (0.5 * wd_ref[...]).astype(BF16)
    for qi in tiles:
        rows, blocks, p_m, ps, p_t, p_b = probs[qi]
        acc = _dot(p_m.astype(BF16), vma)
        for p, cols in zip(ps, blocks):
            acc = acc + _dot(p.astype(BF16), va_ref[cols, :])
        acc = acc + jnp.concatenate(
            [_dot(p_t.astype(BF16), va_ref[qi * tq:qi * tq + hq, :]),
             _dot(p_b.astype(BF16), va_ref[rows, :])], axis=0)
        o_ref[rows, :] = (acc[:, 0:HEAD_DIM] / acc[:, HEAD_DIM:]).astype(BF16)


def _attention(q, k, v, km, vm, aug, augm, next_ffn, *, batch, tq=512):
    n, d_att = q.shape
    seq = n // batch
    n_heads = d_att // HEAD_DIM
    steps = batch * n_heads
    tok = pl.BlockSpec((seq, HEAD_DIM), lambda b, h: (b, h))
    meta = pl.BlockSpec((META_PAD, HEAD_DIM), lambda b, h: (0, h))

    def chunk(w):
        assert w.shape[0] % (steps * 2 * SUBLANES) == 0
        return pl.BlockSpec((w.shape[0] // steps, w.shape[1]), lambda b, h: (b * n_heads + h, 0))

    wg, wu, wd = next_ffn
    side_in = [chunk(w) for w in next_ffn]
    side_out = [pl.BlockSpec((wg.shape[0] // steps, 2 * wg.shape[1]),
                             lambda b, h: (b * n_heads + h, 0)), chunk(wd)]
    return pl.pallas_call(
        functools.partial(_attn_kernel, tq=tq),
        grid=(batch, n_heads),
        in_specs=[
            tok, tok, tok, meta, meta,
            pl.BlockSpec((seq, LANES), lambda b, h: (b, 0)),
            pl.BlockSpec((META_PAD, LANES), lambda b, h: (0, 0)),
        ] + side_in,
        out_specs=[tok] + side_out,
        out_shape=[jax.ShapeDtypeStruct((n, d_att), BF16),
                   jax.ShapeDtypeStruct((wg.shape[0], 2 * wg.shape[1]), BF16),
                   jax.ShapeDtypeStruct(wd.shape, BF16)],
        scratch_shapes=[pltpu.VMEM((seq, 2 * HEAD_DIM), BF16),
                        pltpu.VMEM((seq, 2 * HEAD_DIM), BF16),
                        pltpu.VMEM((seq, 2 * HEAD_DIM), BF16)],
        compiler_params=_cparams(2),
        name="fox_attn",
    )(q, k, v, km, vm, aug, augm, *next_ffn)


def _outproj_kernel(ha_ref, hb_ref, pool_ref, att_ref, wop_ref, woa_ref, o_ref, *, tiles_a,
                    col_chunk):
    i = pl.program_id(0)
    d = o_ref.shape[1]
    for c in range(0, d, col_chunk):
        cols = slice(c, c + col_chunk)
        h = jnp.where(i < tiles_a, ha_ref[:, cols], hb_ref[:, cols])
        o_ref[:, cols] = (h + _dot(pool_ref[...], wop_ref[:, cols])
                          + _dot(att_ref[...], woa_ref[:, cols]))


def _out_proj(h1, pool, att, w_out, *, tm=512):
    h1a, h1b = h1
    d = h1a.shape[1]
    n = h1a.shape[0] + h1b.shape[0]
    assert h1a.shape[0] % tm == 0 and h1b.shape[0] % tm == 0
    tiles_a = h1a.shape[0] // tm
    d_pool, d_att = pool.shape[1], att.shape[1]
    assert d_pool == d_att
    rows = lambda w: pl.BlockSpec((tm, w), lambda i: (i, 0))
    return pl.pallas_call(
        functools.partial(_outproj_kernel, tiles_a=tiles_a, col_chunk=COL_CHUNK),
        grid=(n // tm,),
        in_specs=[pl.BlockSpec((tm, d), lambda i: (jnp.minimum(i, tiles_a - 1), 0)),
                  pl.BlockSpec((tm, d), lambda i: (jnp.maximum(i - tiles_a, 0), 0)),
                  rows(d_pool), rows(d_att),
                  _const_spec((d_pool, d), (0, 0)), _const_spec((d_att, d), (1, 0))],
        out_specs=rows(d),
        out_shape=jax.ShapeDtypeStruct((n, d), F32),
        compiler_params=_cparams(1),
        name="out_proj",
    )(h1a, h1b, pool, att, w_out, w_out)


def kernel(x, meta_tokens, ffn1_norm, ffn1_w_gate, ffn1_w_up, ffn1_w_down, mix_norm, w_in,
           b_forget, q_norm, k_norm, pool_w, pool_scale, w_out, ffn2_norm, ffn2_w_gate,
           ffn2_w_up, ffn2_w_down):
    batch, seq, d = x.shape
    n_heads = b_forget.shape[1]
    d_att = n_heads * HEAD_DIM
    d_pool = pool_scale.shape[1]
    assert meta_tokens.shape[0] == N_META
    assert ffn1_norm.shape[0] == 1, "one layer: the meta rows past the mixers never reach the output"

    hx = x.reshape(batch * seq, d)
    hm = meta_tokens.astype(x.dtype)
    g1 = ffn1_norm[0][None]
    h1_head, hm, w1_b = _ffn_head(hx, hm, g1, ffn1_w_gate[0], ffn1_w_up[0], ffn1_w_down[0])
    h1_tail, w_in_b = _ffn_tail(hx, g1, *w1_b, first_tile=1, in_place=False, side_t=w_in[0].T,
                                nested=True)
    h1 = (h1_head, h1_tail)

    b_gate = jnp.pad(b_forget[0][None], ((0, 0), (0, LANES - n_heads)))
    pool, q, k, v, aug, km, vm, augm, w_out_b = _in_proj(
        h1, hm, mix_norm[0][None], w_in_b, b_gate, q_norm[0][None], k_norm[0][None],
        pool_w[0].astype(BF16), pool_scale[0][None], w_out[0],
        batch=batch, d_pool=d_pool, d_att=d_att)

    att, *w2_b = _attention(q, k, v, km, vm, aug, augm,
                            (ffn2_w_gate[0], ffn2_w_up[0], ffn2_w_down[0]), batch=batch)
    h2 = _out_proj(h1, pool, att, w_out_b)

    out = _ffn_tail(h2, ffn2_norm[0][None], *w2_b, first_tile=0, in_place=True, nested=True)
    return out.reshape(batch, seq, d)
```

```python
import functools

import jax
import jax.numpy as jnp
from jax import lax
from jax.experimental import pallas as pl
from jax.experimental.pallas import tpu as pltpu

F32 = jnp.float32
BF16 = jnp.bfloat16

EPS = 1e-6
N_META = 16
POOL_WINDOWS = (2, 4, 8, 16)
HEAD_DIM = 128
LANES = 128
SUBLANES = 8
META_PAD = 128
NEG_INF = float("-inf")
BIAS_SCALE = HEAD_DIM ** 0.5
EXP2_SCALE = 1.4426950408889634 / BIAS_SCALE

VMEM_LIMIT = 60 * 1024 * 1024
ROW_CHUNK = 256
COL_CHUNK = 512
PAIR = 256


def _cparams(n_axes):
    return pltpu.CompilerParams(
        dimension_semantics=("arbitrary",) * n_axes,
        vmem_limit_bytes=VMEM_LIMIT,
    )


def _rmsnorm_rows(x, gain):
    ms = jnp.mean(x * x, axis=-1, keepdims=True)
    return x * lax.rsqrt(ms + EPS) * gain


def _dot(a, b):
    return jnp.dot(a, b, preferred_element_type=F32)


def _dot_nt(a, b):
    return lax.dot_general(a, b, (((1,), (1,)), ((), ())), preferred_element_type=F32)


def _silu_mul(gate, up):
    return (gate / (1.0 + jnp.exp(-gate)) * up).astype(BF16)


def _accumulate_down(hmid, wd_cols, o_ref, oe_ref, *, tm, col_chunk, init_ref=None):
    d = o_ref.shape[1]
    for c in range(0, d, col_chunk):
        cols = slice(c, c + col_chunk)
        part = _dot(hmid, wd_cols(c, c + col_chunk))
        base = o_ref if init_ref is None else init_ref
        o_ref[:, cols] = base[:, cols] + part[:tm]
        if oe_ref is not None:
            oe_ref[:, cols] += part[tm:]


def _ffn_head_kernel(*refs, tm, n_extra, row_chunk, col_chunk):
    if n_extra:
        (x_ref, e_ref, g_ref, wg_ref, wu_ref, wd_ref,
         o_ref, oe_ref, wgub_ref, wdb_ref, xn_ref) = refs
    else:
        x_ref, g_ref, wg_ref, wu_ref, wd_ref, o_ref, wgub_ref, wdb_ref, xn_ref = refs
        e_ref = oe_ref = None
    f = pl.program_id(0)

    @pl.when(f == 0)
    def _():
        gain = g_ref[...]
        for r in range(0, tm, row_chunk):
            xn_ref[r:r + row_chunk, :] = _rmsnorm_rows(x_ref[r:r + row_chunk, :], gain).astype(BF16)
            o_ref[r:r + row_chunk, :] = x_ref[r:r + row_chunk, :]
        if n_extra:
            xn_ref[tm:tm + n_extra, :] = _rmsnorm_rows(e_ref[...], gain).astype(BF16)
            oe_ref[...] = e_ref[...]

    wg = wg_ref[...].astype(BF16)
    wu = wu_ref[...].astype(BF16)
    wd = (0.5 * wd_ref[...]).astype(BF16)
    wgub_ref[:, 0:PAIR] = wg
    wgub_ref[:, PAIR:] = wu
    wdb_ref[...] = wd
    xn = xn_ref[...]
    _accumulate_down(_silu_mul(_dot(xn, wg), _dot(xn, wu)), lambda c0, c1: wd[:, c0:c1],
                     o_ref, oe_ref, tm=tm, col_chunk=col_chunk)


def _ffn_tail_kernel(*refs, tm, side_rows, side_chunks, row_chunk, col_chunk):
    x_ref, g_ref, wgu_ref, wd_ref = refs[:4]
    xn_ref = refs[-1]
    i, f, nf = pl.program_id(0), pl.program_id(1), pl.num_programs(1)
    if side_rows:
        side_ref, o_ref, sideb_ref = refs[4], refs[5], refs[6]
    else:
        o_ref = refs[4]

    @pl.when(f == 0)
    def _():
        gain = g_ref[...]
        for r in range(0, tm, row_chunk):
            xn_ref[r:r + row_chunk, :] = _rmsnorm_rows(x_ref[r:r + row_chunk, :], gain).astype(BF16)

    def step(first):
        if side_rows:
            chunk = jnp.minimum(i * nf + f, side_chunks - 1)
            row = (chunk * side_ref.shape[0]
                   + lax.broadcasted_iota(jnp.int32, (side_ref.shape[0], 1), 0))
            sideb_ref[...] = jnp.where(row < side_rows, side_ref[...], 0.0).T.astype(BF16)
        z = _dot(xn_ref[...], wgu_ref[...])
        hmid = jnp.concatenate(
            [_silu_mul(z[:, c:c + PAIR], z[:, c + PAIR:c + 2 * PAIR])
             for c in range(0, z.shape[1], 2 * PAIR)], axis=1)
        _accumulate_down(hmid, lambda c0, c1: wd_ref[:, c0:c1], o_ref, None, tm=tm,
                         col_chunk=col_chunk, init_ref=x_ref if first else None)

    pl.when(f == 0)(functools.partial(step, True))
    pl.when(f > 0)(functools.partial(step, False))


def _ffn_head(x, extra, gain, wg, wu, wd, *, tm=1024, tf=PAIR):
    assert tf == PAIR
    n, d = x.shape
    dff = wg.shape[1]
    n_extra = 0 if extra is None else extra.shape[0]
    assert n % tm == 0 and dff % tf == 0
    in_specs = [pl.BlockSpec((tm, d), lambda f: (0, 0), pipeline_mode=pl.Buffered(1))]
    args = [x]
    if n_extra:
        in_specs.append(pl.BlockSpec((n_extra, d), lambda f: (0, 0)))
        args.append(extra)
    in_specs += [
        pl.BlockSpec((1, d), lambda f: (0, 0)),
        pl.BlockSpec((d, tf), lambda f: (0, f)),
        pl.BlockSpec((d, tf), lambda f: (0, f)),
        pl.BlockSpec((tf, d), lambda f: (f, 0)),
    ]
    args += [gain, wg, wu, wd]
    out_shape = [jax.ShapeDtypeStruct((tm, d), F32)]
    out_specs = [pl.BlockSpec((tm, d), lambda f: (0, 0))]
    if n_extra:
        out_shape.append(jax.ShapeDtypeStruct((n_extra, d), F32))
        out_specs.append(pl.BlockSpec((n_extra, d), lambda f: (0, 0)))
    out_shape += [jax.ShapeDtypeStruct((d, 2 * dff), BF16), jax.ShapeDtypeStruct(wd.shape, BF16)]
    out_specs += [pl.BlockSpec((d, 2 * tf), lambda f: (0, f)),
                  pl.BlockSpec((tf, d), lambda f: (f, 0))]
    head = pl.pallas_call(
        functools.partial(_ffn_head_kernel, tm=tm, n_extra=n_extra, row_chunk=ROW_CHUNK,
                          col_chunk=COL_CHUNK),
        grid=(dff // tf,),
        in_specs=in_specs,
        out_specs=out_specs,
        out_shape=out_shape,
        scratch_shapes=[pltpu.VMEM((tm + n_extra, d), BF16)],
        compiler_params=_cparams(1),
        name="ffn_head",
    )(*args)
    return head[0], (head[1] if n_extra else None), tuple(head[-2:])


def _ffn_tail(x, gain, wgub, wdb, *, first_tile, in_place, side_t=None, tm=1024, tf=512,
              side_chunk=LANES):
    n, d = x.shape
    dff = wdb.shape[0]
    assert n % tm == 0 and dff % tf == 0 and tf % PAIR == 0 and wgub.shape == (d, 2 * dff)
    n_tiles = n // tm - first_tile
    nf = dff // tf
    in_specs = [
        pl.BlockSpec((tm, d), lambda i, f: (i + first_tile, 0)),
        pl.BlockSpec((1, d), lambda i, f: (0, 0)),
        pl.BlockSpec((d, 2 * tf), lambda i, f: (0, f)),
        pl.BlockSpec((tf, d), lambda i, f: (f, 0)),
    ]
    args = [x, gain, wgub, wdb]
    out_shape = [jax.ShapeDtypeStruct((n if in_place else n_tiles * tm, d), F32)]
    out_specs = [pl.BlockSpec((tm, d), lambda i, f: (i + (first_tile if in_place else 0), 0))]
    n_chunks = 0
    if side_t is not None:
        cols, rows = side_t.shape
        n_chunks = pl.cdiv(cols, side_chunk)
        assert n_chunks <= n_tiles * nf
        chunk_of = lambda i, f: jnp.minimum(i * nf + f, n_chunks - 1)
        in_specs.append(pl.BlockSpec((side_chunk, rows), lambda i, f: (chunk_of(i, f), 0)))
        args.append(side_t)
        out_shape.append(jax.ShapeDtypeStruct((rows, n_chunks * side_chunk), BF16))
        out_specs.append(pl.BlockSpec((rows, side_chunk), lambda i, f: (0, chunk_of(i, f))))
    step = functools.partial(_ffn_tail_kernel, tm=tm,
                             side_rows=0 if side_t is None else side_t.shape[0],
                             side_chunks=n_chunks, row_chunk=ROW_CHUNK, col_chunk=COL_CHUNK)

    def whole(*refs):
        pltpu.emit_pipeline(step, grid=(n_tiles, nf), in_specs=in_specs, out_specs=out_specs)(
            *refs[:-1], scratches=(refs[-1],))

    anywhere = pl.BlockSpec(memory_space=pl.ANY)
    tail = pl.pallas_call(
        whole,
        in_specs=[anywhere] * len(in_specs),
        out_specs=[anywhere] * len(out_specs),
        out_shape=out_shape,
        scratch_shapes=[pltpu.VMEM((tm, d), BF16)],
        input_output_aliases={0: 0} if in_place else {},
        compiler_params=pltpu.CompilerParams(vmem_limit_bytes=VMEM_LIMIT),
        name="ffn_tail",
    )(*args)
    return tail[0] if side_t is None else (tail[0], tail[1])


def _log_sigmoid(x):
    return jnp.minimum(x, 0.0) - jnp.log1p(jnp.exp(-jnp.abs(x)))


def _split3(x):
    hi = x.astype(BF16)
    r1 = x - hi.astype(F32)
    mid = r1.astype(BF16)
    lo = (r1 - mid.astype(F32)).astype(BF16)
    return hi, mid, lo


def _cumsum_rows(x, tri, head_lane):
    hi, mid, lo = (p.astype(F32) for p in _split3(x))
    packed = hi + pltpu.roll(mid, SUBLANES, axis=1) + pltpu.roll(lo, 2 * SUBLANES, axis=1)
    c = _dot(tri, packed.astype(BF16))
    c = c + pltpu.roll(c, LANES - SUBLANES, axis=1) + pltpu.roll(c, LANES - 2 * SUBLANES, axis=1)
    return jnp.where(head_lane, c, 0.0)


def _bias_columns(cum):
    hi, mid, lo = (p.astype(F32) for p in _split3(cum * BIAS_SCALE))
    roll = lambda x, g: pltpu.roll(x, g * SUBLANES, axis=1)
    cols = hi + roll(mid, 1) + roll(lo, 2) - roll(hi, 3) - roll(mid, 4) - roll(lo, 5)
    return cols.astype(BF16)


def _head_rmsnorm(z, gain, n_heads):
    outs = []
    for h in range(n_heads):
        outs.append(_rmsnorm_rows(z[:, h * HEAD_DIM:(h + 1) * HEAD_DIM], gain))
    return jnp.concatenate(outs, axis=-1)


def _window_means(e, w):
    assert w & (w - 1) == 0 and w <= N_META
    s, span = e, 1
    while span < w:
        s = s + pltpu.roll(s, span, axis=0)
        span *= 2
    return s[N_META:, :] / float(w)


def _inproj_kernel(ha_ref, hb_ref, hm_ref, gn_ref, w_ref, b_ref, qn_ref, kn_ref, pw_ref, ps_ref,
                   side_ref,
                   pool_ref, q_ref, k_ref, v_ref, aug_ref, km_ref, vm_ref, augm_ref, sideb_ref,
                   u_ref, pe_ref, pmeta_ref, ccol_ref, mcol_ref,
                   *, tm, tiles_a, tiles_per_batch, n_heads, d_pool, d_att, row_chunk):
    i = pl.program_id(0)
    gain = gn_ref[...]
    sideb_ref[...] = side_ref[...].astype(BF16)
    head_lane = lax.broadcasted_iota(jnp.int32, (1, LANES), 1) < n_heads
    w_pool = lambda: w_ref[:, 0:d_pool]
    w_q = lambda: w_ref[:, d_pool:d_pool + d_att]
    w_k = lambda: w_ref[:, d_pool + d_att:d_pool + 2 * d_att]
    w_vf = lambda: w_ref[:, d_pool + 2 * d_att:]

    def log_forget(zvf):
        lf = _log_sigmoid(zvf[:, d_att:] + b_ref[...])
        return jnp.where(head_lane, lf, 0.0)

    def tri(t):
        r_i = lax.broadcasted_iota(jnp.int32, (t, t), 0)
        c_i = lax.broadcasted_iota(jnp.int32, (t, t), 1)
        return (c_i <= r_i).astype(BF16)

    @pl.when(i == 0)
    def _():
        u_ref[0:META_PAD, :] = jnp.zeros((META_PAD, u_ref.shape[1]), BF16)
        u_ref[0:N_META, :] = _rmsnorm_rows(hm_ref[...], gain).astype(BF16)
        um = u_ref[0:META_PAD, :]
        row_valid = lax.broadcasted_iota(jnp.int32, (META_PAD, 1), 0) < N_META
        pmeta_ref[...] = _dot(um, w_pool())[0:N_META, :]
        km = _head_rmsnorm(_dot(um, w_k()), kn_ref[...], n_heads)
        km_ref[...] = jnp.where(row_valid, km, 0.0).astype(BF16)
        zvf = _dot(um, w_vf())
        vm_ref[...] = zvf[:, 0:d_att].astype(BF16)
        cum_m = _cumsum_rows(jnp.where(row_valid, log_forget(zvf), 0.0), tri(META_PAD), head_lane)
        augm_ref[...] = _bias_columns(cum_m)
        mcol_ref[...] = cum_m[META_PAD - 1:META_PAD, :]

    @pl.when(i % tiles_per_batch == 0)
    def _():
        ccol_ref[...] = mcol_ref[...]
        pe_ref[0:N_META, :] = pmeta_ref[...]

    for r in range(0, tm, row_chunk):
        h = jnp.where(i < tiles_a, ha_ref[r:r + row_chunk, :], hb_ref[r:r + row_chunk, :])
        u_ref[r:r + row_chunk, :] = _rmsnorm_rows(h, gain).astype(BF16)
    u = u_ref[0:tm, :]

    pe_ref[N_META:N_META + tm, :] = _dot(u, w_pool())
    q_ref[...] = _head_rmsnorm(_dot(u, w_q()), qn_ref[...], n_heads).astype(BF16)
    k_ref[...] = _head_rmsnorm(_dot(u, w_k()), kn_ref[...], n_heads).astype(BF16)
    zvf = _dot(u, w_vf())
    v_ref[...] = zvf[:, 0:d_att].astype(BF16)

    gw = d_pool // len(POOL_WINDOWS)
    for g, w in enumerate(POOL_WINDOWS):
        lanes = slice(g * gw, (g + 1) * gw)
        e = pe_ref[:, lanes]
        pooled = (_window_means(e, w) - e[N_META:, :]).astype(BF16)
        mixed = _dot(pooled, pw_ref[g]) * ps_ref[:, lanes]
        pool_ref[:, lanes] = mixed.astype(BF16)
    pe_ref[0:N_META, :] = pe_ref[tm:tm + N_META, :]

    cum_col = _cumsum_rows(log_forget(zvf), tri(tm), head_lane) + ccol_ref[...]
    aug_ref[...] = _bias_columns(cum_col)
    ccol_ref[...] = cum_col[tm - 1:tm, :]


def _const_spec(shape, index=None):
    nd = len(shape)
    index = (0,) * nd if index is None else index
    return pl.BlockSpec(shape, lambda i: index, pipeline_mode=pl.Buffered(1))


def _in_proj(h1, h1m, gn, w_in, b_gate, qn, kn, pw, ps, side, *, batch, d_pool, d_att, tm=512):
    h1a, h1b = h1
    d = h1a.shape[1]
    n = h1a.shape[0] + h1b.shape[0]
    assert h1a.shape[0] % tm == 0 and h1b.shape[0] % tm == 0
    tiles_a = h1a.shape[0] // tm
    seq = n // batch
    n_heads = d_att // HEAD_DIM
    assert seq % tm == 0 and tm >= META_PAD and d_pool == d_att and n_heads <= SUBLANES
    assert w_in.shape == (d, d_pool + 3 * d_att + LANES)
    tiles_per_batch = seq // tm
    side_rows = side.shape[0] // (n // tm)
    assert side_rows * (n // tm) == side.shape[0] and side_rows % SUBLANES == 0
    side_spec = pl.BlockSpec((side_rows, side.shape[1]), lambda i: (i, 0))
    rows = lambda w: pl.BlockSpec((tm, w), lambda i: (i, 0))
    in_specs = [
        pl.BlockSpec((tm, d), lambda i: (jnp.minimum(i, tiles_a - 1), 0)),
        pl.BlockSpec((tm, d), lambda i: (jnp.maximum(i - tiles_a, 0), 0)),
        _const_spec((N_META, d)),
        _const_spec((1, d)),
        _const_spec(w_in.shape),
        _const_spec((1, LANES)),
        _const_spec((1, HEAD_DIM)),
        _const_spec((1, HEAD_DIM)),
        _const_spec(pw.shape),
        _const_spec((1, d_pool)),
        side_spec,
    ]
    out_shape = [
        jax.ShapeDtypeStruct((n, d_pool), BF16),
        jax.ShapeDtypeStruct((n, d_att), BF16),
        jax.ShapeDtypeStruct((n, d_att), BF16),
        jax.ShapeDtypeStruct((n, d_att), BF16),
        jax.ShapeDtypeStruct((n, LANES), BF16),
        jax.ShapeDtypeStruct((META_PAD, d_att), BF16),
        jax.ShapeDtypeStruct((META_PAD, d_att), BF16),
        jax.ShapeDtypeStruct((META_PAD, LANES), BF16),
        jax.ShapeDtypeStruct(side.shape, BF16),
    ]
    out_specs = [
        rows(d_pool), rows(d_att), rows(d_att), rows(d_att), rows(LANES),
        pl.BlockSpec((META_PAD, d_att), lambda i: (0, 0)),
        pl.BlockSpec((META_PAD, d_att), lambda i: (0, 0)),
        pl.BlockSpec((META_PAD, LANES), lambda i: (0, 0)),
        side_spec,
    ]
    scratch = [
        pltpu.VMEM((tm, d), BF16),
        pltpu.VMEM((tm + N_META, d_pool), F32),
        pltpu.VMEM((N_META, d_pool), F32),
        pltpu.VMEM((1, LANES), F32),
        pltpu.VMEM((1, LANES), F32),
    ]
    kern = functools.partial(_inproj_kernel, tm=tm, tiles_a=tiles_a, tiles_per_batch=tiles_per_batch,
                             n_heads=n_heads, d_pool=d_pool, d_att=d_att, row_chunk=ROW_CHUNK)
    return pl.pallas_call(
        kern,
        grid=(n // tm,),
        in_specs=in_specs,
        out_specs=out_specs,
        out_shape=out_shape,
        scratch_shapes=scratch,
        compiler_params=_cparams(1),
        name="in_proj",
    )(h1a, h1b, h1m, gn, w_in, b_gate, qn, kn, pw, ps, side)


def _attn_kernel(q_ref, k_ref, v_ref, km_ref, vm_ref, aug_ref, augm_ref, wg_ref, wu_ref, wd_ref,
                 o_ref, wgub_ref, wdb_ref, qa_ref, ka_ref, va_ref, *, tq):
    h = pl.program_id(1)
    seq = q_ref.shape[0]
    lane = lax.broadcasted_iota(jnp.int32, (1, LANES), 1)
    group = lane >> (SUBLANES.bit_length() - 1)
    mine = (lane & (SUBLANES - 1)) == h
    one_hot = lambda lo: jnp.where(mine & (group >= lo) & (group < lo + 3), 1.0, 0.0).astype(BF16)
    keep = lambda lo: jnp.where((group >= lo) & (group < lo + 3), 0.0, 1.0).astype(BF16)
    qa_ref[:, 0:HEAD_DIM] = q_ref[...]
    qa_ref[:, HEAD_DIM:] = aug_ref[...] * keep(3) + one_hot(3)
    ka_ref[:, 0:HEAD_DIM] = k_ref[...]
    ka_ref[:, HEAD_DIM:] = aug_ref[...] * keep(0) + one_hot(0)
    kma = jnp.concatenate([km_ref[...], augm_ref[...] * keep(0) + one_hot(0)], axis=1)
    va_ref[:, 0:HEAD_DIM] = v_ref[...]
    va_ref[:, HEAD_DIM:] = jnp.ones((seq, HEAD_DIM), BF16)
    vma = jnp.concatenate([vm_ref[...], jnp.ones((META_PAD, HEAD_DIM), BF16)], axis=1)

    hq = tq // 2
    rows_of = lambda w: lax.broadcasted_iota(jnp.int32, (hq, w), 0)
    cols_of = lambda w: lax.broadcasted_iota(jnp.int32, (hq, w), 1)
    causal_top = cols_of(hq) <= rows_of(hq)
    causal_bot = cols_of(tq) <= rows_of(tq) + hq
    tiles = list(reversed(range(seq // tq)))
    scores = {}
    for qi in tiles:
        rows = slice(qi * tq, (qi + 1) * tq)
        q = qa_ref[rows, :]
        blocks = [slice(j * tq, (j + 1) * tq) for j in range(qi)]
        raw_m = jnp.where(lane < N_META, _dot_nt(q, kma), NEG_INF)
        raws = [_dot_nt(q, ka_ref[cols, :]) for cols in blocks]
        raw_t = jnp.where(causal_top, _dot_nt(q[0:hq], ka_ref[qi * tq:qi * tq + hq, :]), NEG_INF)
        raw_b = jnp.where(causal_bot, _dot_nt(q[hq:], ka_ref[rows, :]), NEG_INF)
        scores[qi] = (rows, blocks, raw_m, raws, raw_t, raw_b)
    probs = {}
    for qi in tiles:
        rows, blocks, raw_m, raws, raw_t, raw_b = scores[qi]
        halves = []
        for r0, raw_d in ((0, raw_t), (hq, raw_b)):
            part = slice(r0, r0 + hq)
            full = [raw_m[part]] + [r[part] for r in raws]
            m = jnp.maximum(jnp.max(full[0], axis=-1, keepdims=True),
                            jnp.max(raw_d, axis=-1, keepdims=True))
            if raws:
                m = jnp.maximum(m, jnp.max(functools.reduce(jnp.maximum, full[1:]),
                                           axis=-1, keepdims=True))
            halves.append([jnp.exp2(EXP2_SCALE * (r - m)) for r in full + [raw_d]])
        top, bot = halves
        p_m = jnp.concatenate([top[0], bot[0]], axis=0)
        ps = [jnp.concatenate([t, b], axis=0) for t, b in zip(top[1:-1], bot[1:-1])]
        probs[qi] = (rows, blocks, p_m, ps, top[-1], bot[-1])
    for c in range(0, wg_ref.shape[1], PAIR):
        wgub_ref[:, 2 * c:2 * c + PAIR] = wg_ref[:, c:c + PAIR].astype(BF16)
        wgub_ref[:, 2 * c + PAIR:2 * c + 2 * PAIR] = wu_ref[:, c:c + PAIR].astype(BF16)
    wdb_ref[...] = (0.5 * wd_ref[...]).astype(BF16)
    for qi in tiles:
        rows, blocks, p_m, ps, p_t, p_b = probs[qi]
        acc = _dot(p_m.astype(BF16), vma)
        for p, cols in zip(ps, blocks):
            acc = acc + _dot(p.astype(BF16), va_ref[cols, :])
        acc = acc + jnp.concatenate(
            [_dot(p_t.astype(BF16), va_ref[qi * tq:qi * tq + hq, :]),
             _dot(p_b.astype(BF16), va_ref[rows, :])], axis=0)
        o_ref[rows, :] = (acc[:, 0:HEAD_DIM] / acc[:, HEAD_DIM:]).astype(BF16)


def _attention(q, k, v, km, vm, aug, augm, next_ffn, *, batch, tq=512):
    n, d_att = q.shape
    seq = n // batch
    n_heads = d_att // HEAD_DIM
    steps = batch * n_heads
    tok = pl.BlockSpec((seq, HEAD_DIM), lambda b, h: (b, h))
    meta = pl.BlockSpec((META_PAD, HEAD_DIM), lambda b, h: (0, h))

    def chunk(w):
        assert w.shape[0] % (steps * 2 * SUBLANES) == 0
        return pl.BlockSpec((w.shape[0] // steps, w.shape[1]), lambda b, h: (b * n_heads + h, 0))

    wg, wu, wd = next_ffn
    side_in = [chunk(w) for w in next_ffn]
    side_out = [pl.BlockSpec((wg.shape[0] // steps, 2 * wg.shape[1]),
                             lambda b, h: (b * n_heads + h, 0)), chunk(wd)]
    return pl.pallas_call(
        functools.partial(_attn_kernel, tq=tq),
        grid=(batch, n_heads),
        in_specs=[
            tok, tok, tok, meta, meta,
            pl.BlockSpec((seq, LANES), lambda b, h: (b, 0)),
            pl.BlockSpec((META_PAD, LANES), lambda b, h: (0, 0)),
        ] + side_in,
        out_specs=[tok] + side_out,
        out_shape=[jax.ShapeDtypeStruct((n, d_att), BF16),
                   jax.ShapeDtypeStruct((wg.shape[0], 2 * wg.shape[1]), BF16),
                   jax.ShapeDtypeStruct(wd.shape, BF16)],
        scratch_shapes=[pltpu.VMEM((seq, 2 * HEAD_DIM), BF16),
                        pltpu.VMEM((seq, 2 * HEAD_DIM), BF16),
                        pltpu.VMEM((seq, 2 * HEAD_DIM), BF16)],
        compiler_params=_cparams(2),
        name="fox_attn",
    )(q, k, v, km, vm, aug, augm, *next_ffn)


def _outproj_kernel(ha_ref, hb_ref, pool_ref, att_ref, wop_ref, woa_ref, o_ref, *, tiles_a,
                    col_chunk):
    i = pl.program_id(0)
    d = o_ref.shape[1]
    for c in range(0, d, col_chunk):
        cols = slice(c, c + col_chunk)
        h = jnp.where(i < tiles_a, ha_ref[:, cols], hb_ref[:, cols])
        o_ref[:, cols] = (h + _dot(pool_ref[...], wop_ref[:, cols])
                          + _dot(att_ref[...], woa_ref[:, cols]))


def _out_proj(h1, pool, att, w_out, *, tm=512):
    h1a, h1b = h1
    d = h1a.shape[1]
    n = h1a.shape[0] + h1b.shape[0]
    assert h1a.shape[0] % tm == 0 and h1b.shape[0] % tm == 0
    tiles_a = h1a.shape[0] // tm
    d_pool, d_att = pool.shape[1], att.shape[1]
    assert d_pool == d_att
    rows = lambda w: pl.BlockSpec((tm, w), lambda i: (i, 0))
    return pl.pallas_call(
        functools.partial(_outproj_kernel, tiles_a=tiles_a, col_chunk=COL_CHUNK),
        grid=(n // tm,),
        in_specs=[pl.BlockSpec((tm, d), lambda i: (jnp.minimum(i, tiles_a - 1), 0)),
                  pl.BlockSpec((tm, d), lambda i: (jnp.maximum(i - tiles_a, 0), 0)),
                  rows(d_pool), rows(d_att),
                  _const_spec((d_pool, d), (0, 0)), _const_spec((d_att, d), (1, 0))],
        out_specs=rows(d),
        out_shape=jax.ShapeDtypeStruct((n, d), F32),
        compiler_params=_cparams(1),
        name="out_proj",
    )(h1a, h1b, pool, att, w_out, w_out)


def kernel(x, meta_tokens, ffn1_norm, ffn1_w_gate, ffn1_w_up, ffn1_w_down, mix_norm, w_in,
           b_forget, q_norm, k_norm, pool_w, pool_scale, w_out, ffn2_norm, ffn2_w_gate,
           ffn2_w_up, ffn2_w_down):
    batch, seq, d = x.shape
    n_heads = b_forget.shape[1]
    d_att = n_heads * HEAD_DIM
    d_pool = pool_scale.shape[1]
    assert meta_tokens.shape[0] == N_META
    assert ffn1_norm.shape[0] == 1, "one layer: the meta rows past the mixers never reach the output"

    hx = x.reshape(batch * seq, d)
    hm = meta_tokens.astype(x.dtype)
    g1 = ffn1_norm[0][None]
    h1_head, hm, w1_b = _ffn_head(hx, hm, g1, ffn1_w_gate[0], ffn1_w_up[0], ffn1_w_down[0])
    h1_tail, w_in_b = _ffn_tail(hx, g1, *w1_b, first_tile=1, in_place=False, side_t=w_in[0].T)
    h1 = (h1_head, h1_tail)

    b_gate = jnp.pad(b_forget[0][None], ((0, 0), (0, LANES - n_heads)))
    pool, q, k, v, aug, km, vm, augm, w_out_b = _in_proj(
        h1, hm, mix_norm[0][None], w_in_b, b_gate, q_norm[0][None], k_norm[0][None],
        pool_w[0].astype(BF16), pool_scale[0][None], w_out[0],
        batch=batch, d_pool=d_pool, d_att=d_att)

    att, *w2_b = _attention(q, k, v, km, vm, aug, augm,
                            (ffn2_w_gate[0], ffn2_w_up[0], ffn2_w_down[0]), batch=batch)
    h2 = _out_proj(h1, pool, att, w_out_b)

    out = _ffn_tail(h2, ffn2_norm[0][None], *w2_b, first_tile=0, in_place=True)
    return out.reshape(batch, seq, d)
```
